```python
import math
import jax, jax.numpy as jnp
from jax import lax
import numpy as np


D_MODEL = 2048
BATCH = 32
SEQ = 256
DEPTH = 2
DEC_BATCH = 8
DEC_SEQ = 1024
PAST_LEN = 512

GRID_W = 64
Q_BLOCK = 128
ROPE_BASE = 10000.0
EPS = 1e-6
SSM_CH = 512
SSM_GROUP = 16
SSM_GROUPS = SSM_CH // SSM_GROUP
SSM_STATE = 64
GQA_HEADS = 6
GQA_KV_HEADS = 2
GQA_GROUP = GQA_HEADS // GQA_KV_HEADS
GQA_HEAD_DIM = 128
MLA_HEADS = 6
MLA_NOPE = 128
MLA_ROPE = 64
MLA_QK = MLA_NOPE + MLA_ROPE
MLA_V = 128
MLA_KV_RANK = 512
D_FF = 4 * D_MODEL
A_U_W = SSM_CH
B_Q_W = GQA_HEADS * GQA_HEAD_DIM
B_KV_W = GQA_KV_HEADS * GQA_HEAD_DIM
C_Q_W = MLA_HEADS * MLA_QK
IN_SPLITS = (A_U_W, A_U_W + B_Q_W, A_U_W + B_Q_W + B_KV_W, A_U_W + B_Q_W + 2 * B_KV_W, A_U_W + B_Q_W + 2 * B_KV_W + C_Q_W, A_U_W + B_Q_W + 2 * B_KV_W + C_Q_W + MLA_KV_RANK)
IN_WIDTH = IN_SPLITS[-1] + MLA_ROPE
MIX_WIDTH = SSM_CH + GQA_HEADS * GQA_HEAD_DIM + MLA_HEADS * MLA_V

kernel_name = 'hybrid_s5_gqa_mla_prefix_dit_step'


def rms_norm(x, g):
    xf = x.astype(jnp.float32)
    y = xf * lax.rsqrt(jnp.mean(xf * xf, axis=-1, keepdims=True) + EPS)
    return (y * g.astype(jnp.float32)).astype(x.dtype)


def modulate(h, shift, scale):
    return h * (1.0 + scale) + shift


def adaln(cvec, w, b):
    m = (jax.nn.silu(cvec) @ w + b)[:, None, :]
    return jnp.split(m, 6, axis=-1)


def axial_rope(x):
    t, d = x.shape[1], x.shape[-1]
    rows = t // GRID_W
    half = d // 2
    quarter = half // 2
    row = jnp.repeat(jnp.arange(rows, dtype=jnp.float32), GRID_W)
    col = jnp.tile(jnp.arange(GRID_W, dtype=jnp.float32), rows)
    inv = ROPE_BASE ** (-(jnp.arange(quarter, dtype=jnp.float32) / quarter))
    xf = x.astype(jnp.float32)

    def rotate(xa, pos):
        ang = pos[:, None] * inv[None, :]
        cos = jnp.cos(ang)[None, :, None, :]
        sin = jnp.sin(ang)[None, :, None, :]
        x1, x2 = xa[..., :quarter], xa[..., quarter:]
        return jnp.concatenate([x1 * cos - x2 * sin, x2 * cos + x1 * sin], axis=-1)

    out = jnp.concatenate([rotate(xf[..., :half], row), rotate(xf[..., half:], col)], axis=-1)
    return out.astype(x.dtype)


def rope_tail(x):
    return jnp.concatenate([x[..., :MLA_NOPE], axial_rope(x[..., MLA_NOPE:])], axis=-1)


def blocked_attention(q, k, v):
    bsz, s = q.shape[0], q.shape[1]
    nb = s // Q_BLOCK
    scale = 1.0 / math.sqrt(q.shape[-1])
    kf = k.astype(jnp.float32)
    vf = v.astype(jnp.float32)
    qb = q.reshape((bsz, nb, Q_BLOCK) + q.shape[2:]).swapaxes(0, 1)

    def one_block(qblk):
        sc = jnp.einsum('bqhgd,bkhd->bhgqk', qblk.astype(jnp.float32), kf) * scale
        pr = jax.nn.softmax(sc, axis=-1)
        return jnp.einsum('bhgqk,bkhd->bqhgd', pr, vf).astype(q.dtype)

    out = lax.map(one_block, qb)
    return out.swapaxes(0, 1).reshape((bsz, s) + out.shape[3:])


def linear_scan(lam_bar, bu, h0, reverse):
    if h0 is not None:
        edge = bu.shape[1] - 1 if reverse else 0
        bu = bu.at[:, edge].add(lam_bar[None] * h0)
    a = jnp.broadcast_to(lam_bar, bu.shape)

    def combine(e1, e2):
        a1, b1 = e1
        a2, b2 = e2
        return a1 * a2, a2 * b1 + b2

    _, states = lax.associative_scan(combine, (a, bu), reverse=reverse, axis=1)
    return states


def s5_bidirectional(u, lam_re, lam_im, log_dt, b_re, b_im, c_re, c_im, d_skip, w_glu, h0_f, h0_b):
    bsz, t = u.shape[0], u.shape[1]
    uf = u.astype(jnp.float32).reshape(bsz, t, SSM_GROUPS, SSM_GROUP)
    uc = uf.astype(jnp.complex64)
    y = d_skip.astype(jnp.float32).reshape(SSM_GROUPS, SSM_GROUP) * uf
    finals = []
    for dr, (h0, reverse) in enumerate(((h0_f, False), (h0_b, True))):
        lam = lax.complex(lam_re[dr].astype(jnp.float32), lam_im[dr].astype(jnp.float32))
        dt = jnp.exp(log_dt[dr].astype(jnp.float32))[:, None]
        lam_bar = jnp.exp(lam * dt)
        b_mat = lax.complex(b_re[dr].astype(jnp.float32), b_im[dr].astype(jnp.float32))
        b_bar = ((lam_bar - 1.0) / lam)[..., None] * b_mat
        bu = jnp.einsum('btgc,gpc->btgp', uc, b_bar)
        states = linear_scan(lam_bar, bu, h0, reverse)
        c_mat = lax.complex(c_re[dr].astype(jnp.float32), c_im[dr].astype(jnp.float32))
        y = y + jnp.real(jnp.einsum('gcp,btgp->btgc', c_mat, states))
        finals.append(states[:, 0] if reverse else states[:, -1])
    y = y.reshape(bsz, t, SSM_CH).astype(u.dtype)
    zg = y @ w_glu
    out = zg[..., :SSM_CH] * jax.nn.sigmoid(zg[..., SSM_CH:])
    return out, finals[0], finals[1]


def mla_expand(ckv_n, kr, w_uk, w_uv, k_norm):
    k_nope = jnp.einsum('btr,rhd->bthd', ckv_n, w_uk)
    v = jnp.einsum('btr,rhd->bthd', ckv_n, w_uv)
    k_rope = jnp.broadcast_to(kr[:, :, None, :], kr.shape[:2] + (MLA_HEADS, MLA_ROPE))
    k = rms_norm(jnp.concatenate([k_nope, k_rope], axis=-1), k_norm)
    return k, v


def trunk_layer(x, mod, p, ctx):
    latent = ctx is not None
    shift1, scale1, gate1, shift2, scale2, gate2 = mod
    bsz, t = x.shape[0], x.shape[1]
    h = modulate(rms_norm(x, p['norm_mix']), shift1, scale1)
    z = h @ p['w_in']
    u, qb, kb, vb, qc, ckv, kr = jnp.split(z, IN_SPLITS, axis=-1)

    if latent:
        s = ctx['ssm']
        h0 = lax.complex(s[..., 0].astype(jnp.float32), s[..., 1].astype(jnp.float32))
        h0_f, h0_b = h0[:, 0], h0[:, 1]
    else:
        h0_f, h0_b = None, None
    y_a, hf, hb = s5_bidirectional(u, p['ssm_lam_re'], p['ssm_lam_im'], p['ssm_log_dt'], p['ssm_b_re'], p['ssm_b_im'], p['ssm_c_re'], p['ssm_c_im'], p['ssm_d'], p['ssm_w_glu'], h0_f, h0_b)

    q = rms_norm(qb.reshape(bsz, t, GQA_HEADS, GQA_HEAD_DIM), p['gqa_q_norm'])
    k = rms_norm(kb.reshape(bsz, t, GQA_KV_HEADS, GQA_HEAD_DIM), p['gqa_k_norm'])
    v = vb.reshape(bsz, t, GQA_KV_HEADS, GQA_HEAD_DIM)
    if latent:
        q_b = axial_rope(q)
        k_b = jnp.concatenate([ctx['k'].astype(k.dtype), axial_rope(k)], axis=1)
        v_b = jnp.concatenate([ctx['v'].astype(v.dtype), v], axis=1)
    else:
        q_b, k_b, v_b = q, k, v
    y_b = blocked_attention(q_b.reshape(bsz, t, GQA_KV_HEADS, GQA_GROUP, GQA_HEAD_DIM), k_b, v_b)
    y_b = y_b.reshape(bsz, t, GQA_HEADS * GQA_HEAD_DIM)

    ckv_n = rms_norm(ckv, p['mla_kv_norm'])
    q_c = rms_norm(qc.reshape(bsz, t, MLA_HEADS, MLA_QK), p['mla_q_norm'])
    k_c, v_c = mla_expand(ckv_n, kr, p['mla_w_uk'], p['mla_w_uv'], p['mla_k_norm'])
    if latent:
        q_c = rope_tail(q_c)
        k_ctx, v_ctx = mla_expand(ctx['ckv'].astype(ckv_n.dtype), ctx['kr'].astype(kr.dtype), p['mla_w_uk'], p['mla_w_uv'], p['mla_k_norm'])
        k_c = jnp.concatenate([k_ctx, rope_tail(k_c)], axis=1)
        v_c = jnp.concatenate([v_ctx, v_c], axis=1)
    y_c = blocked_attention(q_c[:, :, :, None, :], k_c, v_c).reshape(bsz, t, MLA_HEADS * MLA_V)

    o = jnp.concatenate([y_a, y_b, y_c], axis=-1) @ p['w_out']
    x = x + gate1 * o
    h2 = modulate(rms_norm(x, p['norm_mlp']), shift2, scale2)
    x = x + gate2 * (jnp.square(jax.nn.relu(h2 @ p['w_ff1'])) @ p['w_ff2'])
    new_ctx = None if latent else (k, v, ckv_n, kr, hf, hb)
    return x, new_ctx


def setup_inputs(seed: int = 0) -> dict:
    key = jax.random.key(seed)
    ks = iter(jax.random.split(key, 48))
    f32 = jnp.float32

    def nrm(shape, scale):
        return scale * jax.random.normal(next(ks), shape, f32)

    def gain(shape):
        return 1.0 + 0.05 * jax.random.normal(next(ks), shape, f32)

    ssm_shape = (DEPTH, 2, SSM_GROUPS, SSM_STATE)
    return {
        'x_prompt': nrm((BATCH, SEQ, D_MODEL), 1.0),
        'x_sample': nrm((DEC_BATCH, DEC_SEQ, D_MODEL), 1.0),
        'cache_attn_k': nrm((DEC_BATCH, DEPTH, PAST_LEN, GQA_KV_HEADS, GQA_HEAD_DIM), 1.0),
        'cache_attn_v': nrm((DEC_BATCH, DEPTH, PAST_LEN, GQA_KV_HEADS, GQA_HEAD_DIM), 1.0),
        'cache_mla_ckv': nrm((DEC_BATCH, DEPTH, PAST_LEN, MLA_KV_RANK), 1.0),
        'cache_mla_krope': nrm((DEC_BATCH, DEPTH, PAST_LEN, MLA_ROPE), 1.0),
        'state_ssm': nrm((DEC_BATCH, DEPTH, 2, SSM_GROUPS, SSM_STATE, 2), 0.3),
        'c': nrm((DEC_BATCH, D_MODEL), 1.0),
        'c_ctx': nrm((D_MODEL,), 1.0),
        'w_mod': nrm((DEPTH, D_MODEL, 6 * D_MODEL), D_MODEL ** -0.5),
        'b_mod': nrm((DEPTH, 6 * D_MODEL), 0.02),
        'norm_mix': gain((DEPTH, D_MODEL)),
        'norm_mlp': gain((DEPTH, D_MODEL)),
        'w_in': nrm((DEPTH, D_MODEL, IN_WIDTH), D_MODEL ** -0.5),
        'gqa_q_norm': gain((DEPTH, GQA_HEAD_DIM)),
        'gqa_k_norm': gain((DEPTH, GQA_HEAD_DIM)),
        'mla_kv_norm': gain((DEPTH, MLA_KV_RANK)),
        'mla_q_norm': gain((DEPTH, MLA_QK)),
        'mla_k_norm': gain((DEPTH, MLA_QK)),
        'mla_w_uk': nrm((DEPTH, MLA_KV_RANK, MLA_HEADS, MLA_NOPE), MLA_KV_RANK ** -0.5),
        'mla_w_uv': nrm((DEPTH, MLA_KV_RANK, MLA_HEADS, MLA_V), MLA_KV_RANK ** -0.5),
        'ssm_lam_re': -0.5 * jnp.exp(nrm(ssm_shape, 0.02)),
        'ssm_lam_im': math.pi * jnp.arange(SSM_STATE, dtype=f32) + nrm(ssm_shape, 0.02),
        'ssm_log_dt': jax.random.uniform(next(ks), (DEPTH, 2, SSM_GROUPS), f32, math.log(1e-3), math.log(1e-1)),
        'ssm_b_re': nrm((DEPTH, 2, SSM_GROUPS, SSM_STATE, SSM_GROUP), SSM_GROUP ** -0.5),
        'ssm_b_im': nrm((DEPTH, 2, SSM_GROUPS, SSM_STATE, SSM_GROUP), SSM_GROUP ** -0.5),
        'ssm_c_re': nrm((DEPTH, 2, SSM_GROUPS, SSM_GROUP, SSM_STATE), SSM_STATE ** -0.5),
        'ssm_c_im': nrm((DEPTH, 2, SSM_GROUPS, SSM_GROUP, SSM_STATE), SSM_STATE ** -0.5),
        'ssm_d': nrm((DEPTH, SSM_CH), 1.0),
        'ssm_w_glu': nrm((DEPTH, SSM_CH, 2 * SSM_CH), SSM_CH ** -0.5),
        'w_out': nrm((DEPTH, MIX_WIDTH, D_MODEL), MIX_WIDTH ** -0.5),
        'w_ff1': nrm((DEPTH, D_MODEL, D_FF), D_MODEL ** -0.5),
        'w_ff2': nrm((DEPTH, D_FF, D_MODEL), D_FF ** -0.5),
    }


def reference(x_prompt, x_sample, cache_attn_k, cache_attn_v, cache_mla_ckv, cache_mla_krope, state_ssm, c, c_ctx, w_mod, b_mod, norm_mix, norm_mlp, w_in, gqa_q_norm, gqa_k_norm, mla_kv_norm, mla_q_norm, mla_k_norm, mla_w_uk, mla_w_uv, ssm_lam_re, ssm_lam_im, ssm_log_dt, ssm_b_re, ssm_b_im, ssm_c_re, ssm_c_im, ssm_d, ssm_w_glu, w_out, w_ff1, w_ff2):
    xp = x_prompt
    xs = x_sample
    new_k, new_v, new_ckv, new_kr, new_ssm = [], [], [], [], []
    for l in range(DEPTH):
        p = {
            'norm_mix': norm_mix[l], 'norm_mlp': norm_mlp[l], 'w_in': w_in[l], 'w_out': w_out[l],
            'w_ff1': w_ff1[l], 'w_ff2': w_ff2[l],
            'gqa_q_norm': gqa_q_norm[l], 'gqa_k_norm': gqa_k_norm[l],
            'mla_kv_norm': mla_kv_norm[l], 'mla_q_norm': mla_q_norm[l], 'mla_k_norm': mla_k_norm[l],
            'mla_w_uk': mla_w_uk[l], 'mla_w_uv': mla_w_uv[l],
            'ssm_lam_re': ssm_lam_re[l], 'ssm_lam_im': ssm_lam_im[l], 'ssm_log_dt': ssm_log_dt[l],
            'ssm_b_re': ssm_b_re[l], 'ssm_b_im': ssm_b_im[l], 'ssm_c_re': ssm_c_re[l], 'ssm_c_im': ssm_c_im[l],
            'ssm_d': ssm_d[l], 'ssm_w_glu': ssm_w_glu[l],
        }
        mod_ctx = adaln(c_ctx[None, :], w_mod[l], b_mod[l])
        mod_lat = adaln(c, w_mod[l], b_mod[l])
        xp, (k, v, ckv_n, kr, hf, hb) = trunk_layer(xp, mod_ctx, p, None)
        new_k.append(k)
        new_v.append(v)
        new_ckv.append(ckv_n)
        new_kr.append(kr)
        hs = jnp.stack([hf, hb], axis=1)
        new_ssm.append(jnp.stack([jnp.real(hs), jnp.imag(hs)], axis=-1))
        ctx = {'k': cache_attn_k[:, l], 'v': cache_attn_v[:, l], 'ckv': cache_mla_ckv[:, l], 'kr': cache_mla_krope[:, l], 'ssm': state_ssm[:, l]}
        xs, _ = trunk_layer(xs, mod_lat, p, ctx)
    return (xp, xs, jnp.stack(new_k, axis=1), jnp.stack(new_v, axis=1), jnp.stack(new_ckv, axis=1), jnp.stack(new_kr, axis=1), jnp.stack(new_ssm, axis=1))
```

```python
import functools
import math

import numpy as np
import jax
import jax.numpy as jnp
from jax import lax
from jax.experimental import pallas as pl
from jax.experimental.pallas import tpu as pltpu

D_MODEL = 2048
BATCH = 32
SEQ = 256
DEPTH = 2
DEC_BATCH = 8
DEC_SEQ = 1024
PAST_LEN = 512
GRID_W = 64
ROPE_BASE = 10000.0
EPS = 1e-6
SSM_CH = 512
SSM_GROUP = 16
SSM_GROUPS = SSM_CH // SSM_GROUP
SSM_STATE = 64
SSM_N = SSM_GROUPS * SSM_STATE
GQA_HEADS = 6
GQA_KV_HEADS = 2
GQA_GROUP = GQA_HEADS // GQA_KV_HEADS
GQA_HEAD_DIM = 128
MLA_HEADS = 6
MLA_NOPE = 128
MLA_ROPE = 64
MLA_QK = MLA_NOPE + MLA_ROPE
MLA_V = 128
MLA_KV_RANK = 512
D_FF = 4 * D_MODEL
N_MOD = 16

OFF_U = 0
OFF_QB = OFF_U + SSM_CH
OFF_KB = OFF_QB + GQA_HEADS * GQA_HEAD_DIM
OFF_VB = OFF_KB + GQA_KV_HEADS * GQA_HEAD_DIM
OFF_QCN = OFF_VB + GQA_KV_HEADS * GQA_HEAD_DIM
OFF_QCR = OFF_QCN + MLA_HEADS * MLA_NOPE
OFF_CKV = OFF_QCR + MLA_HEADS * MLA_ROPE
OFF_KR = OFF_CKV + MLA_KV_RANK
IN_WIDTH_P = OFF_KR + 2 * MLA_ROPE

LANES = 128
SUBLANES = 8
VMEM_LIMIT = 56 * 1024 * 1024

BF16 = jnp.bfloat16
F32 = jnp.float32


def _cparams(sem):
    return pltpu.CompilerParams(dimension_semantics=sem, vmem_limit_bytes=VMEM_LIMIT)


def _dot(a, b):
    return jnp.dot(a, b, preferred_element_type=F32)


def _dot_nt(a, b):
    return lax.dot_general(a, b, (((1,), (1,)), ((), ())), preferred_element_type=F32)


def _adaln_kernel(c_ref, w_ref, b_ref, o_ref):
    c = c_ref[...]
    s = (c * jax.nn.sigmoid(c)).astype(BF16)
    o_ref[...] = _dot(s, w_ref[...].astype(BF16)) + b_ref[...]


def _adaln(cvec, w_mod, b_mod):
    tn = 1024
    return pl.pallas_call(
        _adaln_kernel,
        out_shape=jax.ShapeDtypeStruct((DEPTH, N_MOD, 6 * D_MODEL), F32),
        grid=(DEPTH, 6 * D_MODEL // tn),
        in_specs=[
            pl.BlockSpec((N_MOD, D_MODEL), lambda l, j: (0, 0)),
            pl.BlockSpec((None, D_MODEL, tn), lambda l, j: (l, 0, j)),
            pl.BlockSpec((None, 1, tn), lambda l, j: (l, 0, j)),
        ],
        out_specs=pl.BlockSpec((None, N_MOD, tn), lambda l, j: (l, 0, j)),
        compiler_params=_cparams(("parallel", "parallel")),
        name="adaln",
    )(cvec, w_mod, b_mod)


def _swap_halves(x, block):
    lane = lax.broadcasted_iota(jnp.int32, x.shape, 1)
    first = (lane % (2 * block)) < block
    return jnp.where(first, pltpu.roll(x, LANES - block, 1), pltpu.roll(x, block, 1))


def _rope(x, cos, sin_signed, block):
    return x * cos + _swap_halves(x, block) * sin_signed


def _in_proj_kernel(latent, x_ref, mod_ref, gmix_ref, w_ref, gq_ref, gk_ref, gkv_ref, gqn_ref, gqr_ref, gkr_ref,
                    cosb_ref, sinb_ref, cosc_ref, sinc_ref,
                    u_ref, qb_ref, kb_ref, vb_ref, qcn_ref, qcr_ref, ckv_ref, kr_ref, krg_ref):
    x = x_ref[...]
    ms = jnp.mean(x * x, axis=-1, keepdims=True)
    y = x * lax.rsqrt(ms + EPS) * gmix_ref[...]
    h = (y * (1.0 + mod_ref[1:2, :]) + mod_ref[0:1, :]).astype(BF16)

    def proj(off, width):
        return _dot(h, w_ref[:, off:off + width])

    u_ref[...] = proj(OFF_U, SSM_CH)

    scale_b = 1.0 / math.sqrt(GQA_HEAD_DIM)
    zq = proj(OFF_QB, GQA_HEADS * GQA_HEAD_DIM)
    for hd in range(GQA_HEADS):
        col = zq[:, hd * LANES:(hd + 1) * LANES]
        q = col * lax.rsqrt(jnp.mean(col * col, axis=-1, keepdims=True) + EPS) * gq_ref[...]
        if latent:
            q = _rope(q, cosb_ref[...], sinb_ref[...], GQA_HEAD_DIM // 4)
        qb_ref[:, hd * LANES:(hd + 1) * LANES] = (q * scale_b).astype(qb_ref.dtype)
    zk = proj(OFF_KB, GQA_KV_HEADS * GQA_HEAD_DIM)
    for hd in range(GQA_KV_HEADS):
        col = zk[:, hd * LANES:(hd + 1) * LANES]
        k = col * lax.rsqrt(jnp.mean(col * col, axis=-1, keepdims=True) + EPS) * gk_ref[...]
        if latent:
            k = _rope(k, cosb_ref[...], sinb_ref[...], GQA_HEAD_DIM // 4)
        kb_ref[:, hd * LANES:(hd + 1) * LANES] = k.astype(kb_ref.dtype)
    vb_ref[...] = proj(OFF_VB, GQA_KV_HEADS * GQA_HEAD_DIM).astype(vb_ref.dtype)

    scale_c = 1.0 / math.sqrt(MLA_QK)
    zn = proj(OFF_QCN, MLA_HEADS * MLA_NOPE)
    zr = proj(OFF_QCR, MLA_HEADS * MLA_ROPE)
    lane = lax.broadcasted_iota(jnp.int32, (x.shape[0], LANES), 1)
    low = lane < MLA_ROPE
    for pair in range(MLA_HEADS // 2):
        colr = zr[:, pair * LANES:(pair + 1) * LANES]
        sq = colr * colr
        ss_lo = jnp.sum(jnp.where(low, sq, 0.0), axis=-1, keepdims=True)
        ss_hi = jnp.sum(jnp.where(low, 0.0, sq), axis=-1, keepdims=True)
        rs = []
        for half, ss_r in ((0, ss_lo), (1, ss_hi)):
            hd = 2 * pair + half
            coln = zn[:, hd * LANES:(hd + 1) * LANES]
            ss = jnp.sum(coln * coln, axis=-1, keepdims=True) + ss_r
            r = lax.rsqrt(ss * (1.0 / MLA_QK) + EPS)
            rs.append(r)
            qcn_ref[:, hd * LANES:(hd + 1) * LANES] = (coln * r * gqn_ref[...] * scale_c).astype(qcn_ref.dtype)
        qr = colr * jnp.where(low, rs[0], rs[1]) * gqr_ref[...]
        if latent:
            qr = _rope(qr, cosc_ref[...], sinc_ref[...], MLA_ROPE // 4)
        qr = qr * scale_c
        qcr_ref[2 * pair] = qr[:, :MLA_ROPE].astype(qcr_ref.dtype)
        qcr_ref[2 * pair + 1] = qr[:, MLA_ROPE:].astype(qcr_ref.dtype)

    zc = proj(OFF_CKV, MLA_KV_RANK)
    ckv_ref[...] = (zc * lax.rsqrt(jnp.mean(zc * zc, axis=-1, keepdims=True) + EPS) * gkv_ref[...]).astype(ckv_ref.dtype)
    zkr = proj(OFF_KR, 2 * MLA_ROPE)
    kr_ref[...] = zkr[:, :MLA_ROPE]
    krg = zkr * gkr_ref[...]
    if latent:
        krg = _rope(krg, cosc_ref[...], sinc_ref[...], MLA_ROPE // 4)
    krg_ref[...] = krg[:, :MLA_ROPE]


def _in_proj(x2d, mod_l, latent, lw, tabs, tm):
    rows = x2d.shape[0]
    n_tiles = rows // tm
    tiles_per_seq = DEC_SEQ // tm
    if latent:
        mod_map = lambda i: (1 + i // tiles_per_seq, 0, 0)
        tab_map = lambda i: (i % tiles_per_seq, 0)
    else:
        mod_map = lambda i: (0, 0, 0)
        tab_map = lambda i: (0, 0)
    full = lambda i: (0, 0)
    row = lambda i: (i, 0)
    act_dt = BF16 if latent else F32
    in_specs = [
        pl.BlockSpec((tm, D_MODEL), row),
        pl.BlockSpec((None, 6, D_MODEL), mod_map),
        pl.BlockSpec((1, D_MODEL), full),
        pl.BlockSpec((D_MODEL, IN_WIDTH_P), full),
        pl.BlockSpec((1, LANES), full),
        pl.BlockSpec((1, LANES), full),
        pl.BlockSpec((1, MLA_KV_RANK), full),
        pl.BlockSpec((1, LANES), full),
        pl.BlockSpec((1, LANES), full),
        pl.BlockSpec((1, LANES), full),
        pl.BlockSpec((tm, LANES), tab_map),
        pl.BlockSpec((tm, LANES), tab_map),
        pl.BlockSpec((tm, LANES), tab_map),
        pl.BlockSpec((tm, LANES), tab_map),
    ]
    out_shape = (
        jax.ShapeDtypeStruct((rows, SSM_CH), F32),
        jax.ShapeDtypeStruct((rows, GQA_HEADS * GQA_HEAD_DIM), BF16),
        jax.ShapeDtypeStruct((rows, GQA_KV_HEADS * GQA_HEAD_DIM), act_dt),
        jax.ShapeDtypeStruct((rows, GQA_KV_HEADS * GQA_HEAD_DIM), act_dt),
        jax.ShapeDtypeStruct((rows, MLA_HEADS * MLA_NOPE), BF16),
        jax.ShapeDtypeStruct((MLA_HEADS, rows, MLA_ROPE), BF16),
        jax.ShapeDtypeStruct((rows, MLA_KV_RANK), act_dt),
        jax.ShapeDtypeStruct((rows, MLA_ROPE), F32),
        jax.ShapeDtypeStruct((rows, MLA_ROPE), F32),
    )
    out_specs = (
        pl.BlockSpec((tm, SSM_CH), row),
        pl.BlockSpec((tm, GQA_HEADS * GQA_HEAD_DIM), row),
        pl.BlockSpec((tm, GQA_KV_HEADS * GQA_HEAD_DIM), row),
        pl.BlockSpec((tm, GQA_KV_HEADS * GQA_HEAD_DIM), row),
        pl.BlockSpec((tm, MLA_HEADS * MLA_NOPE), row),
        pl.BlockSpec((MLA_HEADS, tm, MLA_ROPE), lambda i: (0, i, 0)),
        pl.BlockSpec((tm, MLA_KV_RANK), row),
        pl.BlockSpec((tm, MLA_ROPE), row),
        pl.BlockSpec((tm, MLA_ROPE), row),
    )
    return pl.pallas_call(
        functools.partial(_in_proj_kernel, latent),
        out_shape=out_shape,
        grid=(n_tiles,),
        in_specs=in_specs,
        out_specs=out_specs,
        compiler_params=_cparams(("parallel",)),
        name="in_proj_lat" if latent else "in_proj_ctx",
    )(x2d, mod_l, lw['norm_mix'], lw['w_in'], lw['gqa_q_norm'], lw['gqa_k_norm'], lw['mla_kv_norm'],
      lw['mla_q_nope_g'], lw['mla_q_rope_g'], lw['mla_k_rope_g'],
      tabs['cos_b'], tabs['sin_b'], tabs['cos_c'], tabs['sin_c'])


SSM_TC = 128
SSM_LANE_SPLIT = 2
SSM_UNROLL = 8


def _ssm_scan_kernel(n_chunks, u_ref, h0_ref, lam_ref, b_ref, cre_ref, cim_ref, y_ref, hT_ref,
                     utm_ref, sre_ref, sim_ref, h_ref):
    d = pl.program_id(1)
    i = pl.program_id(2)
    tc = SSM_TC

    @pl.when(i == 0)
    def _():
        h_ref[...] = h0_ref[...]

    for b in range(SUBLANES):
        for j in range(SSM_CH // LANES):
            utm_ref[j, pl.ds(b, tc, stride=SUBLANES), :] = u_ref[b, :, j * LANES:(j + 1) * LANES]

    half_n = SSM_N // 2
    half_c = SSM_CH // 2
    for k in range(2):
        r = (_dot(utm_ref[2 * k].astype(BF16), b_ref[k, :LANES, :])
             + _dot(utm_ref[2 * k + 1].astype(BF16), b_ref[k, LANES:, :]))
        sre_ref[:, k * half_n:(k + 1) * half_n] = r[:, :half_n]
        sim_ref[:, k * half_n:(k + 1) * half_n] = r[:, half_n:]

    piece = SSM_N // SSM_LANE_SPLIT
    for p in range(SSM_LANE_SPLIT):
        sl = slice(p * piece, (p + 1) * piece)
        lr = lam_ref[0, :, sl]
        li = lam_ref[1, :, sl]

        def step(s, carry):
            hr, hi = carry
            t = s + d * (tc - 1 - 2 * s)
            rowi = pl.multiple_of(t * SUBLANES, SUBLANES)
            nr = lr * hr - li * hi + sre_ref[pl.ds(rowi, SUBLANES), sl]
            ni = lr * hi + li * hr + sim_ref[pl.ds(rowi, SUBLANES), sl]
            sre_ref[pl.ds(rowi, SUBLANES), sl] = nr
            sim_ref[pl.ds(rowi, SUBLANES), sl] = ni
            return nr, ni

        hr, hi = lax.fori_loop(0, tc, step, (h_ref[0, :, sl], h_ref[1, :, sl]), unroll=SSM_UNROLL)
        h_ref[0, :, sl] = hr
        h_ref[1, :, sl] = hi

    for k in range(2):
        yk = (_dot(sre_ref[:, k * half_n:(k + 1) * half_n].astype(BF16), cre_ref[k])
              - _dot(sim_ref[:, k * half_n:(k + 1) * half_n].astype(BF16), cim_ref[k]))
        y_ref[2 * k] = yk[:, :LANES]
        y_ref[2 * k + 1] = yk[:, LANES:]

    @pl.when(i == n_chunks - 1)
    def _():
        hT_ref[...] = h_ref[...]


def _ssm_scan(u3, h0, sp):
    bsz, t, _ = u3.shape
    n_groups = bsz // SUBLANES
    n_chunks = t // SSM_TC
    tc = SSM_TC

    def chunk(d, i):
        return i + d * (n_chunks - 1 - 2 * i)

    return pl.pallas_call(
        functools.partial(_ssm_scan_kernel, n_chunks),
        out_shape=(
            jax.ShapeDtypeStruct((2, n_groups, SSM_CH // LANES, t * SUBLANES, LANES), F32),
            jax.ShapeDtypeStruct((n_groups, 2, 2, SUBLANES, SSM_N), F32),
        ),
        grid=(n_groups, 2, n_chunks),
        in_specs=[
            pl.BlockSpec((SUBLANES, tc, SSM_CH), lambda g, d, i: (g, chunk(d, i), 0)),
            pl.BlockSpec((None, None, 2, SUBLANES, SSM_N), lambda g, d, i: (g, d, 0, 0, 0)),
            pl.BlockSpec((None, 2, SUBLANES, SSM_N), lambda g, d, i: (d, 0, 0, 0)),
            pl.BlockSpec((None, 2, SSM_CH // 2, SSM_N), lambda g, d, i: (d, 0, 0, 0)),
            pl.BlockSpec((None, 2, SSM_N // 2, SSM_CH // 2), lambda g, d, i: (d, 0, 0, 0)),
            pl.BlockSpec((None, 2, SSM_N // 2, SSM_CH // 2), lambda g, d, i: (d, 0, 0, 0)),
        ],
        out_specs=(
            pl.BlockSpec((None, None, SSM_CH // LANES, tc * SUBLANES, LANES),
                         lambda g, d, i: (d, g, 0, chunk(d, i), 0)),
            pl.BlockSpec((None, None, 2, SUBLANES, SSM_N), lambda g, d, i: (g, d, 0, 0, 0)),
        ),
        scratch_shapes=[
            pltpu.VMEM((SSM_CH // LANES, tc * SUBLANES, LANES), F32),
            pltpu.VMEM((tc * SUBLANES, SSM_N), F32),
            pltpu.VMEM((tc * SUBLANES, SSM_N), F32),
            pltpu.VMEM((2, SUBLANES, SSM_N), F32),
        ],
        compiler_params=_cparams(("parallel", "parallel", "arbitrary")),
        name="ssm_scan",
    )(u3, h0, sp['lam'], sp['b'], sp['c_re'], sp['c_im'])


def _ssm_glu_kernel(u_ref, y_ref, d_ref, w_ref, o_ref, ycat_ref):
    tc = SSM_TC
    for b in range(SUBLANES):
        for j in range(SSM_CH // LANES):
            sl = slice(j * LANES, (j + 1) * LANES)
            rows_b = pl.ds(b, tc, stride=SUBLANES)
            ycat_ref[b * tc:(b + 1) * tc, sl] = (y_ref[0, j, rows_b, :] + y_ref[1, j, rows_b, :]
                                                + d_ref[:, sl] * u_ref[b, :, sl])
    zg = _dot(ycat_ref[...].astype(BF16), w_ref[...])
    out = zg[:, :SSM_CH] * jax.nn.sigmoid(zg[:, SSM_CH:])
    o_ref[...] = out.reshape(SUBLANES, tc, SSM_CH).astype(o_ref.dtype)


def _ssm_glu(u3, y, d_skip, w_glu):
    bsz, t, _ = u3.shape
    n_groups = bsz // SUBLANES
    n_chunks = t // SSM_TC
    tc = SSM_TC
    return pl.pallas_call(
        _ssm_glu_kernel,
        out_shape=jax.ShapeDtypeStruct((bsz, t, SSM_CH), BF16),
        grid=(n_groups, n_chunks),
        in_specs=[
            pl.BlockSpec((SUBLANES, tc, SSM_CH), lambda g, i: (g, i, 0)),
            pl.BlockSpec((2, None, SSM_CH // LANES, tc * SUBLANES, LANES), lambda g, i: (0, g, 0, i, 0)),
            pl.BlockSpec((1, SSM_CH), lambda g, i: (0, 0)),
            pl.BlockSpec((SSM_CH, 2 * SSM_CH), lambda g, i: (0, 0)),
        ],
        out_specs=pl.BlockSpec((SUBLANES, tc, SSM_CH), lambda g, i: (g, i, 0)),
        scratch_shapes=[pltpu.VMEM((SUBLANES * tc, SSM_CH), F32)],
        compiler_params=_cparams(("parallel", "parallel")),
        name="ssm_glu",
    )(u3, y, d_skip, w_glu)


def _softmax_pv(scores, values):
    m = functools.reduce(jnp.maximum, [jnp.max(s, axis=-1, keepdims=True) for s in scores])
    ps = [jnp.exp(s - m) for s in scores]
    l = functools.reduce(jnp.add, [jnp.sum(p, axis=-1, keepdims=True) for p in ps])
    o = functools.reduce(jnp.add, [_dot(p.astype(BF16), v) for p, v in zip(ps, values)])
    return o / l


def _gqa_kernel(latent, *refs):
    if latent:
        q_ref, k_ref, v_ref, kc_ref, vc_ref, o_ref = refs
    else:
        q_ref, k_ref, v_ref, o_ref = refs
    tq = q_ref.shape[0]
    q3 = jnp.concatenate([q_ref[:, g * LANES:(g + 1) * LANES] for g in range(GQA_GROUP)], axis=0)
    keys = [k_ref[...].astype(BF16)]
    vals = [v_ref[...].astype(BF16)]
    if latent:
        keys.append(kc_ref[...].astype(BF16))
        vals.append(vc_ref[...].astype(BF16))
    o = _softmax_pv([_dot_nt(q3, k) for k in keys], vals)
    for g in range(GQA_GROUP):
        o_ref[:, g * LANES:(g + 1) * LANES] = o[g * tq:(g + 1) * tq].astype(o_ref.dtype)


def _gqa_attn(qb, kb, vb, latent, cache_k=None, cache_v=None, layer=0):
    rows = qb.shape[0]
    seq = DEC_SEQ if latent else SEQ
    bsz = rows // seq
    tq = 256
    nq = seq // tq
    gw = GQA_GROUP * GQA_HEAD_DIM
    in_specs = [
        pl.BlockSpec((tq, gw), lambda b, h, i: (b * nq + i, h)),
        pl.BlockSpec((seq, GQA_HEAD_DIM), lambda b, h, i: (b, h)),
        pl.BlockSpec((seq, GQA_HEAD_DIM), lambda b, h, i: (b, h)),
    ]
    args = [qb, kb, vb]
    if latent:
        cspec = pl.BlockSpec((None, None, PAST_LEN, GQA_HEAD_DIM), lambda b, h, i: (b, layer, 0, h))
        in_specs += [cspec, cspec]
        args += [cache_k, cache_v]
    return pl.pallas_call(
        functools.partial(_gqa_kernel, latent),
        out_shape=jax.ShapeDtypeStruct((rows, GQA_HEADS * GQA_HEAD_DIM), BF16),
        grid=(bsz, GQA_KV_HEADS, nq),
        in_specs=in_specs,
        out_specs=pl.BlockSpec((tq, gw), lambda b, h, i: (b * nq + i, h)),
        compiler_params=_cparams(("parallel", "parallel", "parallel")),
        name="gqa_lat" if latent else "gqa_ctx",
    )(*args)


def _mla_kernel(latent, *refs):
    if latent:
        (qn_ref, qr_ref, ckv_ref, kr_ref, krg_ref, ckvc_ref, krc_ref, wuk_ref, wuv_ref, gn_ref, gr_ref,
         o_ref, kn_s, kp_s, v_s) = refs
    else:
        (qn_ref, qr_ref, ckv_ref, kr_ref, krg_ref, wuk_ref, wuv_ref, gn_ref, gr_ref,
         o_ref, kn_s, kp_s, v_s) = refs
    seq = qn_ref.shape[0]

    def expand(lo, n, ckv, kr_raw, krg):
        c = ckv.astype(BF16)
        kn = _dot(c, wuk_ref[...])
        ss = jnp.sum(kn * kn, axis=-1, keepdims=True) + jnp.sum(kr_raw * kr_raw, axis=-1, keepdims=True)
        rs = lax.rsqrt(ss * (1.0 / MLA_QK) + EPS)
        kn_s[lo:lo + n, :] = (kn * rs * gn_ref[...]).astype(BF16)
        kp_s[lo:lo + n, :] = (krg * rs).astype(BF16)
        v_s[lo:lo + n, :] = _dot(c, wuv_ref[...]).astype(BF16)

    expand(0, seq, ckv_ref[...], kr_ref[...], krg_ref[...])
    if latent:
        krc = krc_ref[...]
        expand(seq, PAST_LEN, ckvc_ref[...], krc, krc * gr_ref[...])

    tq = 256
    for qi in range(seq // tq):
        rs_ = slice(qi * tq, (qi + 1) * tq)
        s = _dot_nt(qn_ref[rs_, :], kn_s[...]) + _dot_nt(qr_ref[rs_, :], kp_s[...])
        o_ref[rs_, :] = _softmax_pv([s], [v_s[...]]).astype(o_ref.dtype)


def _mla_attn(qcn, qcr, ckv, kr, krg, lw, latent, cache_ckv=None, cache_kr=None, layer=0):
    rows = qcn.shape[0]
    seq = DEC_SEQ if latent else SEQ
    bsz = rows // seq
    t_all = seq + (PAST_LEN if latent else 0)
    in_specs = [
        pl.BlockSpec((seq, MLA_NOPE), lambda b, h: (b, h)),
        pl.BlockSpec((None, seq, MLA_ROPE), lambda b, h: (h, b, 0)),
        pl.BlockSpec((seq, MLA_KV_RANK), lambda b, h: (b, 0)),
        pl.BlockSpec((seq, MLA_ROPE), lambda b, h: (b, 0)),
        pl.BlockSpec((seq, MLA_ROPE), lambda b, h: (b, 0)),
    ]
    args = [qcn, qcr, ckv, kr, krg]
    if latent:
        in_specs += [
            pl.BlockSpec((None, None, PAST_LEN, MLA_KV_RANK), lambda b, h: (b, layer, 0, 0)),
            pl.BlockSpec((None, None, PAST_LEN, MLA_ROPE), lambda b, h: (b, layer, 0, 0)),
        ]
        args += [cache_ckv, cache_kr]
    in_specs += [
        pl.BlockSpec((MLA_KV_RANK, MLA_NOPE), lambda b, h: (0, h)),
        pl.BlockSpec((MLA_KV_RANK, MLA_V), lambda b, h: (0, h)),
        pl.BlockSpec((1, MLA_NOPE), lambda b, h: (0, 0)),
        pl.BlockSpec((1, MLA_ROPE), lambda b, h: (0, 0)),
    ]
    args += [lw['mla_w_uk'], lw['mla_w_uv'], lw['mla_k_nope_g'], lw['mla_k_rope_g64']]
    return pl.pallas_call(
        functools.partial(_mla_kernel, latent),
        out_shape=jax.ShapeDtypeStruct((rows, MLA_HEADS * MLA_V), BF16),
        grid=(bsz, MLA_HEADS),
        in_specs=in_specs,
        out_specs=pl.BlockSpec((seq, MLA_V), lambda b, h: (b, h)),
        scratch_shapes=[
            pltpu.VMEM((t_all, MLA_NOPE), BF16),
            pltpu.VMEM((t_all, MLA_ROPE), BF16),
            pltpu.VMEM((t_all, MLA_V), BF16),
        ],
        compiler_params=_cparams(("parallel", "arbitrary")),
        name="mla_lat" if latent else "mla_ctx",
    )(*args)


def _out_proj_kernel(ya_ref, yb_ref, yc_ref, x_ref, mod_ref, w_ref, g_ref, x1_ref, h2_ref):
    wa = SSM_CH
    wb = wa + GQA_HEADS * GQA_HEAD_DIM
    o = (_dot(ya_ref[...], w_ref[0:wa, :]) + _dot(yb_ref[...], w_ref[wa:wb, :])
         + _dot(yc_ref[...], w_ref[wb:, :]))
    x1 = x_ref[...] + mod_ref[2:3, :] * o
    x1_ref[...] = x1
    ms = jnp.mean(x1 * x1, axis=-1, keepdims=True)
    y = x1 * lax.rsqrt(ms + EPS) * g_ref[...]
    h2_ref[...] = (y * (1.0 + mod_ref[4:5, :]) + mod_ref[3:4, :]).astype(h2_ref.dtype)


def _mod_map(latent, tm):
    tiles_per_seq = DEC_SEQ // tm
    if latent:
        return lambda i, *_: (1 + i // tiles_per_seq, 0, 0)
    return lambda i, *_: (0, 0, 0)


def _out_proj(ya, yb, yc, x2d, mod_l, lw, latent, tm):
    rows = x2d.shape[0]
    row = lambda i: (i, 0)
    full = lambda i: (0, 0)
    return pl.pallas_call(
        _out_proj_kernel,
        out_shape=(jax.ShapeDtypeStruct((rows, D_MODEL), F32), jax.ShapeDtypeStruct((rows, D_MODEL), BF16)),
        grid=(rows // tm,),
        in_specs=[
            pl.BlockSpec((tm, SSM_CH), row),
            pl.BlockSpec((tm, GQA_HEADS * GQA_HEAD_DIM), row),
            pl.BlockSpec((tm, MLA_HEADS * MLA_V), row),
            pl.BlockSpec((tm, D_MODEL), row),
            pl.BlockSpec((None, 6, D_MODEL), _mod_map(latent, tm)),
            pl.BlockSpec((D_MODEL, D_MODEL), full),
            pl.BlockSpec((1, D_MODEL), full),
        ],
        out_specs=(pl.BlockSpec((tm, D_MODEL), row), pl.BlockSpec((tm, D_MODEL), row)),
        compiler_params=_cparams(("parallel",)),
        name="out_proj",
    )(ya, yb, yc, x2d, mod_l, lw['w_out'], lw['norm_mlp'])


def _mlp_kernel(h2_ref, x1_ref, mod_ref, w1_ref, w2_ref, o_ref, acc_ref):
    j = pl.program_id(1)

    @pl.when(j == 0)
    def _():
        acc_ref[...] = jnp.zeros_like(acc_ref)

    f = jnp.maximum(_dot(h2_ref[...], w1_ref[...]), 0.0)
    acc_ref[...] += _dot((f * f).astype(BF16), w2_ref[...])

    @pl.when(j == pl.num_programs(1) - 1)
    def _():
        o_ref[...] = x1_ref[...] + mod_ref[5:6, :] * acc_ref[...]


def _mlp(h2, x1, mod_l, lw, latent, tm, tf):
    rows = h2.shape[0]
    return pl.pallas_call(
        _mlp_kernel,
        out_shape=jax.ShapeDtypeStruct((rows, D_MODEL), F32),
        grid=(rows // tm, D_FF // tf),
        in_specs=[
            pl.BlockSpec((tm, D_MODEL), lambda i, j: (i, 0)),
            pl.BlockSpec((tm, D_MODEL), lambda i, j: (i, 0)),
            pl.BlockSpec((None, 6, D_MODEL), _mod_map(latent, tm)),
            pl.BlockSpec((D_MODEL, tf), lambda i, j: (0, j)),
            pl.BlockSpec((tf, D_MODEL), lambda i, j: (j, 0)),
        ],
        out_specs=pl.BlockSpec((tm, D_MODEL), lambda i, j: (i, 0)),
        scratch_shapes=[pltpu.VMEM((tm, D_MODEL), F32)],
        compiler_params=_cparams(("parallel", "arbitrary")),
        name="mlp",
    )(h2, x1, mod_l, lw['w_ff1'], lw['w_ff2'])


def _in_proj_perm():
    base = OFF_VB + GQA_KV_HEADS * GQA_HEAD_DIM
    nope = [base + h * MLA_QK + i for h in range(MLA_HEADS) for i in range(MLA_NOPE)]
    rope = [base + h * MLA_QK + MLA_NOPE + i for h in range(MLA_HEADS) for i in range(MLA_ROPE)]
    ckv0 = base + MLA_HEADS * MLA_QK
    ckv = list(range(ckv0, ckv0 + MLA_KV_RANK))
    kr = list(range(ckv0 + MLA_KV_RANK, ckv0 + MLA_KV_RANK + MLA_ROPE))
    return np.array(list(range(base)) + nope + rope + ckv + kr + kr, dtype=np.int32)


def _rope_tables(seq):
    t = jnp.arange(seq)
    row = (t // GRID_W).astype(F32)
    col = (t % GRID_W).astype(F32)

    def table(d):
        quarter = d // 4
        inv = ROPE_BASE ** (-(jnp.arange(quarter, dtype=F32) / quarter))
        ar = row[:, None] * inv[None, :]
        ac = col[:, None] * inv[None, :]
        cos = jnp.concatenate([jnp.cos(ar), jnp.cos(ar), jnp.cos(ac), jnp.cos(ac)], axis=-1)
        sin = jnp.concatenate([-jnp.sin(ar), jnp.sin(ar), -jnp.sin(ac), jnp.sin(ac)], axis=-1)
        reps = LANES // d
        return jnp.tile(cos, (1, reps)), jnp.tile(sin, (1, reps))

    cos_b, sin_b = table(GQA_HEAD_DIM)
    cos_c, sin_c = table(MLA_ROPE)
    return {'cos_b': cos_b, 'sin_b': sin_b, 'cos_c': cos_c, 'sin_c': sin_c}


def _ssm_params(lam_re, lam_im, log_dt, b_re, b_im, c_re, c_im):
    lam = lax.complex(lam_re.astype(F32), lam_im.astype(F32))
    dt = jnp.exp(log_dt.astype(F32))[..., None]
    lam_bar = jnp.exp(lam * dt)
    b_bar = ((lam_bar - 1.0) / lam)[..., None] * lax.complex(b_re.astype(F32), b_im.astype(F32))
    c_mat = lax.complex(c_re.astype(F32), c_im.astype(F32))
    lam_flat = lam_bar.reshape(2, SSM_N)
    lam_ri = jnp.stack([jnp.real(lam_flat), jnp.imag(lam_flat)], axis=1)
    lam_b = jnp.broadcast_to(lam_ri[:, :, None, :], (2, 2, SUBLANES, SSM_N))
    gh = SSM_GROUPS // 2
    eye = jnp.eye(gh, dtype=F32)

    def bmat(x):
        x = x.reshape(2, 2, gh, SSM_STATE, SSM_GROUP)
        return jnp.einsum('dkgpc,gh->dkgchp', x, eye).reshape(2, 2, gh * SSM_GROUP, gh * SSM_STATE)

    def cmat(x):
        x = x.reshape(2, 2, gh, SSM_GROUP, SSM_STATE)
        return jnp.einsum('dkgcp,gh->dkgphc', x, eye).reshape(2, 2, gh * SSM_STATE, gh * SSM_GROUP)

    b_cat = jnp.concatenate([bmat(jnp.real(b_bar)), bmat(jnp.imag(b_bar))], axis=-1).astype(BF16)
    return {'lam': lam_b, 'b': b_cat, 'c_re': cmat(jnp.real(c_mat)).astype(BF16),
            'c_im': cmat(jnp.imag(c_mat)).astype(BF16)}


def _trunk_layer(x2d, mod_l, lw, sp, tabs, latent, ctx, layer):
    seq = DEC_SEQ if latent else SEQ
    bsz = x2d.shape[0] // seq
    tm = 512
    u, qb, kb, vb, qcn, qcr, ckv, kr, krg = _in_proj(x2d, mod_l, latent, lw, tabs, tm)

    u3 = u.reshape(bsz, seq, SSM_CH)
    y_tm, h_t = _ssm_scan(u3, ctx['h0'], sp)
    ya = _ssm_glu(u3, y_tm, lw['ssm_d'], lw['ssm_w_glu']).reshape(bsz * seq, SSM_CH)

    if latent:
        yb = _gqa_attn(qb, kb, vb, True, ctx['k'], ctx['v'], layer)
        yc = _mla_attn(qcn, qcr, ckv, kr, krg, lw, True, ctx['ckv'], ctx['kr'], layer)
    else:
        yb = _gqa_attn(qb, kb, vb, False)
        yc = _mla_attn(qcn, qcr, ckv, kr, krg, lw, False)

    x1, h2 = _out_proj(ya, yb, yc, x2d, mod_l, lw, latent, tm)
    x2 = _mlp(h2, x1, mod_l, lw, latent, tm, 512)
    return x2, (kb, vb, ckv, kr, h_t)


def kernel(x_prompt, x_sample, cache_attn_k, cache_attn_v, cache_mla_ckv, cache_mla_krope, state_ssm, c, c_ctx, w_mod, b_mod, norm_mix, norm_mlp, w_in, gqa_q_norm, gqa_k_norm, mla_kv_norm, mla_q_norm, mla_k_norm, mla_w_uk, mla_w_uv, ssm_lam_re, ssm_lam_im, ssm_log_dt, ssm_b_re, ssm_b_im, ssm_c_re, ssm_c_im, ssm_d, ssm_w_glu, w_out, w_ff1, w_ff2):
    cvec = jnp.zeros((N_MOD, D_MODEL), F32).at[0].set(c_ctx).at[1:1 + DEC_BATCH].set(c)
    mod = _adaln(cvec, w_mod, b_mod.reshape(DEPTH, 1, 6 * D_MODEL)).reshape(DEPTH, N_MOD, 6, D_MODEL)

    perm = _in_proj_perm()
    tabs = _rope_tables(DEC_SEQ)
    cache_k = cache_attn_k.reshape(DEC_BATCH, DEPTH, PAST_LEN, GQA_KV_HEADS * GQA_HEAD_DIM)
    cache_v = cache_attn_v.reshape(DEC_BATCH, DEPTH, PAST_LEN, GQA_KV_HEADS * GQA_HEAD_DIM)

    xp = x_prompt.reshape(BATCH * SEQ, D_MODEL)
    xs = x_sample.reshape(DEC_BATCH * DEC_SEQ, D_MODEL)
    new_k, new_v, new_ckv, new_kr, new_ssm = [], [], [], [], []
    h0_zero = jnp.zeros((BATCH // SUBLANES, 2, 2, SUBLANES, SSM_N), F32)
    for l in range(DEPTH):
        row2 = lambda a: a[l].reshape(1, -1)
        gq = mla_q_norm[l]
        gk = mla_k_norm[l]
        lw = {
            'norm_mix': row2(norm_mix), 'norm_mlp': row2(norm_mlp),
            'w_in': w_in[l][:, perm].astype(BF16),
            'w_out': w_out[l].astype(BF16), 'w_ff1': w_ff1[l].astype(BF16), 'w_ff2': w_ff2[l].astype(BF16),
            'gqa_q_norm': row2(gqa_q_norm), 'gqa_k_norm': row2(gqa_k_norm), 'mla_kv_norm': row2(mla_kv_norm),
            'mla_q_nope_g': gq[:MLA_NOPE].reshape(1, -1),
            'mla_q_rope_g': jnp.tile(gq[MLA_NOPE:], 2).reshape(1, -1),
            'mla_k_nope_g': gk[:MLA_NOPE].reshape(1, -1),
            'mla_k_rope_g': jnp.tile(gk[MLA_NOPE:], 2).reshape(1, -1),
            'mla_k_rope_g64': gk[MLA_NOPE:].reshape(1, -1),
            'mla_w_uk': mla_w_uk[l].reshape(MLA_KV_RANK, MLA_HEADS * MLA_NOPE).astype(BF16),
            'mla_w_uv': mla_w_uv[l].reshape(MLA_KV_RANK, MLA_HEADS * MLA_V).astype(BF16),
            'ssm_d': row2(ssm_d), 'ssm_w_glu': ssm_w_glu[l].astype(BF16),
        }
        sp = _ssm_params(ssm_lam_re[l], ssm_lam_im[l], ssm_log_dt[l], ssm_b_re[l], ssm_b_im[l],
                         ssm_c_re[l], ssm_c_im[l])
        mod_l = mod[l]

        xp, (k, v, ckv_n, kr, h_t) = _trunk_layer(xp, mod_l, lw, sp, tabs, False, {'h0': h0_zero}, l)
        new_k.append(k.reshape(BATCH, SEQ, GQA_KV_HEADS, GQA_HEAD_DIM))
        new_v.append(v.reshape(BATCH, SEQ, GQA_KV_HEADS, GQA_HEAD_DIM))
        new_ckv.append(ckv_n.reshape(BATCH, SEQ, MLA_KV_RANK))
        new_kr.append(kr.reshape(BATCH, SEQ, MLA_ROPE))
        hs = h_t.transpose(0, 3, 1, 4, 2).reshape(BATCH, 2, SSM_GROUPS, SSM_STATE, 2)
        new_ssm.append(hs)

        s0 = state_ssm[:, l].reshape(DEC_BATCH, 2, SSM_N, 2)
        h0 = s0.transpose(1, 3, 0, 2)[None]
        ctx = {'k': cache_k, 'v': cache_v, 'ckv': cache_mla_ckv, 'kr': cache_mla_krope, 'h0': h0}
        xs, _ = _trunk_layer(xs, mod_l, lw, sp, tabs, True, ctx, l)

    return (xp.reshape(BATCH, SEQ, D_MODEL), xs.reshape(DEC_BATCH, DEC_SEQ, D_MODEL),
            jnp.stack(new_k, axis=1), jnp.stack(new_v, axis=1), jnp.stack(new_ckv, axis=1),
            jnp.stack(new_kr, axis=1), jnp.stack(new_ssm, axis=1))
```

```python
import functools
import math

import numpy as np
import jax
import jax.numpy as jnp
from jax import lax
from jax.experimental import pallas as pl
from jax.experimental.pallas import tpu as pltpu

D_MODEL = 2048
BATCH = 32
SEQ = 256
DEPTH = 2
DEC_BATCH = 8
DEC_SEQ = 1024
PAST_LEN = 512
GRID_W = 64
ROPE_BASE = 10000.0
EPS = 1e-6
SSM_CH = 512
SSM_GROUP = 16
SSM_GROUPS = SSM_CH // SSM_GROUP
SSM_STATE = 64
SSM_N = SSM_GROUPS * SSM_STATE
GQA_HEADS = 6
GQA_KV_HEADS = 2
GQA_GROUP = GQA_HEADS // GQA_KV_HEADS
GQA_HEAD_DIM = 128
MLA_HEADS = 6
MLA_NOPE = 128
MLA_ROPE = 64
MLA_QK = MLA_NOPE + MLA_ROPE
MLA_QK_PAD = 256
MLA_V = 128
MLA_KV_RANK = 512
D_FF = 4 * D_MODEL
N_MOD = 16

OFF_U = 0
OFF_QB = OFF_U + SSM_CH
OFF_KB = OFF_QB + GQA_HEADS * GQA_HEAD_DIM
OFF_VB = OFF_KB + GQA_KV_HEADS * GQA_HEAD_DIM
OFF_QCN = OFF_VB + GQA_KV_HEADS * GQA_HEAD_DIM
OFF_QCR = OFF_QCN + MLA_HEADS * MLA_NOPE
OFF_CKV = OFF_QCR + MLA_HEADS * MLA_ROPE
OFF_KR = OFF_CKV + MLA_KV_RANK
IN_WIDTH_P = OFF_KR + 2 * MLA_ROPE

LANES = 128
SUBLANES = 8
VMEM_LIMIT = 56 * 1024 * 1024

TM_PROJ = 512
TM_MLP = 1024
TF_MLP = 512

BF16 = jnp.bfloat16
F32 = jnp.float32


def _cparams(sem):
    return pltpu.CompilerParams(dimension_semantics=sem, vmem_limit_bytes=VMEM_LIMIT)


def _dot(a, b):
    return jnp.dot(a, b, preferred_element_type=F32)


def _dot_nt(a, b):
    return lax.dot_general(a, b, (((1,), (1,)), ((), ())), preferred_element_type=F32)


def _adaln_kernel(c_ref, w_ref, b_ref, o_ref):
    c = c_ref[...]
    s = (c * jax.nn.sigmoid(c)).astype(BF16)
    o_ref[...] = _dot(s, w_ref[...].astype(BF16)) + b_ref[...]


def _adaln(cvec, w_mod, b_mod):
    tn = 1024
    return pl.pallas_call(
        _adaln_kernel,
        out_shape=jax.ShapeDtypeStruct((DEPTH, N_MOD, 6 * D_MODEL), F32),
        grid=(DEPTH, 6 * D_MODEL // tn),
        in_specs=[
            pl.BlockSpec((N_MOD, D_MODEL), lambda l, j: (0, 0)),
            pl.BlockSpec((None, D_MODEL, tn), lambda l, j: (l, 0, j)),
            pl.BlockSpec((None, 1, tn), lambda l, j: (l, 0, j)),
        ],
        out_specs=pl.BlockSpec((None, N_MOD, tn), lambda l, j: (l, 0, j)),
        compiler_params=_cparams(("parallel", "parallel")),
        name="adaln",
    )(cvec, w_mod, b_mod)


def _swap_halves(x, block):
    lane = lax.broadcasted_iota(jnp.int32, x.shape, 1)
    first = (lane % (2 * block)) < block
    return jnp.where(first, pltpu.roll(x, LANES - block, 1), pltpu.roll(x, block, 1))


def _rope(x, cos, sin_signed, block):
    return x * cos + _swap_halves(x, block) * sin_signed


def _in_proj_kernel(latent, x_ref, mod_ref, gmix_ref, w_ref, gq_ref, gk_ref, gkv_ref, gqn_ref, gqr_ref, gkr_ref,
                    cosb_ref, sinb_ref, cosc_ref, sinc_ref,
                    u_ref, qb_ref, kb_ref, vb_ref, qc_ref, ckv_ref, kr_ref, krg_ref):
    x = x_ref[...]
    ms = jnp.mean(x * x, axis=-1, keepdims=True)
    y = x * lax.rsqrt(ms + EPS) * gmix_ref[...]
    h = (y * (1.0 + mod_ref[1:2, :]) + mod_ref[0:1, :]).astype(BF16)

    def proj(off, width):
        return _dot(h, w_ref[:, off:off + width])

    u_ref[...] = proj(OFF_U, SSM_CH)

    scale_b = 1.0 / math.sqrt(GQA_HEAD_DIM)
    zq = proj(OFF_QB, GQA_HEADS * GQA_HEAD_DIM)
    for hd in range(GQA_HEADS):
        col = zq[:, hd * LANES:(hd + 1) * LANES]
        q = col * lax.rsqrt(jnp.mean(col * col, axis=-1, keepdims=True) + EPS) * gq_ref[...]
        if latent:
            q = _rope(q, cosb_ref[...], sinb_ref[...], GQA_HEAD_DIM // 4)
        qb_ref[:, hd * LANES:(hd + 1) * LANES] = (q * scale_b).astype(qb_ref.dtype)
    zk = proj(OFF_KB, GQA_KV_HEADS * GQA_HEAD_DIM)
    for hd in range(GQA_KV_HEADS):
        col = zk[:, hd * LANES:(hd + 1) * LANES]
        k = col * lax.rsqrt(jnp.mean(col * col, axis=-1, keepdims=True) + EPS) * gk_ref[...]
        if latent:
            k = _rope(k, cosb_ref[...], sinb_ref[...], GQA_HEAD_DIM // 4)
        kb_ref[:, hd * LANES:(hd + 1) * LANES] = k.astype(kb_ref.dtype)
    vb_ref[...] = proj(OFF_VB, GQA_KV_HEADS * GQA_HEAD_DIM).astype(vb_ref.dtype)

    scale_c = 1.0 / math.sqrt(MLA_QK)
    zn = proj(OFF_QCN, MLA_HEADS * MLA_NOPE)
    zr = proj(OFF_QCR, MLA_HEADS * MLA_ROPE)
    lane = lax.broadcasted_iota(jnp.int32, (x.shape[0], LANES), 1)
    low = lane < MLA_ROPE
    for pair in range(MLA_HEADS // 2):
        colr = zr[:, pair * LANES:(pair + 1) * LANES]
        sq = colr * colr
        ss_lo = jnp.sum(jnp.where(low, sq, 0.0), axis=-1, keepdims=True)
        ss_hi = jnp.sum(jnp.where(low, 0.0, sq), axis=-1, keepdims=True)
        rs = []
        for half, ss_r in ((0, ss_lo), (1, ss_hi)):
            hd = 2 * pair + half
            coln = zn[:, hd * LANES:(hd + 1) * LANES]
            ss = jnp.sum(coln * coln, axis=-1, keepdims=True) + ss_r
            r = lax.rsqrt(ss * (1.0 / MLA_QK) + EPS)
            rs.append(r)
            qc_ref[hd, :, :MLA_NOPE] = (coln * r * gqn_ref[...] * scale_c).astype(qc_ref.dtype)
        qr = colr * jnp.where(low, rs[0], rs[1]) * gqr_ref[...]
        if latent:
            qr = _rope(qr, cosc_ref[...], sinc_ref[...], MLA_ROPE // 4)
        qr = qr * scale_c
        zeros = jnp.zeros((x.shape[0], MLA_QK_PAD - MLA_QK), qc_ref.dtype)
        for half in range(2):
            hd = 2 * pair + half
            qc_ref[hd, :, MLA_NOPE:MLA_QK] = qr[:, half * MLA_ROPE:(half + 1) * MLA_ROPE].astype(qc_ref.dtype)
            qc_ref[hd, :, MLA_QK:] = zeros

    zc = proj(OFF_CKV, MLA_KV_RANK)
    ckv_ref[...] = (zc * lax.rsqrt(jnp.mean(zc * zc, axis=-1, keepdims=True) + EPS) * gkv_ref[...]).astype(ckv_ref.dtype)
    zkr = proj(OFF_KR, 2 * MLA_ROPE)
    kr_ref[...] = zkr[:, :MLA_ROPE]
    krg = zkr * gkr_ref[...]
    if latent:
        krg = _rope(krg, cosc_ref[...], sinc_ref[...], MLA_ROPE // 4)
    krg_ref[...] = krg[:, :MLA_ROPE]


def _in_proj(x2d, mod_l, latent, lw, tabs, tm):
    rows = x2d.shape[0]
    n_tiles = rows // tm
    tiles_per_seq = DEC_SEQ // tm
    if latent:
        mod_map = lambda i: (1 + i // tiles_per_seq, 0, 0)
        tab_map = lambda i: (i % tiles_per_seq, 0)
    else:
        mod_map = lambda i: (0, 0, 0)
        tab_map = lambda i: (0, 0)
    full = lambda i: (0, 0)
    row = lambda i: (i, 0)
    act_dt = BF16 if latent else F32
    in_specs = [
        pl.BlockSpec((tm, D_MODEL), row),
        pl.BlockSpec((None, 6, D_MODEL), mod_map),
        pl.BlockSpec((1, D_MODEL), full),
        pl.BlockSpec((D_MODEL, IN_WIDTH_P), full),
        pl.BlockSpec((1, LANES), full),
        pl.BlockSpec((1, LANES), full),
        pl.BlockSpec((1, MLA_KV_RANK), full),
        pl.BlockSpec((1, LANES), full),
        pl.BlockSpec((1, LANES), full),
        pl.BlockSpec((1, LANES), full),
        pl.BlockSpec((tm, LANES), tab_map),
        pl.BlockSpec((tm, LANES), tab_map),
        pl.BlockSpec((tm, LANES), tab_map),
        pl.BlockSpec((tm, LANES), tab_map),
    ]
    out_shape = (
        jax.ShapeDtypeStruct((rows, SSM_CH), F32),
        jax.ShapeDtypeStruct((rows, GQA_HEADS * GQA_HEAD_DIM), BF16),
        jax.ShapeDtypeStruct((rows, GQA_KV_HEADS * GQA_HEAD_DIM), act_dt),
        jax.ShapeDtypeStruct((rows, GQA_KV_HEADS * GQA_HEAD_DIM), act_dt),
        jax.ShapeDtypeStruct((MLA_HEADS, rows, MLA_QK_PAD), BF16),
        jax.ShapeDtypeStruct((rows, MLA_KV_RANK), act_dt),
        jax.ShapeDtypeStruct((rows, MLA_ROPE), F32),
        jax.ShapeDtypeStruct((rows, MLA_ROPE), F32),
    )
    out_specs = (
        pl.BlockSpec((tm, SSM_CH), row),
        pl.BlockSpec((tm, GQA_HEADS * GQA_HEAD_DIM), row),
        pl.BlockSpec((tm, GQA_KV_HEADS * GQA_HEAD_DIM), row),
        pl.BlockSpec((tm, GQA_KV_HEADS * GQA_HEAD_DIM), row),
        pl.BlockSpec((MLA_HEADS, tm, MLA_QK_PAD), lambda i: (0, i, 0)),
        pl.BlockSpec((tm, MLA_KV_RANK), row),
        pl.BlockSpec((tm, MLA_ROPE), row),
        pl.BlockSpec((tm, MLA_ROPE), row),
    )
    return pl.pallas_call(
        functools.partial(_in_proj_kernel, latent),
        out_shape=out_shape,
        grid=(n_tiles,),
        in_specs=in_specs,
        out_specs=out_specs,
        compiler_params=_cparams(("parallel",)),
        name="in_proj_lat" if latent else "in_proj_ctx",
    )(x2d, mod_l, lw['norm_mix'], lw['w_in'], lw['gqa_q_norm'], lw['gqa_k_norm'], lw['mla_kv_norm'],
      lw['mla_q_nope_g'], lw['mla_q_rope_g'], lw['mla_k_rope_g'],
      tabs['cos_b'], tabs['sin_b'], tabs['cos_c'], tabs['sin_c'])


SSM_TC = 128
SSM_LANE_SPLIT = 2
SSM_UNROLL = 8


def _ssm_scan_kernel(n_chunks, u_ref, h0_ref, lam_ref, b_ref, cre_ref, cim_ref, y_ref, hT_ref,
                     utm_ref, sre_ref, sim_ref, h_ref):
    d = pl.program_id(1)
    i = pl.program_id(2)
    tc = SSM_TC

    @pl.when(i == 0)
    def _():
        h_ref[...] = h0_ref[...]

    for b in range(SUBLANES):
        for j in range(SSM_CH // LANES):
            utm_ref[j, pl.ds(b, tc, stride=SUBLANES), :] = u_ref[b, :, j * LANES:(j + 1) * LANES]

    half_n = SSM_N // 2
    half_c = SSM_CH // 2
    for k in range(2):
        uk = jnp.concatenate([utm_ref[2 * k], utm_ref[2 * k + 1]], axis=-1).astype(BF16)
        r = _dot(uk, b_ref[k])
        sre_ref[:, k * half_n:(k + 1) * half_n] = r[:, :half_n]
        sim_ref[:, k * half_n:(k + 1) * half_n] = r[:, half_n:]

    piece = SSM_N // SSM_LANE_SPLIT
    for p in range(SSM_LANE_SPLIT):
        sl = slice(p * piece, (p + 1) * piece)
        lr = lam_ref[0, :, sl]
        li = lam_ref[1, :, sl]

        def step(s, carry):
            hr, hi = carry
            t = s + d * (tc - 1 - 2 * s)
            rowi = pl.multiple_of(t * SUBLANES, SUBLANES)
            nr = lr * hr - li * hi + sre_ref[pl.ds(rowi, SUBLANES), sl]
            ni = lr * hi + li * hr + sim_ref[pl.ds(rowi, SUBLANES), sl]
            sre_ref[pl.ds(rowi, SUBLANES), sl] = nr
            sim_ref[pl.ds(rowi, SUBLANES), sl] = ni
            return nr, ni

        hr, hi = lax.fori_loop(0, tc, step, (h_ref[0, :, sl], h_ref[1, :, sl]), unroll=SSM_UNROLL)
        h_ref[0, :, sl] = hr
        h_ref[1, :, sl] = hi

    for k in range(2):
        yk = (_dot(sre_ref[:, k * half_n:(k + 1) * half_n].astype(BF16), cre_ref[k])
              - _dot(sim_ref[:, k * half_n:(k + 1) * half_n].astype(BF16), cim_ref[k]))
        y_ref[2 * k] = yk[:, :LANES]
        y_ref[2 * k + 1] = yk[:, LANES:]

    @pl.when(i == n_chunks - 1)
    def _():
        hT_ref[...] = h_ref[...]


def _ssm_scan(u3, h0, sp):
    bsz, t, _ = u3.shape
    n_groups = bsz // SUBLANES
    n_chunks = t // SSM_TC
    tc = SSM_TC

    def chunk(d, i):
        return i + d * (n_chunks - 1 - 2 * i)

    return pl.pallas_call(
        functools.partial(_ssm_scan_kernel, n_chunks),
        out_shape=(
            jax.ShapeDtypeStruct((2, n_groups, SSM_CH // LANES, t * SUBLANES, LANES), F32),
            jax.ShapeDtypeStruct((n_groups, 2, 2, SUBLANES, SSM_N), F32),
        ),
        grid=(n_groups, 2, n_chunks),
        in_specs=[
            pl.BlockSpec((SUBLANES, tc, SSM_CH), lambda g, d, i: (g, chunk(d, i), 0)),
            pl.BlockSpec((None, None, 2, SUBLANES, SSM_N), lambda g, d, i: (g, d, 0, 0, 0)),
            pl.BlockSpec((None, 2, SUBLANES, SSM_N), lambda g, d, i: (d, 0, 0, 0)),
            pl.BlockSpec((None, 2, SSM_CH // 2, SSM_N), lambda g, d, i: (d, 0, 0, 0)),
            pl.BlockSpec((None, 2, SSM_N // 2, SSM_CH // 2), lambda g, d, i: (d, 0, 0, 0)),
            pl.BlockSpec((None, 2, SSM_N // 2, SSM_CH // 2), lambda g, d, i: (d, 0, 0, 0)),
        ],
        out_specs=(
            pl.BlockSpec((None, None, SSM_CH // LANES, tc * SUBLANES, LANES),
                         lambda g, d, i: (d, g, 0, chunk(d, i), 0)),
            pl.BlockSpec((None, None, 2, SUBLANES, SSM_N), lambda g, d, i: (g, d, 0, 0, 0)),
        ),
        scratch_shapes=[
            pltpu.VMEM((SSM_CH // LANES, tc * SUBLANES, LANES), F32),
            pltpu.VMEM((tc * SUBLANES, SSM_N), F32),
            pltpu.VMEM((tc * SUBLANES, SSM_N), F32),
            pltpu.VMEM((2, SUBLANES, SSM_N), F32),
        ],
        compiler_params=_cparams(("parallel", "parallel", "arbitrary")),
        name="ssm_scan",
    )(u3, h0, sp['lam'], sp['b'], sp['c_re'], sp['c_im'])


def _ssm_glu_kernel(u_ref, y_ref, d_ref, w_ref, o_ref, ycat_ref):
    tc = SSM_TC
    for b in range(SUBLANES):
        for j in range(SSM_CH // LANES):
            sl = slice(j * LANES, (j + 1) * LANES)
            rows_b = pl.ds(b, tc, stride=SUBLANES)
            ycat_ref[b * tc:(b + 1) * tc, sl] = (y_ref[0, j, rows_b, :] + y_ref[1, j, rows_b, :]
                                                + d_ref[:, sl] * u_ref[b, :, sl])
    zg = _dot(ycat_ref[...].astype(BF16), w_ref[...])
    out = zg[:, :SSM_CH] * jax.nn.sigmoid(zg[:, SSM_CH:])
    o_ref[...] = out.reshape(SUBLANES, tc, SSM_CH).astype(o_ref.dtype)


def _ssm_glu(u3, y, d_skip, w_glu):
    bsz, t, _ = u3.shape
    n_groups = bsz // SUBLANES
    n_chunks = t // SSM_TC
    tc = SSM_TC
    return pl.pallas_call(
        _ssm_glu_kernel,
        out_shape=jax.ShapeDtypeStruct((bsz, t, SSM_CH), BF16),
        grid=(n_groups, n_chunks),
        in_specs=[
            pl.BlockSpec((SUBLANES, tc, SSM_CH), lambda g, i: (g, i, 0)),
            pl.BlockSpec((2, None, SSM_CH // LANES, tc * SUBLANES, LANES), lambda g, i: (0, g, 0, i, 0)),
            pl.BlockSpec((1, SSM_CH), lambda g, i: (0, 0)),
            pl.BlockSpec((SSM_CH, 2 * SSM_CH), lambda g, i: (0, 0)),
        ],
        out_specs=pl.BlockSpec((SUBLANES, tc, SSM_CH), lambda g, i: (g, i, 0)),
        scratch_shapes=[pltpu.VMEM((SUBLANES * tc, SSM_CH), F32)],
        compiler_params=_cparams(("parallel", "parallel")),
        name="ssm_glu",
    )(u3, y, d_skip, w_glu)


def _softmax_pv(scores, values):
    m = functools.reduce(jnp.maximum, [jnp.max(s, axis=-1, keepdims=True) for s in scores])
    ps = [jnp.exp(s - m) for s in scores]
    l = functools.reduce(jnp.add, [jnp.sum(p, axis=-1, keepdims=True) for p in ps])
    o = functools.reduce(jnp.add, [_dot(p.astype(BF16), v) for p, v in zip(ps, values)])
    return o / l


ATTN_TQ = 256


def _for_each_q_block(nb, seq, body):
    nq = seq // ATTN_TQ
    for s in range(nb):
        if nq == 1:
            body(s, s * seq)
        else:
            def it(qi, carry, s=s):
                body(s, pl.multiple_of(s * seq + qi * ATTN_TQ, ATTN_TQ))
                return carry
            lax.fori_loop(0, nq, it, 0)


def _gqa_kernel(latent, nb, seq, *refs):
    if latent:
        q_ref, k_ref, v_ref, kc_ref, vc_ref, o_ref, k_s, v_s = refs
    else:
        q_ref, k_ref, v_ref, o_ref, k_s, v_s = refs
    tq = ATTN_TQ
    for s in range(nb):
        for h in range(GQA_KV_HEADS):
            hl = slice(h * LANES, (h + 1) * LANES)
            k_s[s, h, 0:seq, :] = k_ref[s * seq:(s + 1) * seq, hl].astype(BF16)
            v_s[s, h, 0:seq, :] = v_ref[s * seq:(s + 1) * seq, hl].astype(BF16)
            if latent:
                k_s[s, h, seq:, :] = kc_ref[:, hl].astype(BF16)
                v_s[s, h, seq:, :] = vc_ref[:, hl].astype(BF16)

    def body(s, r0):
        for h in range(GQA_KV_HEADS):
            heads = [h * GQA_GROUP + g for g in range(GQA_GROUP)]
            q3 = jnp.concatenate([q_ref[pl.ds(r0, tq), hd * LANES:(hd + 1) * LANES] for hd in heads], axis=0)
            o = _softmax_pv([_dot_nt(q3, k_s[s, h])], [v_s[s, h]])
            for g, hd in enumerate(heads):
                o_ref[pl.ds(r0, tq), hd * LANES:(hd + 1) * LANES] = o[g * tq:(g + 1) * tq].astype(o_ref.dtype)

    _for_each_q_block(nb, seq, body)


def _gqa_attn(qb, kb, vb, latent, cache_k=None, cache_v=None, layer=0):
    rows = qb.shape[0]
    seq = DEC_SEQ if latent else SEQ
    nb = 1 if latent else 4
    t_all = seq + (PAST_LEN if latent else 0)
    qw = GQA_HEADS * GQA_HEAD_DIM
    kw = GQA_KV_HEADS * GQA_HEAD_DIM
    row = lambda i: (i, 0)
    in_specs = [
        pl.BlockSpec((nb * seq, qw), row),
        pl.BlockSpec((nb * seq, kw), row),
        pl.BlockSpec((nb * seq, kw), row),
    ]
    args = [qb, kb, vb]
    if latent:
        cspec = pl.BlockSpec((None, None, PAST_LEN, kw), lambda i: (i, layer, 0, 0))
        in_specs += [cspec, cspec]
        args += [cache_k, cache_v]
    return pl.pallas_call(
        functools.partial(_gqa_kernel, latent, nb, seq),
        out_shape=jax.ShapeDtypeStruct((rows, qw), BF16),
        grid=(rows // (nb * seq),),
        in_specs=in_specs,
        out_specs=pl.BlockSpec((nb * seq, qw), row),
        scratch_shapes=[
            pltpu.VMEM((nb, GQA_KV_HEADS, t_all, GQA_HEAD_DIM), BF16),
            pltpu.VMEM((nb, GQA_KV_HEADS, t_all, GQA_HEAD_DIM), BF16),
        ],
        compiler_params=_cparams(("parallel",)),
        name="gqa_lat" if latent else "gqa_ctx",
    )(*args)


def _mla_kernel(latent, nb, seq, *refs):
    if latent:
        (q_ref, ckv_ref, kr_ref, krg_ref, ckvc_ref, krc_ref, wuk_ref, wuv_ref, gn_ref, gr_ref,
         o_ref, k_s, v_s) = refs
    else:
        (q_ref, ckv_ref, kr_ref, krg_ref, wuk_ref, wuv_ref, gn_ref, gr_ref, o_ref, k_s, v_s) = refs
    tq = ATTN_TQ

    def expand(ckv, kr_raw, krg, place):
        c = ckv.astype(BF16)
        kn_all = _dot(c, wuk_ref[...])
        v_all = _dot(c, wuv_ref[...])
        ss_kr = jnp.sum(kr_raw * kr_raw, axis=-1, keepdims=True)
        for h in range(MLA_HEADS):
            hl = slice(h * LANES, (h + 1) * LANES)
            kn = kn_all[:, hl]
            rs = lax.rsqrt((jnp.sum(kn * kn, axis=-1, keepdims=True) + ss_kr) * (1.0 / MLA_QK) + EPS)
            kn = (kn * rs * gn_ref[...]).astype(BF16)
            kp = (krg * rs).astype(BF16)
            vv = v_all[:, hl].astype(BF16)
            for s, lo, n, rows in place:
                k_s[s, h, lo:lo + n, :MLA_NOPE] = kn[rows]
                k_s[s, h, lo:lo + n, MLA_NOPE:MLA_QK] = kp[rows]
                k_s[s, h, lo:lo + n, MLA_QK:] = jnp.zeros((n, MLA_QK_PAD - MLA_QK), BF16)
                v_s[s, h, lo:lo + n, :] = vv[rows]

    expand(ckv_ref[...], kr_ref[...], krg_ref[...],
           [(s, 0, seq, slice(s * seq, (s + 1) * seq)) for s in range(nb)])
    if latent:
        krc = krc_ref[...]
        expand(ckvc_ref[...], krc, krc * gr_ref[...], [(0, seq, PAST_LEN, slice(0, PAST_LEN))])

    def body(s, r0):
        for h in range(MLA_HEADS):
            sc = _dot_nt(q_ref[h, pl.ds(r0, tq), :], k_s[s, h])
            o_ref[pl.ds(r0, tq), h * LANES:(h + 1) * LANES] = _softmax_pv([sc], [v_s[s, h]]).astype(o_ref.dtype)

    _for_each_q_block(nb, seq, body)


def _mla_attn(qc, ckv, kr, krg, lw, latent, cache_ckv=None, cache_kr=None, layer=0):
    rows = qc.shape[1]
    seq = DEC_SEQ if latent else SEQ
    nb = 1 if latent else 4
    t_all = seq + (PAST_LEN if latent else 0)
    row = lambda i: (i, 0)
    full = lambda i: (0, 0)
    in_specs = [
        pl.BlockSpec((MLA_HEADS, nb * seq, MLA_QK_PAD), lambda i: (0, i, 0)),
        pl.BlockSpec((nb * seq, MLA_KV_RANK), row),
        pl.BlockSpec((nb * seq, MLA_ROPE), row),
        pl.BlockSpec((nb * seq, MLA_ROPE), row),
    ]
    args = [qc, ckv, kr, krg]
    if latent:
        in_specs += [
            pl.BlockSpec((None, None, PAST_LEN, MLA_KV_RANK), lambda i: (i, layer, 0, 0)),
            pl.BlockSpec((None, None, PAST_LEN, MLA_ROPE), lambda i: (i, layer, 0, 0)),
        ]
        args += [cache_ckv, cache_kr]
    in_specs += [
        pl.BlockSpec((MLA_KV_RANK, MLA_HEADS * MLA_NOPE), full),
        pl.BlockSpec((MLA_KV_RANK, MLA_HEADS * MLA_V), full),
        pl.BlockSpec((1, MLA_NOPE), full),
        pl.BlockSpec((1, MLA_ROPE), full),
    ]
    args += [lw['mla_w_uk'], lw['mla_w_uv'], lw['mla_k_nope_g'], lw['mla_k_rope_g64']]
    return pl.pallas_call(
        functools.partial(_mla_kernel, latent, nb, seq),
        out_shape=jax.ShapeDtypeStruct((rows, MLA_HEADS * MLA_V), BF16),
        grid=(rows // (nb * seq),),
        in_specs=in_specs,
        out_specs=pl.BlockSpec((nb * seq, MLA_HEADS * MLA_V), row),
        scratch_shapes=[
            pltpu.VMEM((nb, MLA_HEADS, t_all, MLA_QK_PAD), BF16),
            pltpu.VMEM((nb, MLA_HEADS, t_all, MLA_V), BF16),
        ],
        compiler_params=_cparams(("parallel",)),
        name="mla_lat" if latent else "mla_ctx",
    )(*args)


def _out_proj_kernel(ya_ref, yb_ref, yc_ref, x_ref, mod_ref, w_ref, g_ref, x1_ref, h2_ref):
    wa = SSM_CH
    wb = wa + GQA_HEADS * GQA_HEAD_DIM
    o = (_dot(ya_ref[...], w_ref[0:wa, :]) + _dot(yb_ref[...], w_ref[wa:wb, :])
         + _dot(yc_ref[...], w_ref[wb:, :]))
    x1 = x_ref[...] + mod_ref[2:3, :] * o
    x1_ref[...] = x1
    ms = jnp.mean(x1 * x1, axis=-1, keepdims=True)
    y = x1 * lax.rsqrt(ms + EPS) * g_ref[...]
    h2_ref[...] = (y * (1.0 + mod_ref[4:5, :]) + mod_ref[3:4, :]).astype(h2_ref.dtype)


def _mod_map(latent, tm):
    tiles_per_seq = DEC_SEQ // tm
    if latent:
        return lambda i, *_: (1 + i // tiles_per_seq, 0, 0)
    return lambda i, *_: (0, 0, 0)


def _out_proj(ya, yb, yc, x2d, mod_l, lw, latent, tm):
    rows = x2d.shape[0]
    row = lambda i: (i, 0)
    full = lambda i: (0, 0)
    return pl.pallas_call(
        _out_proj_kernel,
        out_shape=(jax.ShapeDtypeStruct((rows, D_MODEL), F32), jax.ShapeDtypeStruct((rows, D_MODEL), BF16)),
        grid=(rows // tm,),
        in_specs=[
            pl.BlockSpec((tm, SSM_CH), row),
            pl.BlockSpec((tm, GQA_HEADS * GQA_HEAD_DIM), row),
            pl.BlockSpec((tm, MLA_HEADS * MLA_V), row),
            pl.BlockSpec((tm, D_MODEL), row),
            pl.BlockSpec((None, 6, D_MODEL), _mod_map(latent, tm)),
            pl.BlockSpec((D_MODEL, D_MODEL), full),
            pl.BlockSpec((1, D_MODEL), full),
        ],
        out_specs=(pl.BlockSpec((tm, D_MODEL), row), pl.BlockSpec((tm, D_MODEL), row)),
        compiler_params=_cparams(("parallel",)),
        name="out_proj",
    )(ya, yb, yc, x2d, mod_l, lw['w_out'], lw['norm_mlp'])


def _mlp_kernel(h2_ref, x1_ref, mod_ref, w1_ref, w2_ref, o_ref):
    j = pl.program_id(1)

    @pl.when(j == 0)
    def _():
        o_ref[...] = jnp.zeros_like(o_ref)

    f = jnp.maximum(_dot(h2_ref[...], w1_ref[...]), 0.0)
    o_ref[...] += _dot((f * f).astype(BF16), w2_ref[...])

    @pl.when(j == pl.num_programs(1) - 1)
    def _():
        o_ref[...] = x1_ref[...] + mod_ref[5:6, :] * o_ref[...]


def _mlp(h2, x1, mod_l, lw, latent, tm, tf):
    rows = h2.shape[0]
    return pl.pallas_call(
        _mlp_kernel,
        out_shape=jax.ShapeDtypeStruct((rows, D_MODEL), F32),
        grid=(rows // tm, D_FF // tf),
        in_specs=[
            pl.BlockSpec((tm, D_MODEL), lambda i, j: (i, 0)),
            pl.BlockSpec((tm, D_MODEL), lambda i, j: (i, 0), pipeline_mode=pl.Buffered(1)),
            pl.BlockSpec((None, 6, D_MODEL), _mod_map(latent, tm)),
            pl.BlockSpec((D_MODEL, tf), lambda i, j: (0, j)),
            pl.BlockSpec((tf, D_MODEL), lambda i, j: (j, 0)),
        ],
        out_specs=pl.BlockSpec((tm, D_MODEL), lambda i, j: (i, 0)),
        compiler_params=_cparams(("parallel", "arbitrary")),
        name="mlp",
    )(h2, x1, mod_l, lw['w_ff1'], lw['w_ff2'])


def _permute_w_in(w):
    base = OFF_QCN
    ckv0 = base + MLA_HEADS * MLA_QK
    qc = w[:, base:ckv0].reshape(D_MODEL, MLA_HEADS, MLA_QK)
    nope = qc[:, :, :MLA_NOPE].reshape(D_MODEL, MLA_HEADS * MLA_NOPE)
    rope = qc[:, :, MLA_NOPE:].reshape(D_MODEL, MLA_HEADS * MLA_ROPE)
    kr = w[:, ckv0 + MLA_KV_RANK:]
    return jnp.concatenate([w[:, :base], nope, rope, w[:, ckv0:ckv0 + MLA_KV_RANK], kr, kr], axis=1).astype(BF16)


def _rope_tables(seq):
    t = jnp.arange(seq)
    row = (t // GRID_W).astype(F32)
    col = (t % GRID_W).astype(F32)

    def table(d):
        quarter = d // 4
        inv = ROPE_BASE ** (-(jnp.arange(quarter, dtype=F32) / quarter))
        ar = row[:, None] * inv[None, :]
        ac = col[:, None] * inv[None, :]
        cos = jnp.concatenate([jnp.cos(ar), jnp.cos(ar), jnp.cos(ac), jnp.cos(ac)], axis=-1)
        sin = jnp.concatenate([-jnp.sin(ar), jnp.sin(ar), -jnp.sin(ac), jnp.sin(ac)], axis=-1)
        reps = LANES // d
        return jnp.tile(cos, (1, reps)), jnp.tile(sin, (1, reps))

    cos_b, sin_b = table(GQA_HEAD_DIM)
    cos_c, sin_c = table(MLA_ROPE)
    return {'cos_b': cos_b, 'sin_b': sin_b, 'cos_c': cos_c, 'sin_c': sin_c}


def _ssm_params(lam_re, lam_im, log_dt, b_re, b_im, c_re, c_im):
    a = lam_re.astype(F32)
    w = lam_im.astype(F32)
    dt = jnp.exp(log_dt.astype(F32))[..., None]
    mag = jnp.exp(a * dt)
    lbr = mag * jnp.cos(w * dt)
    lbi = mag * jnp.sin(w * dt)
    den = a * a + w * w
    cr = (((lbr - 1.0) * a + lbi * w) / den)[..., None]
    ci = ((lbi * a - (lbr - 1.0) * w) / den)[..., None]
    bre = b_re.astype(F32)
    bim = b_im.astype(F32)
    bb_re = cr * bre - ci * bim
    bb_im = cr * bim + ci * bre
    lam_ri = jnp.stack([lbr.reshape(2, SSM_N), lbi.reshape(2, SSM_N)], axis=1)
    lam_b = jnp.broadcast_to(lam_ri[:, :, None, :], (2, 2, SUBLANES, SSM_N))
    gh = SSM_GROUPS // 2
    eye = jnp.eye(gh, dtype=F32)

    def bmat(x):
        x = x.reshape(2, 2, gh, SSM_STATE, SSM_GROUP)
        return jnp.einsum('dkgpc,gh->dkgchp', x, eye).reshape(2, 2, gh * SSM_GROUP, gh * SSM_STATE)

    def cmat(x):
        x = x.reshape(2, 2, gh, SSM_GROUP, SSM_STATE)
        return jnp.einsum('dkgcp,gh->dkgphc', x, eye).reshape(2, 2, gh * SSM_STATE, gh * SSM_GROUP)

    b_cat = jnp.concatenate([bmat(bb_re), bmat(bb_im)], axis=-1).astype(BF16)
    return {'lam': lam_b, 'b': b_cat, 'c_re': cmat(c_re.astype(F32)).astype(BF16),
            'c_im': cmat(c_im.astype(F32)).astype(BF16)}


def _trunk_layer(x2d, mod_l, lw, sp, tabs, latent, ctx, layer):
    seq = DEC_SEQ if latent else SEQ
    bsz = x2d.shape[0] // seq
    u, qb, kb, vb, qc, ckv, kr, krg = _in_proj(x2d, mod_l, latent, lw, tabs, TM_PROJ)

    u3 = u.reshape(bsz, seq, SSM_CH)
    y_tm, h_t = _ssm_scan(u3, ctx['h0'], sp)
    ya = _ssm_glu(u3, y_tm, lw['ssm_d'], lw['ssm_w_glu']).reshape(bsz * seq, SSM_CH)

    if latent:
        yb = _gqa_attn(qb, kb, vb, True, ctx['k'], ctx['v'], layer)
        yc = _mla_attn(qc, ckv, kr, krg, lw, True, ctx['ckv'], ctx['kr'], layer)
    else:
        yb = _gqa_attn(qb, kb, vb, False)
        yc = _mla_attn(qc, ckv, kr, krg, lw, False)

    x1, h2 = _out_proj(ya, yb, yc, x2d, mod_l, lw, latent, TM_PROJ)
    x2 = _mlp(h2, x1, mod_l, lw, latent, TM_MLP, TF_MLP)
    return x2, (kb, vb, ckv, kr, h_t)


def kernel(x_prompt, x_sample, cache_attn_k, cache_attn_v, cache_mla_ckv, cache_mla_krope, state_ssm, c, c_ctx, w_mod, b_mod, norm_mix, norm_mlp, w_in, gqa_q_norm, gqa_k_norm, mla_kv_norm, mla_q_norm, mla_k_norm, mla_w_uk, mla_w_uv, ssm_lam_re, ssm_lam_im, ssm_log_dt, ssm_b_re, ssm_b_im, ssm_c_re, ssm_c_im, ssm_d, ssm_w_glu, w_out, w_ff1, w_ff2):
    cvec = jnp.zeros((N_MOD, D_MODEL), F32).at[0].set(c_ctx).at[1:1 + DEC_BATCH].set(c)
    mod = _adaln(cvec, w_mod, b_mod.reshape(DEPTH, 1, 6 * D_MODEL)).reshape(DEPTH, N_MOD, 6, D_MODEL)

    tabs = _rope_tables(DEC_SEQ)
    cache_k = cache_attn_k.reshape(DEC_BATCH, DEPTH, PAST_LEN, GQA_KV_HEADS * GQA_HEAD_DIM)
    cache_v = cache_attn_v.reshape(DEC_BATCH, DEPTH, PAST_LEN, GQA_KV_HEADS * GQA_HEAD_DIM)

    xp = x_prompt.reshape(BATCH * SEQ, D_MODEL)
    xs = x_sample.reshape(DEC_BATCH * DEC_SEQ, D_MODEL)
    new_k, new_v, new_ckv, new_kr, new_ssm = [], [], [], [], []
    h0_zero = jnp.zeros((BATCH // SUBLANES, 2, 2, SUBLANES, SSM_N), F32)
    for l in range(DEPTH):
        row2 = lambda a: a[l].reshape(1, -1)
        gq = mla_q_norm[l]
        gk = mla_k_norm[l]
        lw = {
            'norm_mix': row2(norm_mix), 'norm_mlp': row2(norm_mlp),
            'w_in': _permute_w_in(w_in[l]),
            'w_out': w_out[l].astype(BF16), 'w_ff1': w_ff1[l].astype(BF16), 'w_ff2': w_ff2[l].astype(BF16),
            'gqa_q_norm': row2(gqa_q_norm), 'gqa_k_norm': row2(gqa_k_norm), 'mla_kv_norm': row2(mla_kv_norm),
            'mla_q_nope_g': gq[:MLA_NOPE].reshape(1, -1),
            'mla_q_rope_g': jnp.tile(gq[MLA_NOPE:], 2).reshape(1, -1),
            'mla_k_nope_g': gk[:MLA_NOPE].reshape(1, -1),
            'mla_k_rope_g': jnp.tile(gk[MLA_NOPE:], 2).reshape(1, -1),
            'mla_k_rope_g64': gk[MLA_NOPE:].reshape(1, -1),
            'mla_w_uk': mla_w_uk[l].reshape(MLA_KV_RANK, MLA_HEADS * MLA_NOPE).astype(BF16),
            'mla_w_uv': mla_w_uv[l].reshape(MLA_KV_RANK, MLA_HEADS * MLA_V).astype(BF16),
            'ssm_d': row2(ssm_d), 'ssm_w_glu': ssm_w_glu[l].astype(BF16),
        }
        sp = _ssm_params(ssm_lam_re[l], ssm_lam_im[l], ssm_log_dt[l], ssm_b_re[l], ssm_b_im[l],
                         ssm_c_re[l], ssm_c_im[l])
        mod_l = mod[l]

        xp, (k, v, ckv_n, kr, h_t) = _trunk_layer(xp, mod_l, lw, sp, tabs, False, {'h0': h0_zero}, l)
        new_k.append(k.reshape(BATCH, SEQ, GQA_KV_HEADS, GQA_HEAD_DIM))
        new_v.append(v.reshape(BATCH, SEQ, GQA_KV_HEADS, GQA_HEAD_DIM))
        new_ckv.append(ckv_n.reshape(BATCH, SEQ, MLA_KV_RANK))
        new_kr.append(kr.reshape(BATCH, SEQ, MLA_ROPE))
        hs = h_t.transpose(0, 3, 1, 4, 2).reshape(BATCH, 2, SSM_GROUPS, SSM_STATE, 2)
        new_ssm.append(hs)

        s0 = state_ssm[:, l].reshape(DEC_BATCH, 2, SSM_N, 2)
        h0 = s0.transpose(1, 3, 0, 2)[None]
        ctx = {'k': cache_k, 'v': cache_v, 'ckv': cache_mla_ckv, 'kr': cache_mla_krope, 'h0': h0}
        xs, _ = _trunk_layer(xs, mod_l, lw, sp, tabs, True, ctx, l)

    return (xp.reshape(BATCH, SEQ, D_MODEL), xs.reshape(DEC_BATCH, DEC_SEQ, D_MODEL),
            jnp.stack(new_k, axis=1), jnp.stack(new_v, axis=1), jnp.stack(new_ckv, axis=1),
            jnp.stack(new_kr, axis=1), jnp.stack(new_ssm, axis=1))
```

```python
import functools
import math

import numpy as np
import jax
import jax.numpy as jnp
from jax import lax
from jax.experimental import pallas as pl
from jax.experimental.pallas import tpu as pltpu

D_MODEL = 2048
BATCH = 32
SEQ = 256
DEPTH = 2
DEC_BATCH = 8
DEC_SEQ = 1024
PAST_LEN = 512
GRID_W = 64
ROPE_BASE = 10000.0
EPS = 1e-6
SSM_CH = 512
SSM_GROUP = 16
SSM_GROUPS = SSM_CH // SSM_GROUP
SSM_STATE = 64
SSM_N = SSM_GROUPS * SSM_STATE
GQA_HEADS = 6
GQA_KV_HEADS = 2
GQA_GROUP = GQA_HEADS // GQA_KV_HEADS
GQA_HEAD_DIM = 128
MLA_HEADS = 6
MLA_NOPE = 128
MLA_ROPE = 64
MLA_QK = MLA_NOPE + MLA_ROPE
MLA_QK_PAD = 256
MLA_V = 128
MLA_KV_RANK = 512
D_FF = 4 * D_MODEL
N_MOD = 16

OFF_U = 0
OFF_QB = OFF_U + SSM_CH
OFF_KB = OFF_QB + GQA_HEADS * GQA_HEAD_DIM
OFF_VB = OFF_KB + GQA_KV_HEADS * GQA_HEAD_DIM
OFF_QCN = OFF_VB + GQA_KV_HEADS * GQA_HEAD_DIM
OFF_QCR = OFF_QCN + MLA_HEADS * MLA_NOPE
OFF_CKV = OFF_QCR + MLA_HEADS * MLA_ROPE
OFF_KR = OFF_CKV + MLA_KV_RANK
IN_WIDTH_P = OFF_KR + 2 * MLA_ROPE

LANES = 128
SUBLANES = 8
VMEM_LIMIT = 56 * 1024 * 1024

TM_PROJ = 512
TM_MLP = 1024
TF_MLP = 512

BF16 = jnp.bfloat16
F32 = jnp.float32


def _cparams(sem):
    return pltpu.CompilerParams(dimension_semantics=sem, vmem_limit_bytes=VMEM_LIMIT)


def _dot(a, b):
    return jnp.dot(a, b, preferred_element_type=F32)


def _dot_nt(a, b):
    return lax.dot_general(a, b, (((1,), (1,)), ((), ())), preferred_element_type=F32)


def _adaln_kernel(c_ref, w_ref, b_ref, o_ref):
    c = c_ref[...]
    s = (c * jax.nn.sigmoid(c)).astype(BF16)
    o_ref[...] = _dot(s, w_ref[...].astype(BF16)) + b_ref[...]


def _adaln(cvec, w_mod, b_mod):
    tn = 1024
    return pl.pallas_call(
        _adaln_kernel,
        out_shape=jax.ShapeDtypeStruct((DEPTH, N_MOD, 6 * D_MODEL), F32),
        grid=(DEPTH, 6 * D_MODEL // tn),
        in_specs=[
            pl.BlockSpec((N_MOD, D_MODEL), lambda l, j: (0, 0)),
            pl.BlockSpec((None, D_MODEL, tn), lambda l, j: (l, 0, j)),
            pl.BlockSpec((None, 1, tn), lambda l, j: (l, 0, j)),
        ],
        out_specs=pl.BlockSpec((None, N_MOD, tn), lambda l, j: (l, 0, j)),
        compiler_params=_cparams(("parallel", "parallel")),
        name="adaln",
    )(cvec, w_mod, b_mod)


def _swap_halves(x, block):
    lane = lax.broadcasted_iota(jnp.int32, x.shape, 1)
    first = (lane % (2 * block)) < block
    return jnp.where(first, pltpu.roll(x, LANES - block, 1), pltpu.roll(x, block, 1))


def _rope(x, cos, sin_signed, block):
    return x * cos + _swap_halves(x, block) * sin_signed


def _in_proj_kernel(latent, x_ref, mod_ref, gmix_ref, w_ref, gq_ref, gk_ref, gkv_ref, gqn_ref, gqr_ref, gkr_ref,
                    cosb_ref, sinb_ref, cosc_ref, sinc_ref,
                    u_ref, qb_ref, kb_ref, vb_ref, qc_ref, ckv_ref, kr_ref, krg_ref):
    x = x_ref[...]
    ms = jnp.mean(x * x, axis=-1, keepdims=True)
    y = x * lax.rsqrt(ms + EPS) * gmix_ref[...]
    h = (y * (1.0 + mod_ref[1:2, :]) + mod_ref[0:1, :]).astype(BF16)

    def proj(off, width):
        return _dot(h, w_ref[:, off:off + width])

    u_ref[...] = proj(OFF_U, SSM_CH)

    scale_b = 1.0 / math.sqrt(GQA_HEAD_DIM)
    zq = proj(OFF_QB, GQA_HEADS * GQA_HEAD_DIM)
    for hd in range(GQA_HEADS):
        col = zq[:, hd * LANES:(hd + 1) * LANES]
        q = col * lax.rsqrt(jnp.mean(col * col, axis=-1, keepdims=True) + EPS) * gq_ref[...]
        if latent:
            q = _rope(q, cosb_ref[...], sinb_ref[...], GQA_HEAD_DIM // 4)
        qb_ref[:, hd * LANES:(hd + 1) * LANES] = (q * scale_b).astype(qb_ref.dtype)
    zk = proj(OFF_KB, GQA_KV_HEADS * GQA_HEAD_DIM)
    for hd in range(GQA_KV_HEADS):
        col = zk[:, hd * LANES:(hd + 1) * LANES]
        k = col * lax.rsqrt(jnp.mean(col * col, axis=-1, keepdims=True) + EPS) * gk_ref[...]
        if latent:
            k = _rope(k, cosb_ref[...], sinb_ref[...], GQA_HEAD_DIM // 4)
        kb_ref[:, hd * LANES:(hd + 1) * LANES] = k.astype(kb_ref.dtype)
    vb_ref[...] = proj(OFF_VB, GQA_KV_HEADS * GQA_HEAD_DIM).astype(vb_ref.dtype)

    scale_c = 1.0 / math.sqrt(MLA_QK)
    zn = proj(OFF_QCN, MLA_HEADS * MLA_NOPE)
    zr = proj(OFF_QCR, MLA_HEADS * MLA_ROPE)
    lane = lax.broadcasted_iota(jnp.int32, (x.shape[0], LANES), 1)
    low = lane < MLA_ROPE
    for pair in range(MLA_HEADS // 2):
        colr = zr[:, pair * LANES:(pair + 1) * LANES]
        sq = colr * colr
        ss_lo = jnp.sum(jnp.where(low, sq, 0.0), axis=-1, keepdims=True)
        ss_hi = jnp.sum(jnp.where(low, 0.0, sq), axis=-1, keepdims=True)
        rs = []
        for half, ss_r in ((0, ss_lo), (1, ss_hi)):
            hd = 2 * pair + half
            coln = zn[:, hd * LANES:(hd + 1) * LANES]
            ss = jnp.sum(coln * coln, axis=-1, keepdims=True) + ss_r
            r = lax.rsqrt(ss * (1.0 / MLA_QK) + EPS)
            rs.append(r)
            qc_ref[hd, :, :MLA_NOPE] = (coln * r * gqn_ref[...] * scale_c).astype(qc_ref.dtype)
        qr = colr * jnp.where(low, rs[0], rs[1]) * gqr_ref[...]
        if latent:
            qr = _rope(qr, cosc_ref[...], sinc_ref[...], MLA_ROPE // 4)
        qr = qr * scale_c
        zeros = jnp.zeros((x.shape[0], MLA_QK_PAD - MLA_QK), qc_ref.dtype)
        for half in range(2):
            hd = 2 * pair + half
            qc_ref[hd, :, MLA_NOPE:MLA_QK] = qr[:, half * MLA_ROPE:(half + 1) * MLA_ROPE].astype(qc_ref.dtype)
            qc_ref[hd, :, MLA_QK:] = zeros

    zc = proj(OFF_CKV, MLA_KV_RANK)
    ckv_ref[...] = (zc * lax.rsqrt(jnp.mean(zc * zc, axis=-1, keepdims=True) + EPS) * gkv_ref[...]).astype(ckv_ref.dtype)
    zkr = proj(OFF_KR, 2 * MLA_ROPE)
    kr_ref[...] = zkr[:, :MLA_ROPE]
    krg = zkr * gkr_ref[...]
    if latent:
        krg = _rope(krg, cosc_ref[...], sinc_ref[...], MLA_ROPE // 4)
    krg_ref[...] = krg[:, :MLA_ROPE]


def _lspec(block_tail, layer):
    zeros = (0,) * len(block_tail)
    return pl.BlockSpec((None,) + tuple(block_tail), lambda *_: (layer,) + zeros)


def _mod_spec(latent, tm, layer):
    tiles_per_seq = DEC_SEQ // tm
    if latent:
        return pl.BlockSpec((None, None, 6, D_MODEL), lambda i, *_: (layer, 1 + i // tiles_per_seq, 0, 0))
    return pl.BlockSpec((None, None, 6, D_MODEL), lambda i, *_: (layer, 0, 0, 0))


def _in_proj(x2d, mod, latent, lw, tabs, tm, layer):
    rows = x2d.shape[0]
    n_tiles = rows // tm
    tiles_per_seq = DEC_SEQ // tm
    if latent:
        tab_map = lambda i: (i % tiles_per_seq, 0)
    else:
        tab_map = lambda i: (0, 0)
    row = lambda i: (i, 0)
    act_dt = BF16 if latent else F32
    in_specs = [
        pl.BlockSpec((tm, D_MODEL), row),
        _mod_spec(latent, tm, layer),
        _lspec((1, D_MODEL), layer),
        _lspec((D_MODEL, IN_WIDTH_P), layer),
        _lspec((1, LANES), layer),
        _lspec((1, LANES), layer),
        _lspec((1, MLA_KV_RANK), layer),
        _lspec((1, LANES), layer),
        _lspec((1, LANES), layer),
        _lspec((1, LANES), layer),
        pl.BlockSpec((tm, LANES), tab_map),
        pl.BlockSpec((tm, LANES), tab_map),
        pl.BlockSpec((tm, LANES), tab_map),
        pl.BlockSpec((tm, LANES), tab_map),
    ]
    out_shape = (
        jax.ShapeDtypeStruct((rows, SSM_CH), F32),
        jax.ShapeDtypeStruct((rows, GQA_HEADS * GQA_HEAD_DIM), BF16),
        jax.ShapeDtypeStruct((rows, GQA_KV_HEADS * GQA_HEAD_DIM), act_dt),
        jax.ShapeDtypeStruct((rows, GQA_KV_HEADS * GQA_HEAD_DIM), act_dt),
        jax.ShapeDtypeStruct((MLA_HEADS, rows, MLA_QK_PAD), BF16),
        jax.ShapeDtypeStruct((rows, MLA_KV_RANK), act_dt),
        jax.ShapeDtypeStruct((rows, MLA_ROPE), F32),
        jax.ShapeDtypeStruct((rows, MLA_ROPE), F32),
    )
    out_specs = (
        pl.BlockSpec((tm, SSM_CH), row),
        pl.BlockSpec((tm, GQA_HEADS * GQA_HEAD_DIM), row),
        pl.BlockSpec((tm, GQA_KV_HEADS * GQA_HEAD_DIM), row),
        pl.BlockSpec((tm, GQA_KV_HEADS * GQA_HEAD_DIM), row),
        pl.BlockSpec((MLA_HEADS, tm, MLA_QK_PAD), lambda i: (0, i, 0)),
        pl.BlockSpec((tm, MLA_KV_RANK), row),
        pl.BlockSpec((tm, MLA_ROPE), row),
        pl.BlockSpec((tm, MLA_ROPE), row),
    )
    return pl.pallas_call(
        functools.partial(_in_proj_kernel, latent),
        out_shape=out_shape,
        grid=(n_tiles,),
        in_specs=in_specs,
        out_specs=out_specs,
        compiler_params=_cparams(("parallel",)),
        name="in_proj_lat" if latent else "in_proj_ctx",
    )(x2d, mod, lw['norm_mix'], lw['w_in'], lw['gqa_q_norm'], lw['gqa_k_norm'], lw['mla_kv_norm'],
      lw['mla_q_nope_g'], lw['mla_q_rope_g'], lw['mla_k_rope_g'],
      tabs['cos_b'], tabs['sin_b'], tabs['cos_c'], tabs['sin_c'])


SSM_TC = 128
SSM_LANE_SPLIT = 2
SSM_UNROLL = 8


def _ssm_scan_kernel(n_chunks, u_ref, h0_ref, lam_ref, b_ref, cre_ref, cim_ref, y_ref, hT_ref,
                     utm_ref, sre_ref, sim_ref, h_ref):
    d = pl.program_id(1)
    i = pl.program_id(2)
    tc = SSM_TC

    @pl.when(i == 0)
    def _():
        h_ref[...] = h0_ref[...]

    for b in range(SUBLANES):
        for j in range(SSM_CH // LANES):
            utm_ref[j, pl.ds(b, tc, stride=SUBLANES), :] = u_ref[b, :, j * LANES:(j + 1) * LANES]

    half_n = SSM_N // 2
    half_c = SSM_CH // 2
    for k in range(2):
        uk = jnp.concatenate([utm_ref[2 * k], utm_ref[2 * k + 1]], axis=-1).astype(BF16)
        r = _dot(uk, b_ref[k])
        sre_ref[:, k * half_n:(k + 1) * half_n] = r[:, :half_n]
        sim_ref[:, k * half_n:(k + 1) * half_n] = r[:, half_n:]

    piece = SSM_N // SSM_LANE_SPLIT
    for p in range(SSM_LANE_SPLIT):
        sl = slice(p * piece, (p + 1) * piece)
        lr = lam_ref[0, :, sl]
        li = lam_ref[1, :, sl]

        def step(s, carry):
            hr, hi = carry
            t = s + d * (tc - 1 - 2 * s)
            rowi = pl.multiple_of(t * SUBLANES, SUBLANES)
            nr = lr * hr - li * hi + sre_ref[pl.ds(rowi, SUBLANES), sl]
            ni = lr * hi + li * hr + sim_ref[pl.ds(rowi, SUBLANES), sl]
            sre_ref[pl.ds(rowi, SUBLANES), sl] = nr
            sim_ref[pl.ds(rowi, SUBLANES), sl] = ni
            return nr, ni

        hr, hi = lax.fori_loop(0, tc, step, (h_ref[0, :, sl], h_ref[1, :, sl]), unroll=SSM_UNROLL)
        h_ref[0, :, sl] = hr
        h_ref[1, :, sl] = hi

    for k in range(2):
        yk = (_dot(sre_ref[:, k * half_n:(k + 1) * half_n].astype(BF16), cre_ref[k])
              - _dot(sim_ref[:, k * half_n:(k + 1) * half_n].astype(BF16), cim_ref[k]))
        y_ref[2 * k] = yk[:, :LANES]
        y_ref[2 * k + 1] = yk[:, LANES:]

    @pl.when(i == n_chunks - 1)
    def _():
        hT_ref[...] = h_ref[...]


def _ssm_scan(u3, h0, sp, layer):
    bsz, t, _ = u3.shape
    n_groups = bsz // SUBLANES
    n_chunks = t // SSM_TC
    tc = SSM_TC

    def chunk(d, i):
        return i + d * (n_chunks - 1 - 2 * i)

    return pl.pallas_call(
        functools.partial(_ssm_scan_kernel, n_chunks),
        out_shape=(
            jax.ShapeDtypeStruct((2, n_groups, SSM_CH // LANES, t * SUBLANES, LANES), F32),
            jax.ShapeDtypeStruct((n_groups, 2, 2, SUBLANES, SSM_N), F32),
        ),
        grid=(n_groups, 2, n_chunks),
        in_specs=[
            pl.BlockSpec((SUBLANES, tc, SSM_CH), lambda g, d, i: (g, chunk(d, i), 0)),
            pl.BlockSpec((None, None, 2, SUBLANES, SSM_N), lambda g, d, i: (g, d, 0, 0, 0)),
            pl.BlockSpec((None, None, 2, SUBLANES, SSM_N), lambda g, d, i: (layer, d, 0, 0, 0)),
            pl.BlockSpec((None, None, 2, SSM_CH // 2, SSM_N), lambda g, d, i: (layer, d, 0, 0, 0)),
            pl.BlockSpec((None, None, 2, SSM_N // 2, SSM_CH // 2), lambda g, d, i: (layer, d, 0, 0, 0)),
            pl.BlockSpec((None, None, 2, SSM_N // 2, SSM_CH // 2), lambda g, d, i: (layer, d, 0, 0, 0)),
        ],
        out_specs=(
            pl.BlockSpec((None, None, SSM_CH // LANES, tc * SUBLANES, LANES),
                         lambda g, d, i: (d, g, 0, chunk(d, i), 0)),
            pl.BlockSpec((None, None, 2, SUBLANES, SSM_N), lambda g, d, i: (g, d, 0, 0, 0)),
        ),
        scratch_shapes=[
            pltpu.VMEM((SSM_CH // LANES, tc * SUBLANES, LANES), F32),
            pltpu.VMEM((tc * SUBLANES, SSM_N), F32),
            pltpu.VMEM((tc * SUBLANES, SSM_N), F32),
            pltpu.VMEM((2, SUBLANES, SSM_N), F32),
        ],
        compiler_params=_cparams(("parallel", "parallel", "arbitrary")),
        name="ssm_scan",
    )(u3, h0, sp['lam'], sp['b'], sp['c_re'], sp['c_im'])


def _ssm_glu_kernel(u_ref, y_ref, d_ref, w_ref, o_ref, ycat_ref):
    tc = SSM_TC
    for b in range(SUBLANES):
        for j in range(SSM_CH // LANES):
            sl = slice(j * LANES, (j + 1) * LANES)
            rows_b = pl.ds(b, tc, stride=SUBLANES)
            ycat_ref[b * tc:(b + 1) * tc, sl] = (y_ref[0, j, rows_b, :] + y_ref[1, j, rows_b, :]
                                                + d_ref[:, sl] * u_ref[b, :, sl])
    zg = _dot(ycat_ref[...].astype(BF16), w_ref[...])
    out = zg[:, :SSM_CH] * jax.nn.sigmoid(zg[:, SSM_CH:])
    o_ref[...] = out.reshape(SUBLANES, tc, SSM_CH).astype(o_ref.dtype)


def _ssm_glu(u3, y, d_skip, w_glu, layer):
    bsz, t, _ = u3.shape
    n_groups = bsz // SUBLANES
    n_chunks = t // SSM_TC
    tc = SSM_TC
    return pl.pallas_call(
        _ssm_glu_kernel,
        out_shape=jax.ShapeDtypeStruct((bsz, t, SSM_CH), BF16),
        grid=(n_groups, n_chunks),
        in_specs=[
            pl.BlockSpec((SUBLANES, tc, SSM_CH), lambda g, i: (g, i, 0)),
            pl.BlockSpec((2, None, SSM_CH // LANES, tc * SUBLANES, LANES), lambda g, i: (0, g, 0, i, 0)),
            _lspec((1, SSM_CH), layer),
            _lspec((SSM_CH, 2 * SSM_CH), layer),
        ],
        out_specs=pl.BlockSpec((SUBLANES, tc, SSM_CH), lambda g, i: (g, i, 0)),
        scratch_shapes=[pltpu.VMEM((SUBLANES * tc, SSM_CH), F32)],
        compiler_params=_cparams(("parallel", "parallel")),
        name="ssm_glu",
    )(u3, y, d_skip, w_glu)


def _ones_column(n):
    lane = lax.broadcasted_iota(jnp.int32, (n, LANES), 1)
    return jnp.where(lane == 0, 1.0, 0.0).astype(BF16)


def _softmax_pv(scores, v_ext):
    m = jnp.max(scores, axis=-1, keepdims=True)
    p = jnp.exp((scores - m).astype(BF16))
    o = _dot(p, v_ext)
    return o[:, :LANES] / o[:, LANES:LANES + 1]


ATTN_TQ = 256


def _for_each_q_block(nb, seq, body):
    nq = seq // ATTN_TQ
    for s in range(nb):
        if nq == 1:
            body(s, s * seq)
        else:
            def it(qi, carry, s=s):
                body(s, pl.multiple_of(s * seq + qi * ATTN_TQ, ATTN_TQ))
                return carry
            lax.fori_loop(0, nq, it, 0)


def _gqa_kernel(latent, nb, seq, *refs):
    if latent:
        q_ref, k_ref, v_ref, kc_ref, vc_ref, o_ref, k_s, v_s = refs
    else:
        q_ref, k_ref, v_ref, o_ref, k_s, v_s = refs
    tq = ATTN_TQ
    for s in range(nb):
        for h in range(GQA_KV_HEADS):
            hl = slice(h * LANES, (h + 1) * LANES)
            k_s[s, h, 0:seq, :] = k_ref[s * seq:(s + 1) * seq, hl].astype(BF16)
            v_s[s, h, 0:seq, :LANES] = v_ref[s * seq:(s + 1) * seq, hl].astype(BF16)
            if latent:
                k_s[s, h, seq:, :] = kc_ref[:, hl].astype(BF16)
                v_s[s, h, seq:, :LANES] = vc_ref[:, hl].astype(BF16)
            v_s[s, h, :, LANES:] = _ones_column(v_s.shape[2])

    def body(s, r0):
        for h in range(GQA_KV_HEADS):
            heads = [h * GQA_GROUP + g for g in range(GQA_GROUP)]
            q3 = jnp.concatenate([q_ref[pl.ds(r0, tq), hd * LANES:(hd + 1) * LANES] for hd in heads], axis=0)
            o = _softmax_pv(_dot_nt(q3, k_s[s, h]), v_s[s, h])
            for g, hd in enumerate(heads):
                o_ref[pl.ds(r0, tq), hd * LANES:(hd + 1) * LANES] = o[g * tq:(g + 1) * tq].astype(o_ref.dtype)

    _for_each_q_block(nb, seq, body)


def _gqa_attn(qb, kb, vb, latent, cache_k=None, cache_v=None, layer=0):
    rows = qb.shape[0]
    seq = DEC_SEQ if latent else SEQ
    nb = 1 if latent else 4
    t_all = seq + (PAST_LEN if latent else 0)
    qw = GQA_HEADS * GQA_HEAD_DIM
    kw = GQA_KV_HEADS * GQA_HEAD_DIM
    row = lambda i: (i, 0)
    in_specs = [
        pl.BlockSpec((nb * seq, qw), row),
        pl.BlockSpec((nb * seq, kw), row),
        pl.BlockSpec((nb * seq, kw), row),
    ]
    args = [qb, kb, vb]
    if latent:
        cspec = pl.BlockSpec((None, None, PAST_LEN, kw), lambda i: (i, layer, 0, 0))
        in_specs += [cspec, cspec]
        args += [cache_k, cache_v]
    return pl.pallas_call(
        functools.partial(_gqa_kernel, latent, nb, seq),
        out_shape=jax.ShapeDtypeStruct((rows, qw), BF16),
        grid=(rows // (nb * seq),),
        in_specs=in_specs,
        out_specs=pl.BlockSpec((nb * seq, qw), row),
        scratch_shapes=[
            pltpu.VMEM((nb, GQA_KV_HEADS, t_all, GQA_HEAD_DIM), BF16),
            pltpu.VMEM((nb, GQA_KV_HEADS, t_all, 2 * LANES), BF16),
        ],
        compiler_params=_cparams(("parallel",)),
        name="gqa_lat" if latent else "gqa_ctx",
    )(*args)


def _mla_kernel(latent, nb, seq, *refs):
    if latent:
        (q_ref, ckv_ref, kr_ref, krg_ref, ckvc_ref, krc_ref, wuk_ref, wuv_ref, gn_ref, gr_ref,
         o_ref, k_s, v_s) = refs
    else:
        (q_ref, ckv_ref, kr_ref, krg_ref, wuk_ref, wuv_ref, gn_ref, gr_ref, o_ref, k_s, v_s) = refs
    tq = ATTN_TQ

    def expand(ckv, kr_raw, krg, place):
        c = ckv.astype(BF16)
        kn_all = _dot(c, wuk_ref[...])
        v_all = _dot(c, wuv_ref[...])
        ss_kr = jnp.sum(kr_raw * kr_raw, axis=-1, keepdims=True)
        for h in range(MLA_HEADS):
            hl = slice(h * LANES, (h + 1) * LANES)
            kn = kn_all[:, hl]
            rs = lax.rsqrt((jnp.sum(kn * kn, axis=-1, keepdims=True) + ss_kr) * (1.0 / MLA_QK) + EPS)
            kn = (kn * rs * gn_ref[...]).astype(BF16)
            kp = (krg * rs).astype(BF16)
            vv = v_all[:, hl].astype(BF16)
            for s, lo, n, rows in place:
                k_s[s, h, lo:lo + n, :MLA_NOPE] = kn[rows]
                k_s[s, h, lo:lo + n, MLA_NOPE:MLA_QK] = kp[rows]
                k_s[s, h, lo:lo + n, MLA_QK:] = jnp.zeros((n, MLA_QK_PAD - MLA_QK), BF16)
                v_s[s, h, lo:lo + n, :MLA_V] = vv[rows]
                v_s[s, h, lo:lo + n, MLA_V:] = _ones_column(n)

    expand(ckv_ref[...], kr_ref[...], krg_ref[...],
           [(s, 0, seq, slice(s * seq, (s + 1) * seq)) for s in range(nb)])
    if latent:
        krc = krc_ref[...]
        expand(ckvc_ref[...], krc, krc * gr_ref[...], [(0, seq, PAST_LEN, slice(0, PAST_LEN))])

    def body(s, r0):
        for h in range(MLA_HEADS):
            sc = _dot_nt(q_ref[h, pl.ds(r0, tq), :], k_s[s, h])
            o_ref[pl.ds(r0, tq), h * LANES:(h + 1) * LANES] = _softmax_pv(sc, v_s[s, h]).astype(o_ref.dtype)

    _for_each_q_block(nb, seq, body)


def _mla_attn(qc, ckv, kr, krg, lw, latent, cache_ckv=None, cache_kr=None, layer=0):
    rows = qc.shape[1]
    seq = DEC_SEQ if latent else SEQ
    nb = 1 if latent else 4
    t_all = seq + (PAST_LEN if latent else 0)
    row = lambda i: (i, 0)
    full = lambda i: (0, 0)
    in_specs = [
        pl.BlockSpec((MLA_HEADS, nb * seq, MLA_QK_PAD), lambda i: (0, i, 0)),
        pl.BlockSpec((nb * seq, MLA_KV_RANK), row),
        pl.BlockSpec((nb * seq, MLA_ROPE), row),
        pl.BlockSpec((nb * seq, MLA_ROPE), row),
    ]
    args = [qc, ckv, kr, krg]
    if latent:
        in_specs += [
            pl.BlockSpec((None, None, PAST_LEN, MLA_KV_RANK), lambda i: (i, layer, 0, 0)),
            pl.BlockSpec((None, None, PAST_LEN, MLA_ROPE), lambda i: (i, layer, 0, 0)),
        ]
        args += [cache_ckv, cache_kr]
    in_specs += [
        _lspec((MLA_KV_RANK, MLA_HEADS * MLA_NOPE), layer),
        _lspec((MLA_KV_RANK, MLA_HEADS * MLA_V), layer),
        _lspec((1, MLA_NOPE), layer),
        _lspec((1, MLA_ROPE), layer),
    ]
    args += [lw['mla_w_uk'], lw['mla_w_uv'], lw['mla_k_nope_g'], lw['mla_k_rope_g64']]
    return pl.pallas_call(
        functools.partial(_mla_kernel, latent, nb, seq),
        out_shape=jax.ShapeDtypeStruct((rows, MLA_HEADS * MLA_V), BF16),
        grid=(rows // (nb * seq),),
        in_specs=in_specs,
        out_specs=pl.BlockSpec((nb * seq, MLA_HEADS * MLA_V), row),
        scratch_shapes=[
            pltpu.VMEM((nb, MLA_HEADS, t_all, MLA_QK_PAD), BF16),
            pltpu.VMEM((nb, MLA_HEADS, t_all, MLA_V + LANES), BF16),
        ],
        compiler_params=_cparams(("parallel",)),
        name="mla_lat" if latent else "mla_ctx",
    )(*args)


def _out_proj_kernel(ya_ref, yb_ref, yc_ref, x_ref, mod_ref, w_ref, g_ref, x1_ref, h2_ref):
    wa = SSM_CH
    wb = wa + GQA_HEADS * GQA_HEAD_DIM
    o = (_dot(ya_ref[...], w_ref[0:wa, :]) + _dot(yb_ref[...], w_ref[wa:wb, :])
         + _dot(yc_ref[...], w_ref[wb:, :]))
    x1 = x_ref[...] + mod_ref[2:3, :] * o
    x1_ref[...] = x1
    ms = jnp.mean(x1 * x1, axis=-1, keepdims=True)
    y = x1 * lax.rsqrt(ms + EPS) * g_ref[...]
    h2_ref[...] = (y * (1.0 + mod_ref[4:5, :]) + mod_ref[3:4, :]).astype(h2_ref.dtype)


def _out_proj(ya, yb, yc, x2d, mod, lw, latent, tm, layer):
    rows = x2d.shape[0]
    row = lambda i: (i, 0)
    return pl.pallas_call(
        _out_proj_kernel,
        out_shape=(jax.ShapeDtypeStruct((rows, D_MODEL), F32), jax.ShapeDtypeStruct((rows, D_MODEL), BF16)),
        grid=(rows // tm,),
        in_specs=[
            pl.BlockSpec((tm, SSM_CH), row),
            pl.BlockSpec((tm, GQA_HEADS * GQA_HEAD_DIM), row),
            pl.BlockSpec((tm, MLA_HEADS * MLA_V), row),
            pl.BlockSpec((tm, D_MODEL), row),
            _mod_spec(latent, tm, layer),
            _lspec((D_MODEL, D_MODEL), layer),
            _lspec((1, D_MODEL), layer),
        ],
        out_specs=(pl.BlockSpec((tm, D_MODEL), row), pl.BlockSpec((tm, D_MODEL), row)),
        compiler_params=_cparams(("parallel",)),
        name="out_proj",
    )(ya, yb, yc, x2d, mod, lw['w_out'], lw['norm_mlp'])


def _mlp_kernel(h2_ref, x1_ref, mod_ref, w1_ref, w2_ref, o_ref):
    j = pl.program_id(1)

    @pl.when(j == 0)
    def _():
        o_ref[...] = jnp.zeros_like(o_ref)

    f = jnp.maximum(_dot(h2_ref[...], w1_ref[...]), 0.0)
    o_ref[...] += _dot((f * f).astype(BF16), w2_ref[...])

    @pl.when(j == pl.num_programs(1) - 1)
    def _():
        o_ref[...] = x1_ref[...] + mod_ref[5:6, :] * o_ref[...]


def _mlp(h2, x1, mod, lw, latent, tm, tf, layer):
    rows = h2.shape[0]
    return pl.pallas_call(
        _mlp_kernel,
        out_shape=jax.ShapeDtypeStruct((rows, D_MODEL), F32),
        grid=(rows // tm, D_FF // tf),
        in_specs=[
            pl.BlockSpec((tm, D_MODEL), lambda i, j: (i, 0)),
            pl.BlockSpec((tm, D_MODEL), lambda i, j: (i, 0), pipeline_mode=pl.Buffered(1)),
            _mod_spec(latent, tm, layer),
            pl.BlockSpec((None, D_MODEL, tf), lambda i, j: (layer, 0, j)),
            pl.BlockSpec((None, tf, D_MODEL), lambda i, j: (layer, j, 0)),
        ],
        out_specs=pl.BlockSpec((tm, D_MODEL), lambda i, j: (i, 0)),
        compiler_params=_cparams(("parallel", "arbitrary")),
        name="mlp",
    )(h2, x1, mod, lw['w_ff1'], lw['w_ff2'])


def _permute_w_in(w):
    base = OFF_QCN
    ckv0 = base + MLA_HEADS * MLA_QK
    qc = w[:, :, base:ckv0].reshape(DEPTH, D_MODEL, MLA_HEADS, MLA_QK)
    nope = qc[..., :MLA_NOPE].reshape(DEPTH, D_MODEL, MLA_HEADS * MLA_NOPE)
    rope = qc[..., MLA_NOPE:].reshape(DEPTH, D_MODEL, MLA_HEADS * MLA_ROPE)
    kr = w[:, :, ckv0 + MLA_KV_RANK:]
    parts = [w[:, :, :base], nope, rope, w[:, :, ckv0:ckv0 + MLA_KV_RANK], kr, kr]
    return jnp.concatenate(parts, axis=2).astype(BF16)


def _rope_tables(seq):
    t = jnp.arange(seq)
    row = (t // GRID_W).astype(F32)
    col = (t % GRID_W).astype(F32)

    def table(d):
        quarter = d // 4
        inv = ROPE_BASE ** (-(jnp.arange(quarter, dtype=F32) / quarter))
        ar = row[:, None] * inv[None, :]
        ac = col[:, None] * inv[None, :]
        cos = jnp.concatenate([jnp.cos(ar), jnp.cos(ar), jnp.cos(ac), jnp.cos(ac)], axis=-1)
        sin = jnp.concatenate([-jnp.sin(ar), jnp.sin(ar), -jnp.sin(ac), jnp.sin(ac)], axis=-1)
        reps = LANES // d
        return jnp.tile(cos, (1, reps)), jnp.tile(sin, (1, reps))

    cos_b, sin_b = table(GQA_HEAD_DIM)
    cos_c, sin_c = table(MLA_ROPE)
    return {'cos_b': cos_b, 'sin_b': sin_b, 'cos_c': cos_c, 'sin_c': sin_c}


def _ssm_params(lam_re, lam_im, log_dt, b_re, b_im, c_re, c_im):
    a = lam_re.astype(F32)
    w = lam_im.astype(F32)
    dt = jnp.exp(log_dt.astype(F32))[..., None]
    mag = jnp.exp(a * dt)
    lbr = mag * jnp.cos(w * dt)
    lbi = mag * jnp.sin(w * dt)
    den = a * a + w * w
    cr = (((lbr - 1.0) * a + lbi * w) / den)[..., None]
    ci = ((lbi * a - (lbr - 1.0) * w) / den)[..., None]
    bre = b_re.astype(F32)
    bim = b_im.astype(F32)
    bb_re = cr * bre - ci * bim
    bb_im = cr * bim + ci * bre
    lam_ri = jnp.stack([lbr.reshape(DEPTH, 2, SSM_N), lbi.reshape(DEPTH, 2, SSM_N)], axis=2)
    lam_b = jnp.broadcast_to(lam_ri[:, :, :, None, :], (DEPTH, 2, 2, SUBLANES, SSM_N))
    gh = SSM_GROUPS // 2
    eye = jnp.eye(gh, dtype=F32)[:, None, :, None]

    def blockdiag(x, rows_per_g, cols_per_g):
        x = x.reshape(DEPTH, 2, 2, gh, rows_per_g, 1, cols_per_g) * eye
        return x.reshape(DEPTH, 2, 2, gh * rows_per_g, gh * cols_per_g)

    def bmat(x):
        return blockdiag(jnp.swapaxes(x, -1, -2), SSM_GROUP, SSM_STATE)

    def cmat(x):
        return blockdiag(jnp.swapaxes(x, -1, -2), SSM_STATE, SSM_GROUP)

    b_cat = jnp.concatenate([bmat(bb_re), bmat(bb_im)], axis=-1).astype(BF16)
    return {'lam': lam_b, 'b': b_cat, 'c_re': cmat(c_re.astype(F32)).astype(BF16),
            'c_im': cmat(c_im.astype(F32)).astype(BF16)}


def _trunk_layer(x2d, mod, lw, sp, tabs, latent, ctx, layer):
    seq = DEC_SEQ if latent else SEQ
    bsz = x2d.shape[0] // seq
    u, qb, kb, vb, qc, ckv, kr, krg = _in_proj(x2d, mod, latent, lw, tabs, TM_PROJ, layer)

    u3 = u.reshape(bsz, seq, SSM_CH)
    y_tm, h_t = _ssm_scan(u3, ctx['h0'], sp, layer)
    ya = _ssm_glu(u3, y_tm, lw['ssm_d'], lw['ssm_w_glu'], layer).reshape(bsz * seq, SSM_CH)

    if latent:
        yb = _gqa_attn(qb, kb, vb, True, ctx['k'], ctx['v'], layer)
        yc = _mla_attn(qc, ckv, kr, krg, lw, True, ctx['ckv'], ctx['kr'], layer)
    else:
        yb = _gqa_attn(qb, kb, vb, False)
        yc = _mla_attn(qc, ckv, kr, krg, lw, False, layer=layer)

    x1, h2 = _out_proj(ya, yb, yc, x2d, mod, lw, latent, TM_PROJ, layer)
    x2 = _mlp(h2, x1, mod, lw, latent, TM_MLP, TF_MLP, layer)
    return x2, (kb, vb, ckv, kr, h_t)


def kernel(x_prompt, x_sample, cache_attn_k, cache_attn_v, cache_mla_ckv, cache_mla_krope, state_ssm, c, c_ctx, w_mod, b_mod, norm_mix, norm_mlp, w_in, gqa_q_norm, gqa_k_norm, mla_kv_norm, mla_q_norm, mla_k_norm, mla_w_uk, mla_w_uv, ssm_lam_re, ssm_lam_im, ssm_log_dt, ssm_b_re, ssm_b_im, ssm_c_re, ssm_c_im, ssm_d, ssm_w_glu, w_out, w_ff1, w_ff2):
    cvec = jnp.zeros((N_MOD, D_MODEL), F32).at[0].set(c_ctx).at[1:1 + DEC_BATCH].set(c)
    mod = _adaln(cvec, w_mod, b_mod.reshape(DEPTH, 1, 6 * D_MODEL)).reshape(DEPTH, N_MOD, 6, D_MODEL)

    tabs = _rope_tables(DEC_SEQ)
    cache_k = cache_attn_k.reshape(DEC_BATCH, DEPTH, PAST_LEN, GQA_KV_HEADS * GQA_HEAD_DIM)
    cache_v = cache_attn_v.reshape(DEC_BATCH, DEPTH, PAST_LEN, GQA_KV_HEADS * GQA_HEAD_DIM)

    xp = x_prompt.reshape(BATCH * SEQ, D_MODEL)
    xs = x_sample.reshape(DEC_BATCH * DEC_SEQ, D_MODEL)
    new_k, new_v, new_ckv, new_kr, new_ssm = [], [], [], [], []
    h0_zero = jnp.zeros((BATCH // SUBLANES, 2, 2, SUBLANES, SSM_N), F32)

    row3 = lambda a: a.reshape(DEPTH, 1, -1)
    dup = lambda a: jnp.concatenate([a, a], axis=-1)
    lw = {
        'norm_mix': row3(norm_mix), 'norm_mlp': row3(norm_mlp),
        'w_in': _permute_w_in(w_in),
        'w_out': w_out.astype(BF16), 'w_ff1': w_ff1.astype(BF16), 'w_ff2': w_ff2.astype(BF16),
        'gqa_q_norm': row3(gqa_q_norm), 'gqa_k_norm': row3(gqa_k_norm), 'mla_kv_norm': row3(mla_kv_norm),
        'mla_q_nope_g': row3(mla_q_norm[:, :MLA_NOPE]),
        'mla_q_rope_g': row3(dup(mla_q_norm[:, MLA_NOPE:])),
        'mla_k_nope_g': row3(mla_k_norm[:, :MLA_NOPE]),
        'mla_k_rope_g': row3(dup(mla_k_norm[:, MLA_NOPE:])),
        'mla_k_rope_g64': row3(mla_k_norm[:, MLA_NOPE:]),
        'mla_w_uk': mla_w_uk.reshape(DEPTH, MLA_KV_RANK, MLA_HEADS * MLA_NOPE).astype(BF16),
        'mla_w_uv': mla_w_uv.reshape(DEPTH, MLA_KV_RANK, MLA_HEADS * MLA_V).astype(BF16),
        'ssm_d': row3(ssm_d), 'ssm_w_glu': ssm_w_glu.astype(BF16),
    }
    sp = _ssm_params(ssm_lam_re, ssm_lam_im, ssm_log_dt, ssm_b_re, ssm_b_im, ssm_c_re, ssm_c_im)
    h0_lat = state_ssm.reshape(DEC_BATCH, DEPTH, 2, SSM_N, 2).transpose(1, 2, 4, 0, 3)[:, None]

    for l in range(DEPTH):
        xp, (k, v, ckv_n, kr, h_t) = _trunk_layer(xp, mod, lw, sp, tabs, False, {'h0': h0_zero}, l)
        new_k.append(k.reshape(BATCH, SEQ, GQA_KV_HEADS, GQA_HEAD_DIM))
        new_v.append(v.reshape(BATCH, SEQ, GQA_KV_HEADS, GQA_HEAD_DIM))
        new_ckv.append(ckv_n.reshape(BATCH, SEQ, MLA_KV_RANK))
        new_kr.append(kr.reshape(BATCH, SEQ, MLA_ROPE))
        hs = h_t.transpose(0, 3, 1, 4, 2).reshape(BATCH, 2, SSM_GROUPS, SSM_STATE, 2)
        new_ssm.append(hs)

        ctx = {'k': cache_k, 'v': cache_v, 'ckv': cache_mla_ckv, 'kr': cache_mla_krope, 'h0': h0_lat[l]}
        xs, _ = _trunk_layer(xs, mod, lw, sp, tabs, True, ctx, l)

    return (xp.reshape(BATCH, SEQ, D_MODEL), xs.reshape(DEC_BATCH, DEC_SEQ, D_MODEL),
            jnp.stack(new_k, axis=1), jnp.stack(new_v, axis=1), jnp.stack(new_ckv, axis=1),
            jnp.stack(new_kr, axis=1), jnp.stack(new_ssm, axis=1))
```

```python
import functools
import math

import numpy as np
import jax
import jax.numpy as jnp
from jax import lax
from jax.experimental import pallas as pl
from jax.experimental.pallas import tpu as pltpu

D_MODEL = 2048
BATCH = 32
SEQ = 256
DEPTH = 2
DEC_BATCH = 8
DEC_SEQ = 1024
PAST_LEN = 512
GRID_W = 64
ROPE_BASE = 10000.0
EPS = 1e-6
SSM_CH = 512
SSM_GROUP = 16
SSM_GROUPS = SSM_CH // SSM_GROUP
SSM_STATE = 64
SSM_N = SSM_GROUPS * SSM_STATE
GQA_HEADS = 6
GQA_KV_HEADS = 2
GQA_GROUP = GQA_HEADS // GQA_KV_HEADS
GQA_HEAD_DIM = 128
MLA_HEADS = 6
MLA_NOPE = 128
MLA_ROPE = 64
MLA_QK = MLA_NOPE + MLA_ROPE
MLA_QK_PAD = 256
MLA_V = 128
MLA_KV_RANK = 512
D_FF = 4 * D_MODEL
N_MOD = 16

OFF_U = 0
OFF_QB = OFF_U + SSM_CH
OFF_KB = OFF_QB + GQA_HEADS * GQA_HEAD_DIM
OFF_VB = OFF_KB + GQA_KV_HEADS * GQA_HEAD_DIM
OFF_QCN = OFF_VB + GQA_KV_HEADS * GQA_HEAD_DIM
OFF_QCR = OFF_QCN + MLA_HEADS * MLA_NOPE
OFF_CKV = OFF_QCR + MLA_HEADS * MLA_ROPE
OFF_KR = OFF_CKV + MLA_KV_RANK
IN_WIDTH_P = OFF_KR + 2 * MLA_ROPE

LANES = 128
SUBLANES = 8
VMEM_LIMIT = 56 * 1024 * 1024

TM_PROJ = 512
TM_MLP = 1024
TF_MLP = 512

BF16 = jnp.bfloat16
F32 = jnp.float32


def _cparams(sem):
    return pltpu.CompilerParams(dimension_semantics=sem, vmem_limit_bytes=VMEM_LIMIT)


def _dot(a, b):
    return jnp.dot(a, b, preferred_element_type=F32)


def _dot_nt(a, b):
    return lax.dot_general(a, b, (((1,), (1,)), ((), ())), preferred_element_type=F32)


def _adaln_kernel(c_ref, w_ref, b_ref, o_ref):
    c = c_ref[...]
    s = (c * jax.nn.sigmoid(c)).astype(BF16)
    o_ref[...] = _dot(s, w_ref[...].astype(BF16)) + b_ref[...]


def _adaln(cvec, w_mod, b_mod):
    tn = 1024
    return pl.pallas_call(
        _adaln_kernel,
        out_shape=jax.ShapeDtypeStruct((DEPTH, N_MOD, 6 * D_MODEL), F32),
        grid=(DEPTH, 6 * D_MODEL // tn),
        in_specs=[
            pl.BlockSpec((N_MOD, D_MODEL), lambda l, j: (0, 0)),
            pl.BlockSpec((None, D_MODEL, tn), lambda l, j: (l, 0, j)),
            pl.BlockSpec((None, 1, tn), lambda l, j: (l, 0, j)),
        ],
        out_specs=pl.BlockSpec((None, N_MOD, tn), lambda l, j: (l, 0, j)),
        compiler_params=_cparams(("parallel", "parallel")),
        name="adaln",
    )(cvec, w_mod, b_mod)


def _swap_halves(x, block):
    lane = lax.broadcasted_iota(jnp.int32, x.shape, 1)
    first = (lane % (2 * block)) < block
    return jnp.where(first, pltpu.roll(x, LANES - block, 1), pltpu.roll(x, block, 1))


def _rope(x, cos, sin_signed, block):
    return x * cos + _swap_halves(x, block) * sin_signed


def _in_proj_kernel(latent, x_ref, mod_ref, gmix_ref, w_ref, gq_ref, gk_ref, gkv_ref, gqn_ref, gqr_ref, gkr_ref,
                    cosb_ref, sinb_ref, cosc_ref, sinc_ref,
                    u_ref, qb_ref, kb_ref, vb_ref, qc_ref, ckv_ref, kr_ref, krg_ref):
    x = x_ref[...]
    ms = jnp.mean(x * x, axis=-1, keepdims=True)
    y = x * lax.rsqrt(ms + EPS) * gmix_ref[...]
    h = (y * (1.0 + mod_ref[1:2, :]) + mod_ref[0:1, :]).astype(BF16)

    def proj(off, width):
        return _dot(h, w_ref[:, off:off + width])

    u_ref[...] = proj(OFF_U, SSM_CH)

    scale_b = 1.0 / math.sqrt(GQA_HEAD_DIM)
    zq = proj(OFF_QB, GQA_HEADS * GQA_HEAD_DIM)
    for hd in range(GQA_HEADS):
        col = zq[:, hd * LANES:(hd + 1) * LANES]
        q = col * lax.rsqrt(jnp.mean(col * col, axis=-1, keepdims=True) + EPS) * gq_ref[...]
        if latent:
            q = _rope(q, cosb_ref[...], sinb_ref[...], GQA_HEAD_DIM // 4)
        qb_ref[:, hd * LANES:(hd + 1) * LANES] = (q * scale_b).astype(qb_ref.dtype)
    zk = proj(OFF_KB, GQA_KV_HEADS * GQA_HEAD_DIM)
    for hd in range(GQA_KV_HEADS):
        col = zk[:, hd * LANES:(hd + 1) * LANES]
        k = col * lax.rsqrt(jnp.mean(col * col, axis=-1, keepdims=True) + EPS) * gk_ref[...]
        if latent:
            k = _rope(k, cosb_ref[...], sinb_ref[...], GQA_HEAD_DIM // 4)
        kb_ref[:, hd * LANES:(hd + 1) * LANES] = k.astype(kb_ref.dtype)
    vb_ref[...] = proj(OFF_VB, GQA_KV_HEADS * GQA_HEAD_DIM).astype(vb_ref.dtype)

    scale_c = 1.0 / math.sqrt(MLA_QK)
    zn = proj(OFF_QCN, MLA_HEADS * MLA_NOPE)
    zr = proj(OFF_QCR, MLA_HEADS * MLA_ROPE)
    lane = lax.broadcasted_iota(jnp.int32, (x.shape[0], LANES), 1)
    low = lane < MLA_ROPE
    for pair in range(MLA_HEADS // 2):
        colr = zr[:, pair * LANES:(pair + 1) * LANES]
        sq = colr * colr
        ss_lo = jnp.sum(jnp.where(low, sq, 0.0), axis=-1, keepdims=True)
        ss_hi = jnp.sum(jnp.where(low, 0.0, sq), axis=-1, keepdims=True)
        rs = []
        for half, ss_r in ((0, ss_lo), (1, ss_hi)):
            hd = 2 * pair + half
            coln = zn[:, hd * LANES:(hd + 1) * LANES]
            ss = jnp.sum(coln * coln, axis=-1, keepdims=True) + ss_r
            r = lax.rsqrt(ss * (1.0 / MLA_QK) + EPS)
            rs.append(r)
            qc_ref[hd, :, :MLA_NOPE] = (coln * r * gqn_ref[...] * scale_c).astype(qc_ref.dtype)
        qr = colr * jnp.where(low, rs[0], rs[1]) * gqr_ref[...]
        if latent:
            qr = _rope(qr, cosc_ref[...], sinc_ref[...], MLA_ROPE // 4)
        qr = qr * scale_c
        zeros = jnp.zeros((x.shape[0], MLA_QK_PAD - MLA_QK), qc_ref.dtype)
        for half in range(2):
            hd = 2 * pair + half
            qc_ref[hd, :, MLA_NOPE:MLA_QK] = qr[:, half * MLA_ROPE:(half + 1) * MLA_ROPE].astype(qc_ref.dtype)
            qc_ref[hd, :, MLA_QK:] = zeros

    zc = proj(OFF_CKV, MLA_KV_RANK)
    ckv_ref[...] = (zc * lax.rsqrt(jnp.mean(zc * zc, axis=-1, keepdims=True) + EPS) * gkv_ref[...]).astype(ckv_ref.dtype)
    zkr = proj(OFF_KR, 2 * MLA_ROPE)
    kr_ref[...] = zkr[:, :MLA_ROPE]
    krg = zkr * gkr_ref[...]
    if latent:
        krg = _rope(krg, cosc_ref[...], sinc_ref[...], MLA_ROPE // 4)
    krg_ref[...] = krg[:, :MLA_ROPE]


def _lspec(block_tail, layer):
    zeros = (0,) * len(block_tail)
    return pl.BlockSpec((None,) + tuple(block_tail), lambda *_: (layer,) + zeros)


def _mod_spec(latent, tm, layer):
    tiles_per_seq = DEC_SEQ // tm
    if latent:
        return pl.BlockSpec((None, None, 6, D_MODEL), lambda i, *_: (layer, 1 + i // tiles_per_seq, 0, 0))
    return pl.BlockSpec((None, None, 6, D_MODEL), lambda i, *_: (layer, 0, 0, 0))


def _in_proj(x2d, mod, latent, lw, tabs, tm, layer):
    rows = x2d.shape[0]
    n_tiles = rows // tm
    tiles_per_seq = DEC_SEQ // tm
    if latent:
        tab_map = lambda i: (i % tiles_per_seq, 0)
    else:
        tab_map = lambda i: (0, 0)
    row = lambda i: (i, 0)
    act_dt = BF16 if latent else F32
    in_specs = [
        pl.BlockSpec((tm, D_MODEL), row),
        _mod_spec(latent, tm, layer),
        _lspec((1, D_MODEL), layer),
        _lspec((D_MODEL, IN_WIDTH_P), layer),
        _lspec((1, LANES), layer),
        _lspec((1, LANES), layer),
        _lspec((1, MLA_KV_RANK), layer),
        _lspec((1, LANES), layer),
        _lspec((1, LANES), layer),
        _lspec((1, LANES), layer),
        pl.BlockSpec((tm, LANES), tab_map),
        pl.BlockSpec((tm, LANES), tab_map),
        pl.BlockSpec((tm, LANES), tab_map),
        pl.BlockSpec((tm, LANES), tab_map),
    ]
    out_shape = (
        jax.ShapeDtypeStruct((rows, SSM_CH), F32),
        jax.ShapeDtypeStruct((rows, GQA_HEADS * GQA_HEAD_DIM), BF16),
        jax.ShapeDtypeStruct((rows, GQA_KV_HEADS * GQA_HEAD_DIM), act_dt),
        jax.ShapeDtypeStruct((rows, GQA_KV_HEADS * GQA_HEAD_DIM), act_dt),
        jax.ShapeDtypeStruct((MLA_HEADS, rows, MLA_QK_PAD), BF16),
        jax.ShapeDtypeStruct((rows, MLA_KV_RANK), act_dt),
        jax.ShapeDtypeStruct((rows, MLA_ROPE), F32),
        jax.ShapeDtypeStruct((rows, MLA_ROPE), F32),
    )
    out_specs = (
        pl.BlockSpec((tm, SSM_CH), row),
        pl.BlockSpec((tm, GQA_HEADS * GQA_HEAD_DIM), row),
        pl.BlockSpec((tm, GQA_KV_HEADS * GQA_HEAD_DIM), row),
        pl.BlockSpec((tm, GQA_KV_HEADS * GQA_HEAD_DIM), row),
        pl.BlockSpec((MLA_HEADS, tm, MLA_QK_PAD), lambda i: (0, i, 0)),
        pl.BlockSpec((tm, MLA_KV_RANK), row),
        pl.BlockSpec((tm, MLA_ROPE), row),
        pl.BlockSpec((tm, MLA_ROPE), row),
    )
    return pl.pallas_call(
        functools.partial(_in_proj_kernel, latent),
        out_shape=out_shape,
        grid=(n_tiles,),
        in_specs=in_specs,
        out_specs=out_specs,
        compiler_params=_cparams(("parallel",)),
        name="in_proj_lat" if latent else "in_proj_ctx",
    )(x2d, mod, lw['norm_mix'], lw['w_in'], lw['gqa_q_norm'], lw['gqa_k_norm'], lw['mla_kv_norm'],
      lw['mla_q_nope_g'], lw['mla_q_rope_g'], lw['mla_k_rope_g'],
      tabs['cos_b'], tabs['sin_b'], tabs['cos_c'], tabs['sin_c'])


SSM_TC = 128
SSM_LANE_SPLIT = 2
SSM_UNROLL = 8


def _ssm_scan_kernel(n_chunks, u_ref, h0_ref, lam_ref, b_ref, cre_ref, cim_ref, y_ref, hT_ref,
                     utm_ref, sre_ref, sim_ref, h_ref):
    d = pl.program_id(1)
    i = pl.program_id(2)
    tc = SSM_TC

    @pl.when(i == 0)
    def _():
        h_ref[...] = h0_ref[...]

    for b in range(SUBLANES):
        for j in range(SSM_CH // LANES):
            utm_ref[j, pl.ds(b, tc, stride=SUBLANES), :] = u_ref[b, :, j * LANES:(j + 1) * LANES]

    half_n = SSM_N // 2
    half_c = SSM_CH // 2
    for k in range(2):
        uk = jnp.concatenate([utm_ref[2 * k], utm_ref[2 * k + 1]], axis=-1).astype(BF16)
        r = _dot(uk, b_ref[k])
        sre_ref[:, k * half_n:(k + 1) * half_n] = r[:, :half_n]
        sim_ref[:, k * half_n:(k + 1) * half_n] = r[:, half_n:]

    piece = SSM_N // SSM_LANE_SPLIT
    for p in range(SSM_LANE_SPLIT):
        sl = slice(p * piece, (p + 1) * piece)
        lr = lam_ref[0, :, sl]
        li = lam_ref[1, :, sl]

        def step(s, carry):
            hr, hi = carry
            t = s + d * (tc - 1 - 2 * s)
            rowi = pl.multiple_of(t * SUBLANES, SUBLANES)
            nr = lr * hr - li * hi + sre_ref[pl.ds(rowi, SUBLANES), sl]
            ni = lr * hi + li * hr + sim_ref[pl.ds(rowi, SUBLANES), sl]
            sre_ref[pl.ds(rowi, SUBLANES), sl] = nr
            sim_ref[pl.ds(rowi, SUBLANES), sl] = ni
            return nr, ni

        hr, hi = lax.fori_loop(0, tc, step, (h_ref[0, :, sl], h_ref[1, :, sl]), unroll=SSM_UNROLL)
        h_ref[0, :, sl] = hr
        h_ref[1, :, sl] = hi

    for k in range(2):
        yk = (_dot(sre_ref[:, k * half_n:(k + 1) * half_n].astype(BF16), cre_ref[k])
              - _dot(sim_ref[:, k * half_n:(k + 1) * half_n].astype(BF16), cim_ref[k]))
        y_ref[2 * k] = yk[:, :LANES]
        y_ref[2 * k + 1] = yk[:, LANES:]

    @pl.when(i == n_chunks - 1)
    def _():
        hT_ref[...] = h_ref[...]


def _ssm_scan(u3, h0, sp, layer):
    bsz, t, _ = u3.shape
    n_groups = bsz // SUBLANES
    n_chunks = t // SSM_TC
    tc = SSM_TC

    def chunk(d, i):
        return i + d * (n_chunks - 1 - 2 * i)

    return pl.pallas_call(
        functools.partial(_ssm_scan_kernel, n_chunks),
        out_shape=(
            jax.ShapeDtypeStruct((2, n_groups, SSM_CH // LANES, t * SUBLANES, LANES), F32),
            jax.ShapeDtypeStruct((n_groups, 2, 2, SUBLANES, SSM_N), F32),
        ),
        grid=(n_groups, 2, n_chunks),
        in_specs=[
            pl.BlockSpec((SUBLANES, tc, SSM_CH), lambda g, d, i: (g, chunk(d, i), 0)),
            pl.BlockSpec((None, None, 2, SUBLANES, SSM_N), lambda g, d, i: (g, d, 0, 0, 0)),
            pl.BlockSpec((None, None, 2, SUBLANES, SSM_N), lambda g, d, i: (layer, d, 0, 0, 0)),
            pl.BlockSpec((None, None, 2, SSM_CH // 2, SSM_N), lambda g, d, i: (layer, d, 0, 0, 0)),
            pl.BlockSpec((None, None, 2, SSM_N // 2, SSM_CH // 2), lambda g, d, i: (layer, d, 0, 0, 0)),
            pl.BlockSpec((None, None, 2, SSM_N // 2, SSM_CH // 2), lambda g, d, i: (layer, d, 0, 0, 0)),
        ],
        out_specs=(
            pl.BlockSpec((None, None, SSM_CH // LANES, tc * SUBLANES, LANES),
                         lambda g, d, i: (d, g, 0, chunk(d, i), 0)),
            pl.BlockSpec((None, None, 2, SUBLANES, SSM_N), lambda g, d, i: (g, d, 0, 0, 0)),
        ),
        scratch_shapes=[
            pltpu.VMEM((SSM_CH // LANES, tc * SUBLANES, LANES), F32),
            pltpu.VMEM((tc * SUBLANES, SSM_N), F32),
            pltpu.VMEM((tc * SUBLANES, SSM_N), F32),
            pltpu.VMEM((2, SUBLANES, SSM_N), F32),
        ],
        compiler_params=_cparams(("parallel", "parallel", "arbitrary")),
        name="ssm_scan",
    )(u3, h0, sp['lam'], sp['b'], sp['c_re'], sp['c_im'])


def _ssm_glu_kernel(u_ref, y_ref, d_ref, w_ref, o_ref, ycat_ref):
    tc = SSM_TC
    for b in range(SUBLANES):
        for j in range(SSM_CH // LANES):
            sl = slice(j * LANES, (j + 1) * LANES)
            rows_b = pl.ds(b, tc, stride=SUBLANES)
            ycat_ref[b * tc:(b + 1) * tc, sl] = (y_ref[0, j, rows_b, :] + y_ref[1, j, rows_b, :]
                                                + d_ref[:, sl] * u_ref[b, :, sl])
    zg = _dot(ycat_ref[...].astype(BF16), w_ref[...])
    out = zg[:, :SSM_CH] * jax.nn.sigmoid(zg[:, SSM_CH:])
    o_ref[...] = out.reshape(SUBLANES, tc, SSM_CH).astype(o_ref.dtype)


def _ssm_glu(u3, y, d_skip, w_glu, layer):
    bsz, t, _ = u3.shape
    n_groups = bsz // SUBLANES
    n_chunks = t // SSM_TC
    tc = SSM_TC
    return pl.pallas_call(
        _ssm_glu_kernel,
        out_shape=jax.ShapeDtypeStruct((bsz, t, SSM_CH), BF16),
        grid=(n_groups, n_chunks),
        in_specs=[
            pl.BlockSpec((SUBLANES, tc, SSM_CH), lambda g, i: (g, i, 0)),
            pl.BlockSpec((2, None, SSM_CH // LANES, tc * SUBLANES, LANES), lambda g, i: (0, g, 0, i, 0)),
            _lspec((1, SSM_CH), layer),
            _lspec((SSM_CH, 2 * SSM_CH), layer),
        ],
        out_specs=pl.BlockSpec((SUBLANES, tc, SSM_CH), lambda g, i: (g, i, 0)),
        scratch_shapes=[pltpu.VMEM((SUBLANES * tc, SSM_CH), F32)],
        compiler_params=_cparams(("parallel", "parallel")),
        name="ssm_glu",
    )(u3, y, d_skip, w_glu)


def _ones_column(n):
    return jnp.ones((n, LANES), BF16)


def _softmax_pv(scores, v_ext):
    m = jnp.max(scores, axis=-1, keepdims=True)
    p = jnp.exp((scores - m).astype(BF16))
    o = _dot(p, v_ext)
    return o[:, :LANES] / o[:, LANES:]


ATTN_TQ = 256


def _for_each_q_block(nb, seq, body):
    nq = seq // ATTN_TQ
    for s in range(nb):
        if nq == 1:
            body(s, s * seq)
        else:
            def it(qi, carry, s=s):
                body(s, pl.multiple_of(s * seq + qi * ATTN_TQ, ATTN_TQ))
                return carry
            lax.fori_loop(0, nq, it, 0)


def _gqa_kernel(latent, nb, seq, *refs):
    if latent:
        q_ref, k_ref, v_ref, kc_ref, vc_ref, o_ref, k_s, v_s = refs
    else:
        q_ref, k_ref, v_ref, o_ref, k_s, v_s = refs
    tq = ATTN_TQ
    for s in range(nb):
        for h in range(GQA_KV_HEADS):
            hl = slice(h * LANES, (h + 1) * LANES)
            k_s[s, h, 0:seq, :] = k_ref[s * seq:(s + 1) * seq, hl].astype(BF16)
            v_s[s, h, 0:seq, :LANES] = v_ref[s * seq:(s + 1) * seq, hl].astype(BF16)
            if latent:
                k_s[s, h, seq:, :] = kc_ref[:, hl].astype(BF16)
                v_s[s, h, seq:, :LANES] = vc_ref[:, hl].astype(BF16)
            v_s[s, h, :, LANES:] = _ones_column(v_s.shape[2])

    def body(s, r0):
        for h in range(GQA_KV_HEADS):
            heads = [h * GQA_GROUP + g for g in range(GQA_GROUP)]
            q3 = jnp.concatenate([q_ref[pl.ds(r0, tq), hd * LANES:(hd + 1) * LANES] for hd in heads], axis=0)
            o = _softmax_pv(_dot_nt(q3, k_s[s, h]), v_s[s, h])
            for g, hd in enumerate(heads):
                o_ref[pl.ds(r0, tq), hd * LANES:(hd + 1) * LANES] = o[g * tq:(g + 1) * tq].astype(o_ref.dtype)

    _for_each_q_block(nb, seq, body)


def _gqa_attn(qb, kb, vb, latent, cache_k=None, cache_v=None, layer=0):
    rows = qb.shape[0]
    seq = DEC_SEQ if latent else SEQ
    nb = 1 if latent else 4
    t_all = seq + (PAST_LEN if latent else 0)
    qw = GQA_HEADS * GQA_HEAD_DIM
    kw = GQA_KV_HEADS * GQA_HEAD_DIM
    row = lambda i: (i, 0)
    in_specs = [
        pl.BlockSpec((nb * seq, qw), row),
        pl.BlockSpec((nb * seq, kw), row),
        pl.BlockSpec((nb * seq, kw), row),
    ]
    args = [qb, kb, vb]
    if latent:
        cspec = pl.BlockSpec((None, None, PAST_LEN, kw), lambda i: (i, layer, 0, 0))
        in_specs += [cspec, cspec]
        args += [cache_k, cache_v]
    return pl.pallas_call(
        functools.partial(_gqa_kernel, latent, nb, seq),
        out_shape=jax.ShapeDtypeStruct((rows, qw), BF16),
        grid=(rows // (nb * seq),),
        in_specs=in_specs,
        out_specs=pl.BlockSpec((nb * seq, qw), row),
        scratch_shapes=[
            pltpu.VMEM((nb, GQA_KV_HEADS, t_all, GQA_HEAD_DIM), BF16),
            pltpu.VMEM((nb, GQA_KV_HEADS, t_all, 2 * LANES), BF16),
        ],
        compiler_params=_cparams(("parallel",)),
        name="gqa_lat" if latent else "gqa_ctx",
    )(*args)


def _mla_kernel(latent, nb, seq, *refs):
    if latent:
        (q_ref, ckv_ref, kr_ref, krg_ref, ckvc_ref, krc_ref, wuk_ref, wuv_ref, gn_ref, gr_ref,
         o_ref, k_s, v_s) = refs
    else:
        (q_ref, ckv_ref, kr_ref, krg_ref, wuk_ref, wuv_ref, gn_ref, gr_ref, o_ref, k_s, v_s) = refs
    tq = ATTN_TQ

    def expand(ckv, kr_raw, krg, place):
        c = ckv.astype(BF16)
        kn_all = _dot(c, wuk_ref[...])
        v_all = _dot(c, wuv_ref[...])
        ss_kr = jnp.sum(kr_raw * kr_raw, axis=-1, keepdims=True)
        for h in range(MLA_HEADS):
            hl = slice(h * LANES, (h + 1) * LANES)
            kn = kn_all[:, hl]
            rs = lax.rsqrt((jnp.sum(kn * kn, axis=-1, keepdims=True) + ss_kr) * (1.0 / MLA_QK) + EPS)
            kn = (kn * rs * gn_ref[...]).astype(BF16)
            kp = (krg * rs).astype(BF16)
            vv = v_all[:, hl].astype(BF16)
            for s, lo, n, rows in place:
                k_s[s, h, lo:lo + n, :MLA_NOPE] = kn[rows]
                k_s[s, h, lo:lo + n, MLA_NOPE:MLA_QK] = kp[rows]
                k_s[s, h, lo:lo + n, MLA_QK:] = jnp.zeros((n, MLA_QK_PAD - MLA_QK), BF16)
                v_s[s, h, lo:lo + n, :MLA_V] = vv[rows]
                v_s[s, h, lo:lo + n, MLA_V:] = _ones_column(n)

    expand(ckv_ref[...], kr_ref[...], krg_ref[...],
           [(s, 0, seq, slice(s * seq, (s + 1) * seq)) for s in range(nb)])
    if latent:
        krc = krc_ref[...]
        expand(ckvc_ref[...], krc, krc * gr_ref[...], [(0, seq, PAST_LEN, slice(0, PAST_LEN))])

    def body(s, r0):
        for h in range(MLA_HEADS):
            sc = _dot_nt(q_ref[h, pl.ds(r0, tq), :], k_s[s, h])
            o_ref[pl.ds(r0, tq), h * LANES:(h + 1) * LANES] = _softmax_pv(sc, v_s[s, h]).astype(o_ref.dtype)

    _for_each_q_block(nb, seq, body)


def _mla_attn(qc, ckv, kr, krg, lw, latent, cache_ckv=None, cache_kr=None, layer=0):
    rows = qc.shape[1]
    seq = DEC_SEQ if latent else SEQ
    nb = 1 if latent else 4
    t_all = seq + (PAST_LEN if latent else 0)
    row = lambda i: (i, 0)
    full = lambda i: (0, 0)
    in_specs = [
        pl.BlockSpec((MLA_HEADS, nb * seq, MLA_QK_PAD), lambda i: (0, i, 0)),
        pl.BlockSpec((nb * seq, MLA_KV_RANK), row),
        pl.BlockSpec((nb * seq, MLA_ROPE), row),
        pl.BlockSpec((nb * seq, MLA_ROPE), row),
    ]
    args = [qc, ckv, kr, krg]
    if latent:
        in_specs += [
            pl.BlockSpec((None, None, PAST_LEN, MLA_KV_RANK), lambda i: (i, layer, 0, 0)),
            pl.BlockSpec((None, None, PAST_LEN, MLA_ROPE), lambda i: (i, layer, 0, 0)),
        ]
        args += [cache_ckv, cache_kr]
    in_specs += [
        _lspec((MLA_KV_RANK, MLA_HEADS * MLA_NOPE), layer),
        _lspec((MLA_KV_RANK, MLA_HEADS * MLA_V), layer),
        _lspec((1, MLA_NOPE), layer),
        _lspec((1, MLA_ROPE), layer),
    ]
    args += [lw['mla_w_uk'], lw['mla_w_uv'], lw['mla_k_nope_g'], lw['mla_k_rope_g64']]
    return pl.pallas_call(
        functools.partial(_mla_kernel, latent, nb, seq),
        out_shape=jax.ShapeDtypeStruct((rows, MLA_HEADS * MLA_V), BF16),
        grid=(rows // (nb * seq),),
        in_specs=in_specs,
        out_specs=pl.BlockSpec((nb * seq, MLA_HEADS * MLA_V), row),
        scratch_shapes=[
            pltpu.VMEM((nb, MLA_HEADS, t_all, MLA_QK_PAD), BF16),
            pltpu.VMEM((nb, MLA_HEADS, t_all, MLA_V + LANES), BF16),
        ],
        compiler_params=_cparams(("parallel",)),
        name="mla_lat" if latent else "mla_ctx",
    )(*args)


def _out_proj_kernel(ya_ref, yb_ref, yc_ref, x_ref, mod_ref, w_ref, g_ref, x1_ref, h2_ref):
    wa = SSM_CH
    wb = wa + GQA_HEADS * GQA_HEAD_DIM
    o = (_dot(ya_ref[...], w_ref[0:wa, :]) + _dot(yb_ref[...], w_ref[wa:wb, :])
         + _dot(yc_ref[...], w_ref[wb:, :]))
    x1 = x_ref[...] + mod_ref[2:3, :] * o
    x1_ref[...] = x1
    ms = jnp.mean(x1 * x1, axis=-1, keepdims=True)
    y = x1 * lax.rsqrt(ms + EPS) * g_ref[...]
    h2_ref[...] = (y * (1.0 + mod_ref[4:5, :]) + mod_ref[3:4, :]).astype(h2_ref.dtype)


def _out_proj(ya, yb, yc, x2d, mod, lw, latent, tm, layer):
    rows = x2d.shape[0]
    row = lambda i: (i, 0)
    return pl.pallas_call(
        _out_proj_kernel,
        out_shape=(jax.ShapeDtypeStruct((rows, D_MODEL), F32), jax.ShapeDtypeStruct((rows, D_MODEL), BF16)),
        grid=(rows // tm,),
        in_specs=[
            pl.BlockSpec((tm, SSM_CH), row),
            pl.BlockSpec((tm, GQA_HEADS * GQA_HEAD_DIM), row),
            pl.BlockSpec((tm, MLA_HEADS * MLA_V), row),
            pl.BlockSpec((tm, D_MODEL), row),
            _mod_spec(latent, tm, layer),
            _lspec((D_MODEL, D_MODEL), layer),
            _lspec((1, D_MODEL), layer),
        ],
        out_specs=(pl.BlockSpec((tm, D_MODEL), row), pl.BlockSpec((tm, D_MODEL), row)),
        compiler_params=_cparams(("parallel",)),
        name="out_proj",
    )(ya, yb, yc, x2d, mod, lw['w_out'], lw['norm_mlp'])


def _mlp_kernel(h2_ref, x1_ref, mod_ref, w1_ref, w2_ref, o_ref):
    j = pl.program_id(1)

    @pl.when(j == 0)
    def _():
        o_ref[...] = jnp.zeros_like(o_ref)

    f = jnp.maximum(_dot(h2_ref[...], w1_ref[...]), 0.0)
    o_ref[...] += _dot((f * f).astype(BF16), w2_ref[...])

    @pl.when(j == pl.num_programs(1) - 1)
    def _():
        o_ref[...] = x1_ref[...] + mod_ref[5:6, :] * o_ref[...]


def _mlp(h2, x1, mod, lw, latent, tm, tf, layer):
    rows = h2.shape[0]
    return pl.pallas_call(
        _mlp_kernel,
        out_shape=jax.ShapeDtypeStruct((rows, D_MODEL), F32),
        grid=(rows // tm, D_FF // tf),
        in_specs=[
            pl.BlockSpec((tm, D_MODEL), lambda i, j: (i, 0)),
            pl.BlockSpec((tm, D_MODEL), lambda i, j: (i, 0), pipeline_mode=pl.Buffered(1)),
            _mod_spec(latent, tm, layer),
            pl.BlockSpec((None, D_MODEL, tf), lambda i, j: (layer, 0, j)),
            pl.BlockSpec((None, tf, D_MODEL), lambda i, j: (layer, j, 0)),
        ],
        out_specs=pl.BlockSpec((tm, D_MODEL), lambda i, j: (i, 0)),
        compiler_params=_cparams(("parallel", "arbitrary")),
        name="mlp",
    )(h2, x1, mod, lw['w_ff1'], lw['w_ff2'])


def _permute_w_in(w):
    base = OFF_QCN
    ckv0 = base + MLA_HEADS * MLA_QK
    qc = w[:, :, base:ckv0].reshape(DEPTH, D_MODEL, MLA_HEADS, MLA_QK)
    nope = qc[..., :MLA_NOPE].reshape(DEPTH, D_MODEL, MLA_HEADS * MLA_NOPE)
    rope = qc[..., MLA_NOPE:].reshape(DEPTH, D_MODEL, MLA_HEADS * MLA_ROPE)
    kr = w[:, :, ckv0 + MLA_KV_RANK:]
    parts = [w[:, :, :base], nope, rope, w[:, :, ckv0:ckv0 + MLA_KV_RANK], kr, kr]
    return jnp.concatenate(parts, axis=2).astype(BF16)


def _rope_tables(seq):
    t = jnp.arange(seq)
    row = (t // GRID_W).astype(F32)
    col = (t % GRID_W).astype(F32)

    def table(d):
        quarter = d // 4
        inv = ROPE_BASE ** (-(jnp.arange(quarter, dtype=F32) / quarter))
        ar = row[:, None] * inv[None, :]
        ac = col[:, None] * inv[None, :]
        cos = jnp.concatenate([jnp.cos(ar), jnp.cos(ar), jnp.cos(ac), jnp.cos(ac)], axis=-1)
        sin = jnp.concatenate([-jnp.sin(ar), jnp.sin(ar), -jnp.sin(ac), jnp.sin(ac)], axis=-1)
        reps = LANES // d
        return jnp.tile(cos, (1, reps)), jnp.tile(sin, (1, reps))

    cos_b, sin_b = table(GQA_HEAD_DIM)
    cos_c, sin_c = table(MLA_ROPE)
    return {'cos_b': cos_b, 'sin_b': sin_b, 'cos_c': cos_c, 'sin_c': sin_c}


def _ssm_params(lam_re, lam_im, log_dt, b_re, b_im, c_re, c_im):
    a = lam_re.astype(F32)
    w = lam_im.astype(F32)
    dt = jnp.exp(log_dt.astype(F32))[..., None]
    mag = jnp.exp(a * dt)
    lbr = mag * jnp.cos(w * dt)
    lbi = mag * jnp.sin(w * dt)
    den = a * a + w * w
    cr = (((lbr - 1.0) * a + lbi * w) / den)[..., None]
    ci = ((lbi * a - (lbr - 1.0) * w) / den)[..., None]
    bre = b_re.astype(F32)
    bim = b_im.astype(F32)
    bb_re = cr * bre - ci * bim
    bb_im = cr * bim + ci * bre
    lam_ri = jnp.stack([lbr.reshape(DEPTH, 2, SSM_N), lbi.reshape(DEPTH, 2, SSM_N)], axis=2)
    lam_b = jnp.broadcast_to(lam_ri[:, :, :, None, :], (DEPTH, 2, 2, SUBLANES, SSM_N))
    gh = SSM_GROUPS // 2
    eye = jnp.eye(gh, dtype=F32)[:, None, :, None]

    def blockdiag(x, rows_per_g, cols_per_g):
        x = x.reshape(DEPTH, 2, 2, gh, rows_per_g, 1, cols_per_g) * eye
        return x.reshape(DEPTH, 2, 2, gh * rows_per_g, gh * cols_per_g)

    def bmat(x):
        return blockdiag(jnp.swapaxes(x, -1, -2), SSM_GROUP, SSM_STATE)

    def cmat(x):
        return blockdiag(jnp.swapaxes(x, -1, -2), SSM_STATE, SSM_GROUP)

    b_cat = jnp.concatenate([bmat(bb_re), bmat(bb_im)], axis=-1).astype(BF16)
    return {'lam': lam_b, 'b': b_cat, 'c_re': cmat(c_re.astype(F32)).astype(BF16),
            'c_im': cmat(c_im.astype(F32)).astype(BF16)}


def _trunk_layer(x2d, mod, lw, sp, tabs, latent, ctx, layer):
    seq = DEC_SEQ if latent else SEQ
    bsz = x2d.shape[0] // seq
    u, qb, kb, vb, qc, ckv, kr, krg = _in_proj(x2d, mod, latent, lw, tabs, TM_PROJ, layer)

    u3 = u.reshape(bsz, seq, SSM_CH)
    y_tm, h_t = _ssm_scan(u3, ctx['h0'], sp, layer)
    ya = _ssm_glu(u3, y_tm, lw['ssm_d'], lw['ssm_w_glu'], layer).reshape(bsz * seq, SSM_CH)

    if latent:
        yb = _gqa_attn(qb, kb, vb, True, ctx['k'], ctx['v'], layer)
        yc = _mla_attn(qc, ckv, kr, krg, lw, True, ctx['ckv'], ctx['kr'], layer)
    else:
        yb = _gqa_attn(qb, kb, vb, False)
        yc = _mla_attn(qc, ckv, kr, krg, lw, False, layer=layer)

    x1, h2 = _out_proj(ya, yb, yc, x2d, mod, lw, latent, TM_PROJ, layer)
    x2 = _mlp(h2, x1, mod, lw, latent, TM_MLP, TF_MLP, layer)
    return x2, (kb, vb, ckv, kr, h_t)


def kernel(x_prompt, x_sample, cache_attn_k, cache_attn_v, cache_mla_ckv, cache_mla_krope, state_ssm, c, c_ctx, w_mod, b_mod, norm_mix, norm_mlp, w_in, gqa_q_norm, gqa_k_norm, mla_kv_norm, mla_q_norm, mla_k_norm, mla_w_uk, mla_w_uv, ssm_lam_re, ssm_lam_im, ssm_log_dt, ssm_b_re, ssm_b_im, ssm_c_re, ssm_c_im, ssm_d, ssm_w_glu, w_out, w_ff1, w_ff2):
    cvec = jnp.zeros((N_MOD, D_MODEL), F32).at[0].set(c_ctx).at[1:1 + DEC_BATCH].set(c)
    mod = _adaln(cvec, w_mod, b_mod.reshape(DEPTH, 1, 6 * D_MODEL)).reshape(DEPTH, N_MOD, 6, D_MODEL)

    tabs = _rope_tables(DEC_SEQ)
    cache_k = cache_attn_k.reshape(DEC_BATCH, DEPTH, PAST_LEN, GQA_KV_HEADS * GQA_HEAD_DIM)
    cache_v = cache_attn_v.reshape(DEC_BATCH, DEPTH, PAST_LEN, GQA_KV_HEADS * GQA_HEAD_DIM)

    xp = x_prompt.reshape(BATCH * SEQ, D_MODEL)
    xs = x_sample.reshape(DEC_BATCH * DEC_SEQ, D_MODEL)
    new_k, new_v, new_ckv, new_kr, new_ssm = [], [], [], [], []
    h0_zero = jnp.zeros((BATCH // SUBLANES, 2, 2, SUBLANES, SSM_N), F32)

    row3 = lambda a: a.reshape(DEPTH, 1, -1)
    dup = lambda a: jnp.concatenate([a, a], axis=-1)
    lw = {
        'norm_mix': row3(norm_mix), 'norm_mlp': row3(norm_mlp),
        'w_in': _permute_w_in(w_in),
        'w_out': w_out.astype(BF16), 'w_ff1': w_ff1.astype(BF16), 'w_ff2': w_ff2.astype(BF16),
        'gqa_q_norm': row3(gqa_q_norm), 'gqa_k_norm': row3(gqa_k_norm), 'mla_kv_norm': row3(mla_kv_norm),
        'mla_q_nope_g': row3(mla_q_norm[:, :MLA_NOPE]),
        'mla_q_rope_g': row3(dup(mla_q_norm[:, MLA_NOPE:])),
        'mla_k_nope_g': row3(mla_k_norm[:, :MLA_NOPE]),
        'mla_k_rope_g': row3(dup(mla_k_norm[:, MLA_NOPE:])),
        'mla_k_rope_g64': row3(mla_k_norm[:, MLA_NOPE:]),
        'mla_w_uk': mla_w_uk.reshape(DEPTH, MLA_KV_RANK, MLA_HEADS * MLA_NOPE).astype(BF16),
        'mla_w_uv': mla_w_uv.reshape(DEPTH, MLA_KV_RANK, MLA_HEADS * MLA_V).astype(BF16),
        'ssm_d': row3(ssm_d), 'ssm_w_glu': ssm_w_glu.astype(BF16),
    }
    sp = _ssm_params(ssm_lam_re, ssm_lam_im, ssm_log_dt, ssm_b_re, ssm_b_im, ssm_c_re, ssm_c_im)
    h0_lat = state_ssm.reshape(DEC_BATCH, DEPTH, 2, SSM_N, 2).transpose(1, 2, 4, 0, 3)[:, None]

    for l in range(DEPTH):
        xp, (k, v, ckv_n, kr, h_t) = _trunk_layer(xp, mod, lw, sp, tabs, False, {'h0': h0_zero}, l)
        new_k.append(k.reshape(BATCH, SEQ, GQA_KV_HEADS, GQA_HEAD_DIM))
        new_v.append(v.reshape(BATCH, SEQ, GQA_KV_HEADS, GQA_HEAD_DIM))
        new_ckv.append(ckv_n.reshape(BATCH, SEQ, MLA_KV_RANK))
        new_kr.append(kr.reshape(BATCH, SEQ, MLA_ROPE))
        hs = h_t.transpose(0, 3, 1, 4, 2).reshape(BATCH, 2, SSM_GROUPS, SSM_STATE, 2)
        new_ssm.append(hs)

        ctx = {'k': cache_k, 'v': cache_v, 'ckv': cache_mla_ckv, 'kr': cache_mla_krope, 'h0': h0_lat[l]}
        xs, _ = _trunk_layer(xs, mod, lw, sp, tabs, True, ctx, l)

    return (xp.reshape(BATCH, SEQ, D_MODEL), xs.reshape(DEC_BATCH, DEC_SEQ, D_MODEL),
            jnp.stack(new_k, axis=1), jnp.stack(new_v, axis=1), jnp.stack(new_ckv, axis=1),
            jnp.stack(new_kr, axis=1), jnp.stack(new_ssm, axis=1))
```

```python
import functools
import math

import numpy as np
import jax
import jax.numpy as jnp
from jax import lax
from jax.experimental import pallas as pl
from jax.experimental.pallas import tpu as pltpu

D_MODEL = 2048
BATCH = 32
SEQ = 256
DEPTH = 2
DEC_BATCH = 8
DEC_SEQ = 1024
PAST_LEN = 512
GRID_W = 64
ROPE_BASE = 10000.0
EPS = 1e-6
SSM_CH = 512
SSM_GROUP = 16
SSM_GROUPS = SSM_CH // SSM_GROUP
SSM_STATE = 64
SSM_N = SSM_GROUPS * SSM_STATE
GQA_HEADS = 6
GQA_KV_HEADS = 2
GQA_GROUP = GQA_HEADS // GQA_KV_HEADS
GQA_HEAD_DIM = 128
MLA_HEADS = 6
MLA_NOPE = 128
MLA_ROPE = 64
MLA_QK = MLA_NOPE + MLA_ROPE
MLA_QK_PAD = 256
MLA_V = 128
MLA_KV_RANK = 512
D_FF = 4 * D_MODEL
N_MOD = 16

OFF_U = 0
OFF_QB = OFF_U + SSM_CH
OFF_KB = OFF_QB + GQA_HEADS * GQA_HEAD_DIM
OFF_VB = OFF_KB + GQA_KV_HEADS * GQA_HEAD_DIM
OFF_QCN = OFF_VB + GQA_KV_HEADS * GQA_HEAD_DIM
OFF_QCR = OFF_QCN + MLA_HEADS * MLA_NOPE
OFF_CKV = OFF_QCR + MLA_HEADS * MLA_ROPE
OFF_KR = OFF_CKV + MLA_KV_RANK
IN_WIDTH_P = OFF_KR + 2 * MLA_ROPE

LANES = 128
SUBLANES = 8
VMEM_LIMIT = 56 * 1024 * 1024

TM_PROJ = 512
TM_MLP = 1024
TF_MLP = 512
ROW_PIECES = 2

BF16 = jnp.bfloat16
F32 = jnp.float32


def _cparams(sem):
    return pltpu.CompilerParams(dimension_semantics=sem, vmem_limit_bytes=VMEM_LIMIT)


def _dot(a, b):
    return jnp.dot(a, b, preferred_element_type=F32)


def _dot_nt(a, b):
    return lax.dot_general(a, b, (((1,), (1,)), ((), ())), preferred_element_type=F32)


def _adaln_kernel(c_ref, w_ref, b_ref, o_ref):
    c = c_ref[...]
    s = (c * jax.nn.sigmoid(c)).astype(BF16)
    o_ref[...] = _dot(s, w_ref[...].astype(BF16)) + b_ref[...]


def _adaln(cvec, w_mod, b_mod):
    tn = 1024
    return pl.pallas_call(
        _adaln_kernel,
        out_shape=jax.ShapeDtypeStruct((DEPTH, N_MOD, 6 * D_MODEL), F32),
        grid=(DEPTH, 6 * D_MODEL // tn),
        in_specs=[
            pl.BlockSpec((N_MOD, D_MODEL), lambda l, j: (0, 0)),
            pl.BlockSpec((None, D_MODEL, tn), lambda l, j: (l, 0, j)),
            pl.BlockSpec((None, 1, tn), lambda l, j: (l, 0, j)),
        ],
        out_specs=pl.BlockSpec((None, N_MOD, tn), lambda l, j: (l, 0, j)),
        compiler_params=_cparams(("parallel", "parallel")),
        name="adaln",
    )(cvec, w_mod, b_mod)


def _swap_halves(x, block):
    lane = lax.broadcasted_iota(jnp.int32, x.shape, 1)
    first = (lane % (2 * block)) < block
    return jnp.where(first, pltpu.roll(x, LANES - block, 1), pltpu.roll(x, block, 1))


def _rope(x, cos, sin_signed, block):
    return x * cos + _swap_halves(x, block) * sin_signed


def _in_proj_kernel(latent, x_ref, mod_ref, gmix_ref, w_ref, gq_ref, gk_ref, gkv_ref, gqn_ref, gqr_ref, gkr_ref,
                    cosb_ref, sinb_ref, cosc_ref, sinc_ref,
                    u_ref, qb_ref, kb_ref, vb_ref, qc_ref, ckv_ref, kr_ref, krg_ref):
    x = x_ref[...]
    ms = jnp.mean(x * x, axis=-1, keepdims=True)
    y = x * lax.rsqrt(ms + EPS) * gmix_ref[...]
    h = (y * (1.0 + mod_ref[1:2, :]) + mod_ref[0:1, :]).astype(BF16)

    def proj(off, width):
        return _dot(h, w_ref[:, off:off + width])

    scale_b = 1.0 / math.sqrt(GQA_HEAD_DIM)
    zq = proj(OFF_QB, GQA_HEADS * GQA_HEAD_DIM)
    for hd in range(GQA_HEADS):
        col = zq[:, hd * LANES:(hd + 1) * LANES]
        q = col * lax.rsqrt(jnp.mean(col * col, axis=-1, keepdims=True) + EPS) * gq_ref[...]
        if latent:
            q = _rope(q, cosb_ref[...], sinb_ref[...], GQA_HEAD_DIM // 4)
        qb_ref[:, hd * LANES:(hd + 1) * LANES] = (q * scale_b).astype(qb_ref.dtype)
    zk = proj(OFF_KB, GQA_KV_HEADS * GQA_HEAD_DIM)
    for hd in range(GQA_KV_HEADS):
        col = zk[:, hd * LANES:(hd + 1) * LANES]
        k = col * lax.rsqrt(jnp.mean(col * col, axis=-1, keepdims=True) + EPS) * gk_ref[...]
        if latent:
            k = _rope(k, cosb_ref[...], sinb_ref[...], GQA_HEAD_DIM // 4)
        kb_ref[:, hd * LANES:(hd + 1) * LANES] = k.astype(kb_ref.dtype)

    scale_c = 1.0 / math.sqrt(MLA_QK)
    zn = proj(OFF_QCN, MLA_HEADS * MLA_NOPE)
    zr = proj(OFF_QCR, MLA_HEADS * MLA_ROPE)
    lane = lax.broadcasted_iota(jnp.int32, (x.shape[0], LANES), 1)
    low = lane < MLA_ROPE
    for pair in range(MLA_HEADS // 2):
        colr = zr[:, pair * LANES:(pair + 1) * LANES]
        sq = colr * colr
        ss_lo = jnp.sum(jnp.where(low, sq, 0.0), axis=-1, keepdims=True)
        ss_hi = jnp.sum(jnp.where(low, 0.0, sq), axis=-1, keepdims=True)
        rs = []
        for half, ss_r in ((0, ss_lo), (1, ss_hi)):
            hd = 2 * pair + half
            coln = zn[:, hd * LANES:(hd + 1) * LANES]
            ss = jnp.sum(coln * coln, axis=-1, keepdims=True) + ss_r
            r = lax.rsqrt(ss * (1.0 / MLA_QK) + EPS)
            rs.append(r)
            qc_ref[hd, :, :MLA_NOPE] = (coln * r * gqn_ref[...] * scale_c).astype(qc_ref.dtype)
        qr = colr * jnp.where(low, rs[0], rs[1]) * gqr_ref[...]
        if latent:
            qr = _rope(qr, cosc_ref[...], sinc_ref[...], MLA_ROPE // 4)
        qr = qr * scale_c
        zeros = jnp.zeros((x.shape[0], MLA_QK_PAD - MLA_QK), qc_ref.dtype)
        for half in range(2):
            hd = 2 * pair + half
            qc_ref[hd, :, MLA_NOPE:MLA_QK] = qr[:, half * MLA_ROPE:(half + 1) * MLA_ROPE].astype(qc_ref.dtype)
            qc_ref[hd, :, MLA_QK:] = zeros

    zc = proj(OFF_CKV, MLA_KV_RANK)
    ckv_ref[...] = (zc * lax.rsqrt(jnp.mean(zc * zc, axis=-1, keepdims=True) + EPS) * gkv_ref[...]).astype(ckv_ref.dtype)
    zkr = proj(OFF_KR, 2 * MLA_ROPE)
    kr_ref[...] = zkr[:, :MLA_ROPE]
    krg = zkr * gkr_ref[...]
    if latent:
        krg = _rope(krg, cosc_ref[...], sinc_ref[...], MLA_ROPE // 4)
    krg_ref[...] = krg[:, :MLA_ROPE]

    vb_ref[...] = proj(OFF_VB, GQA_KV_HEADS * GQA_HEAD_DIM).astype(vb_ref.dtype)
    u_ref[...] = proj(OFF_U, SSM_CH)


def _lspec(block_tail, layer):
    zeros = (0,) * len(block_tail)
    return pl.BlockSpec((None,) + tuple(block_tail), lambda *_: (layer,) + zeros)


def _mod_spec(latent, tm, layer):
    tiles_per_seq = DEC_SEQ // tm
    if latent:
        return pl.BlockSpec((None, None, 6, D_MODEL), lambda i, *_: (layer, 1 + i // tiles_per_seq, 0, 0))
    return pl.BlockSpec((None, None, 6, D_MODEL), lambda i, *_: (layer, 0, 0, 0))


def _in_proj(x2d, mod, latent, lw, tabs, tm, layer):
    rows = x2d.shape[0]
    n_tiles = rows // tm
    tiles_per_seq = DEC_SEQ // tm
    if latent:
        tab_map = lambda i: (i % tiles_per_seq, 0)
    else:
        tab_map = lambda i: (0, 0)
    row = lambda i: (i, 0)
    act_dt = BF16 if latent else F32
    in_specs = [
        pl.BlockSpec((tm, D_MODEL), row),
        _mod_spec(latent, tm, layer),
        _lspec((1, D_MODEL), layer),
        _lspec((D_MODEL, IN_WIDTH_P), layer),
        _lspec((1, LANES), layer),
        _lspec((1, LANES), layer),
        _lspec((1, MLA_KV_RANK), layer),
        _lspec((1, LANES), layer),
        _lspec((1, LANES), layer),
        _lspec((1, LANES), layer),
        pl.BlockSpec((tm, LANES), tab_map),
        pl.BlockSpec((tm, LANES), tab_map),
        pl.BlockSpec((tm, LANES), tab_map),
        pl.BlockSpec((tm, LANES), tab_map),
    ]
    out_shape = (
        jax.ShapeDtypeStruct((rows, SSM_CH), F32),
        jax.ShapeDtypeStruct((rows, GQA_HEADS * GQA_HEAD_DIM), BF16),
        jax.ShapeDtypeStruct((rows, GQA_KV_HEADS * GQA_HEAD_DIM), act_dt),
        jax.ShapeDtypeStruct((rows, GQA_KV_HEADS * GQA_HEAD_DIM), act_dt),
        jax.ShapeDtypeStruct((MLA_HEADS, rows, MLA_QK_PAD), BF16),
        jax.ShapeDtypeStruct((rows, MLA_KV_RANK), act_dt),
        jax.ShapeDtypeStruct((rows, MLA_ROPE), F32),
        jax.ShapeDtypeStruct((rows, MLA_ROPE), F32),
    )
    out_specs = (
        pl.BlockSpec((tm, SSM_CH), row),
        pl.BlockSpec((tm, GQA_HEADS * GQA_HEAD_DIM), row),
        pl.BlockSpec((tm, GQA_KV_HEADS * GQA_HEAD_DIM), row),
        pl.BlockSpec((tm, GQA_KV_HEADS * GQA_HEAD_DIM), row),
        pl.BlockSpec((MLA_HEADS, tm, MLA_QK_PAD), lambda i: (0, i, 0)),
        pl.BlockSpec((tm, MLA_KV_RANK), row),
        pl.BlockSpec((tm, MLA_ROPE), row),
        pl.BlockSpec((tm, MLA_ROPE), row),
    )
    return pl.pallas_call(
        functools.partial(_in_proj_kernel, latent),
        out_shape=out_shape,
        grid=(n_tiles,),
        in_specs=in_specs,
        out_specs=out_specs,
        compiler_params=_cparams(("parallel",)),
        name="in_proj_lat" if latent else "in_proj_ctx",
    )(x2d, mod, lw['norm_mix'], lw['w_in'], lw['gqa_q_norm'], lw['gqa_k_norm'], lw['mla_kv_norm'],
      lw['mla_q_nope_g'], lw['mla_q_rope_g'], lw['mla_k_rope_g'],
      tabs['cos_b'], tabs['sin_b'], tabs['cos_c'], tabs['sin_c'])


SSM_TC = 128
SSM_LANE_SPLIT = 2
SSM_UNROLL = 8


def _ssm_scan_kernel(n_chunks, u_ref, h0_ref, lam_ref, b_ref, cre_ref, cim_ref, y_ref, hT_ref,
                     utm_ref, sre_ref, sim_ref, h_ref):
    d = pl.program_id(1)
    i = pl.program_id(2)
    tc = SSM_TC

    @pl.when(i == 0)
    def _():
        h_ref[...] = h0_ref[...]

    for b in range(SUBLANES):
        for j in range(SSM_CH // LANES):
            utm_ref[j, pl.ds(b, tc, stride=SUBLANES), :] = u_ref[b, :, j * LANES:(j + 1) * LANES]

    half_n = SSM_N // 2
    half_c = SSM_CH // 2
    for k in range(2):
        uk = jnp.concatenate([utm_ref[2 * k], utm_ref[2 * k + 1]], axis=-1).astype(BF16)
        r = _dot(uk, b_ref[k])
        sre_ref[:, k * half_n:(k + 1) * half_n] = r[:, :half_n]
        sim_ref[:, k * half_n:(k + 1) * half_n] = r[:, half_n:]

    piece = SSM_N // SSM_LANE_SPLIT
    for p in range(SSM_LANE_SPLIT):
        sl = slice(p * piece, (p + 1) * piece)
        lr = lam_ref[0, :, sl]
        li = lam_ref[1, :, sl]

        def step(s, carry):
            hr, hi = carry
            t = s + d * (tc - 1 - 2 * s)
            rowi = pl.multiple_of(t * SUBLANES, SUBLANES)
            nr = lr * hr - li * hi + sre_ref[pl.ds(rowi, SUBLANES), sl]
            ni = lr * hi + li * hr + sim_ref[pl.ds(rowi, SUBLANES), sl]
            sre_ref[pl.ds(rowi, SUBLANES), sl] = nr
            sim_ref[pl.ds(rowi, SUBLANES), sl] = ni
            return nr, ni

        hr, hi = lax.fori_loop(0, tc, step, (h_ref[0, :, sl], h_ref[1, :, sl]), unroll=SSM_UNROLL)
        h_ref[0, :, sl] = hr
        h_ref[1, :, sl] = hi

    for k in range(2):
        yk = (_dot(sre_ref[:, k * half_n:(k + 1) * half_n].astype(BF16), cre_ref[k])
              - _dot(sim_ref[:, k * half_n:(k + 1) * half_n].astype(BF16), cim_ref[k]))
        y_ref[2 * k] = yk[:, :LANES]
        y_ref[2 * k + 1] = yk[:, LANES:]

    @pl.when(i == n_chunks - 1)
    def _():
        hT_ref[...] = h_ref[...]


def _ssm_scan(u3, h0, sp, layer):
    bsz, t, _ = u3.shape
    n_groups = bsz // SUBLANES
    n_chunks = t // SSM_TC
    tc = SSM_TC

    def chunk(d, i):
        return i + d * (n_chunks - 1 - 2 * i)

    return pl.pallas_call(
        functools.partial(_ssm_scan_kernel, n_chunks),
        out_shape=(
            jax.ShapeDtypeStruct((2, n_groups, SSM_CH // LANES, t * SUBLANES, LANES), F32),
            jax.ShapeDtypeStruct((n_groups, 2, 2, SUBLANES, SSM_N), F32),
        ),
        grid=(n_groups, 2, n_chunks),
        in_specs=[
            pl.BlockSpec((SUBLANES, tc, SSM_CH), lambda g, d, i: (g, chunk(d, i), 0)),
            pl.BlockSpec((None, None, 2, SUBLANES, SSM_N), lambda g, d, i: (g, d, 0, 0, 0)),
            pl.BlockSpec((None, None, 2, SUBLANES, SSM_N), lambda g, d, i: (layer, d, 0, 0, 0)),
            pl.BlockSpec((None, None, 2, SSM_CH // 2, SSM_N), lambda g, d, i: (layer, d, 0, 0, 0)),
            pl.BlockSpec((None, None, 2, SSM_N // 2, SSM_CH // 2), lambda g, d, i: (layer, d, 0, 0, 0)),
            pl.BlockSpec((None, None, 2, SSM_N // 2, SSM_CH // 2), lambda g, d, i: (layer, d, 0, 0, 0)),
        ],
        out_specs=(
            pl.BlockSpec((None, None, SSM_CH // LANES, tc * SUBLANES, LANES),
                         lambda g, d, i: (d, g, 0, chunk(d, i), 0)),
            pl.BlockSpec((None, None, 2, SUBLANES, SSM_N), lambda g, d, i: (g, d, 0, 0, 0)),
        ),
        scratch_shapes=[
            pltpu.VMEM((SSM_CH // LANES, tc * SUBLANES, LANES), F32),
            pltpu.VMEM((tc * SUBLANES, SSM_N), F32),
            pltpu.VMEM((tc * SUBLANES, SSM_N), F32),
            pltpu.VMEM((2, SUBLANES, SSM_N), F32),
        ],
        compiler_params=_cparams(("parallel", "parallel", "arbitrary")),
        name="ssm_scan",
    )(u3, h0, sp['lam'], sp['b'], sp['c_re'], sp['c_im'])


def _ssm_glu_kernel(u_ref, y_ref, d_ref, w_ref, o_ref, ycat_ref):
    tc = SSM_TC
    for b in range(SUBLANES):
        for j in range(SSM_CH // LANES):
            sl = slice(j * LANES, (j + 1) * LANES)
            rows_b = pl.ds(b, tc, stride=SUBLANES)
            ycat_ref[b * tc:(b + 1) * tc, sl] = (y_ref[0, j, rows_b, :] + y_ref[1, j, rows_b, :]
                                                + d_ref[:, sl] * u_ref[b, :, sl])
    zg = _dot(ycat_ref[...].astype(BF16), w_ref[...])
    out = zg[:, :SSM_CH] * jax.nn.sigmoid(zg[:, SSM_CH:])
    o_ref[...] = out.reshape(SUBLANES, tc, SSM_CH).astype(o_ref.dtype)


def _ssm_glu(u3, y, d_skip, w_glu, layer):
    bsz, t, _ = u3.shape
    n_groups = bsz // SUBLANES
    n_chunks = t // SSM_TC
    tc = SSM_TC
    return pl.pallas_call(
        _ssm_glu_kernel,
        out_shape=jax.ShapeDtypeStruct((bsz, t, SSM_CH), BF16),
        grid=(n_groups, n_chunks),
        in_specs=[
            pl.BlockSpec((SUBLANES, tc, SSM_CH), lambda g, i: (g, i, 0)),
            pl.BlockSpec((2, None, SSM_CH // LANES, tc * SUBLANES, LANES), lambda g, i: (0, g, 0, i, 0)),
            _lspec((1, SSM_CH), layer),
            _lspec((SSM_CH, 2 * SSM_CH), layer),
        ],
        out_specs=pl.BlockSpec((SUBLANES, tc, SSM_CH), lambda g, i: (g, i, 0)),
        scratch_shapes=[pltpu.VMEM((SUBLANES * tc, SSM_CH), F32)],
        compiler_params=_cparams(("parallel", "parallel")),
        name="ssm_glu",
    )(u3, y, d_skip, w_glu)


def _ones_column(n):
    return jnp.ones((n, LANES), BF16)


def _softmax_pv(scores, v_ext):
    m = jnp.max(scores, axis=-1, keepdims=True)
    p = jnp.exp((scores - m).astype(BF16))
    o = _dot(p, v_ext)
    return o[:, :LANES] / o[:, LANES:]


ATTN_TQ = 256


def _for_each_q_block(nb, seq, body):
    nq = seq // ATTN_TQ
    for s in range(nb):
        for qi in range(nq):
            body(s, s * seq + qi * ATTN_TQ)


def _gqa_kernel(latent, nb, seq, *refs):
    if latent:
        q_ref, k_ref, v_ref, kc_ref, vc_ref, o_ref, k_s, v_s = refs
    else:
        q_ref, k_ref, v_ref, o_ref, k_s, v_s = refs
    tq = ATTN_TQ
    for s in range(nb):
        for h in range(GQA_KV_HEADS):
            hl = slice(h * LANES, (h + 1) * LANES)
            k_s[s, h, 0:seq, :] = k_ref[s * seq:(s + 1) * seq, hl].astype(BF16)
            v_s[s, h, 0:seq, :LANES] = v_ref[s * seq:(s + 1) * seq, hl].astype(BF16)
            if latent:
                k_s[s, h, seq:, :] = kc_ref[:, hl].astype(BF16)
                v_s[s, h, seq:, :LANES] = vc_ref[:, hl].astype(BF16)
            v_s[s, h, :, LANES:] = _ones_column(v_s.shape[2])

    def body(s, r0):
        for h in range(GQA_KV_HEADS):
            heads = [h * GQA_GROUP + g for g in range(GQA_GROUP)]
            q3 = jnp.concatenate([q_ref[pl.ds(r0, tq), hd * LANES:(hd + 1) * LANES] for hd in heads], axis=0)
            o = _softmax_pv(_dot_nt(q3, k_s[s, h]), v_s[s, h])
            for g, hd in enumerate(heads):
                o_ref[pl.ds(r0, tq), hd * LANES:(hd + 1) * LANES] = o[g * tq:(g + 1) * tq].astype(o_ref.dtype)

    _for_each_q_block(nb, seq, body)


def _gqa_attn(qb, kb, vb, latent, cache_k=None, cache_v=None, layer=0):
    rows = qb.shape[0]
    seq = DEC_SEQ if latent else SEQ
    nb = 1 if latent else 4
    t_all = seq + (PAST_LEN if latent else 0)
    qw = GQA_HEADS * GQA_HEAD_DIM
    kw = GQA_KV_HEADS * GQA_HEAD_DIM
    row = lambda i: (i, 0)
    in_specs = [
        pl.BlockSpec((nb * seq, qw), row),
        pl.BlockSpec((nb * seq, kw), row),
        pl.BlockSpec((nb * seq, kw), row),
    ]
    args = [qb, kb, vb]
    if latent:
        cspec = pl.BlockSpec((None, None, PAST_LEN, kw), lambda i: (i, layer, 0, 0))
        in_specs += [cspec, cspec]
        args += [cache_k, cache_v]
    return pl.pallas_call(
        functools.partial(_gqa_kernel, latent, nb, seq),
        out_shape=jax.ShapeDtypeStruct((rows, qw), BF16),
        grid=(rows // (nb * seq),),
        in_specs=in_specs,
        out_specs=pl.BlockSpec((nb * seq, qw), row),
        scratch_shapes=[
            pltpu.VMEM((nb, GQA_KV_HEADS, t_all, GQA_HEAD_DIM), BF16),
            pltpu.VMEM((nb, GQA_KV_HEADS, t_all, 2 * LANES), BF16),
        ],
        compiler_params=_cparams(("parallel",)),
        name="gqa_lat" if latent else "gqa_ctx",
    )(*args)


def _mla_kernel(latent, nb, seq, *refs):
    if latent:
        (q_ref, ckv_ref, kr_ref, krg_ref, ckvc_ref, krc_ref, wuk_ref, wuv_ref, gn_ref, gr_ref,
         o_ref, k_s, v_s) = refs
    else:
        (q_ref, ckv_ref, kr_ref, krg_ref, wuk_ref, wuv_ref, gn_ref, gr_ref, o_ref, k_s, v_s) = refs
    tq = ATTN_TQ

    def expand(ckv, kr_raw, krg, place):
        c = ckv.astype(BF16)
        kn_all = _dot(c, wuk_ref[...])
        v_all = _dot(c, wuv_ref[...])
        ss_kr = jnp.sum(kr_raw * kr_raw, axis=-1, keepdims=True)
        for h in range(MLA_HEADS):
            hl = slice(h * LANES, (h + 1) * LANES)
            kn = kn_all[:, hl]
            rs = lax.rsqrt((jnp.sum(kn * kn, axis=-1, keepdims=True) + ss_kr) * (1.0 / MLA_QK) + EPS)
            kn = (kn * rs * gn_ref[...]).astype(BF16)
            kp = (krg * rs).astype(BF16)
            vv = v_all[:, hl].astype(BF16)
            for s, lo, n, rows in place:
                k_s[s, h, lo:lo + n, :MLA_NOPE] = kn[rows]
                k_s[s, h, lo:lo + n, MLA_NOPE:MLA_QK] = kp[rows]
                k_s[s, h, lo:lo + n, MLA_QK:] = jnp.zeros((n, MLA_QK_PAD - MLA_QK), BF16)
                v_s[s, h, lo:lo + n, :MLA_V] = vv[rows]
                v_s[s, h, lo:lo + n, MLA_V:] = _ones_column(n)

    expand(ckv_ref[...], kr_ref[...], krg_ref[...],
           [(s, 0, seq, slice(s * seq, (s + 1) * seq)) for s in range(nb)])
    if latent:
        krc = krc_ref[...]
        expand(ckvc_ref[...], krc, krc * gr_ref[...], [(0, seq, PAST_LEN, slice(0, PAST_LEN))])

    def body(s, r0):
        for h in range(MLA_HEADS):
            sc = _dot_nt(q_ref[h, pl.ds(r0, tq), :], k_s[s, h])
            o_ref[pl.ds(r0, tq), h * LANES:(h + 1) * LANES] = _softmax_pv(sc, v_s[s, h]).astype(o_ref.dtype)

    _for_each_q_block(nb, seq, body)


def _mla_attn(qc, ckv, kr, krg, lw, latent, cache_ckv=None, cache_kr=None, layer=0):
    rows = qc.shape[1]
    seq = DEC_SEQ if latent else SEQ
    nb = 1 if latent else 4
    t_all = seq + (PAST_LEN if latent else 0)
    row = lambda i: (i, 0)
    full = lambda i: (0, 0)
    in_specs = [
        pl.BlockSpec((MLA_HEADS, nb * seq, MLA_QK_PAD), lambda i: (0, i, 0)),
        pl.BlockSpec((nb * seq, MLA_KV_RANK), row),
        pl.BlockSpec((nb * seq, MLA_ROPE), row),
        pl.BlockSpec((nb * seq, MLA_ROPE), row),
    ]
    args = [qc, ckv, kr, krg]
    if latent:
        in_specs += [
            pl.BlockSpec((None, None, PAST_LEN, MLA_KV_RANK), lambda i: (i, layer, 0, 0)),
            pl.BlockSpec((None, None, PAST_LEN, MLA_ROPE), lambda i: (i, layer, 0, 0)),
        ]
        args += [cache_ckv, cache_kr]
    in_specs += [
        _lspec((MLA_KV_RANK, MLA_HEADS * MLA_NOPE), layer),
        _lspec((MLA_KV_RANK, MLA_HEADS * MLA_V), layer),
        _lspec((1, MLA_NOPE), layer),
        _lspec((1, MLA_ROPE), layer),
    ]
    args += [lw['mla_w_uk'], lw['mla_w_uv'], lw['mla_k_nope_g'], lw['mla_k_rope_g64']]
    return pl.pallas_call(
        functools.partial(_mla_kernel, latent, nb, seq),
        out_shape=jax.ShapeDtypeStruct((rows, MLA_HEADS * MLA_V), BF16),
        grid=(rows // (nb * seq),),
        in_specs=in_specs,
        out_specs=pl.BlockSpec((nb * seq, MLA_HEADS * MLA_V), row),
        scratch_shapes=[
            pltpu.VMEM((nb, MLA_HEADS, t_all, MLA_QK_PAD), BF16),
            pltpu.VMEM((nb, MLA_HEADS, t_all, MLA_V + LANES), BF16),
        ],
        compiler_params=_cparams(("parallel",)),
        name="mla_lat" if latent else "mla_ctx",
    )(*args)


def _out_proj_kernel(ya_ref, yb_ref, yc_ref, x_ref, mod_ref, w_ref, g_ref, x1_ref, h2_ref):
    wa = SSM_CH
    wb = wa + GQA_HEADS * GQA_HEAD_DIM
    piece = x_ref.shape[0] // ROW_PIECES
    pieces = [slice(r * piece, (r + 1) * piece) for r in range(ROW_PIECES)]
    outs = [(_dot(ya_ref[rs, :], w_ref[0:wa, :]) + _dot(yb_ref[rs, :], w_ref[wa:wb, :])
             + _dot(yc_ref[rs, :], w_ref[wb:, :])) for rs in pieces]
    for rs, o in zip(pieces, outs):
        x1 = x_ref[rs, :] + mod_ref[2:3, :] * o
        x1_ref[rs, :] = x1
        ms = jnp.mean(x1 * x1, axis=-1, keepdims=True)
        y = x1 * lax.rsqrt(ms + EPS) * g_ref[...]
        h2_ref[rs, :] = (y * (1.0 + mod_ref[4:5, :]) + mod_ref[3:4, :]).astype(h2_ref.dtype)


def _out_proj(ya, yb, yc, x2d, mod, lw, latent, tm, layer):
    rows = x2d.shape[0]
    row = lambda i: (i, 0)
    return pl.pallas_call(
        _out_proj_kernel,
        out_shape=(jax.ShapeDtypeStruct((rows, D_MODEL), F32), jax.ShapeDtypeStruct((rows, D_MODEL), BF16)),
        grid=(rows // tm,),
        in_specs=[
            pl.BlockSpec((tm, SSM_CH), row),
            pl.BlockSpec((tm, GQA_HEADS * GQA_HEAD_DIM), row),
            pl.BlockSpec((tm, MLA_HEADS * MLA_V), row),
            pl.BlockSpec((tm, D_MODEL), row),
            _mod_spec(latent, tm, layer),
            _lspec((D_MODEL, D_MODEL), layer),
            _lspec((1, D_MODEL), layer),
        ],
        out_specs=(pl.BlockSpec((tm, D_MODEL), row), pl.BlockSpec((tm, D_MODEL), row)),
        compiler_params=_cparams(("parallel",)),
        name="out_proj",
    )(ya, yb, yc, x2d, mod, lw['w_out'], lw['norm_mlp'])


def _mlp_kernel(h2_ref, x1_ref, mod_ref, w1_ref, w2_ref, o_ref):
    j = pl.program_id(1)

    @pl.when(j == 0)
    def _():
        o_ref[...] = jnp.zeros_like(o_ref)

    f = jnp.maximum(_dot(h2_ref[...], w1_ref[...]), 0.0)
    o_ref[...] += _dot((f * f).astype(BF16), w2_ref[...])

    @pl.when(j == pl.num_programs(1) - 1)
    def _():
        o_ref[...] = x1_ref[...] + mod_ref[5:6, :] * o_ref[...]


def _mlp(h2, x1, mod, lw, latent, tm, tf, layer):
    rows = h2.shape[0]
    return pl.pallas_call(
        _mlp_kernel,
        out_shape=jax.ShapeDtypeStruct((rows, D_MODEL), F32),
        grid=(rows // tm, D_FF // tf),
        in_specs=[
            pl.BlockSpec((tm, D_MODEL), lambda i, j: (i, 0)),
            pl.BlockSpec((tm, D_MODEL), lambda i, j: (i, 0), pipeline_mode=pl.Buffered(1)),
            _mod_spec(latent, tm, layer),
            pl.BlockSpec((None, D_MODEL, tf), lambda i, j: (layer, 0, j)),
            pl.BlockSpec((None, tf, D_MODEL), lambda i, j: (layer, j, 0)),
        ],
        out_specs=pl.BlockSpec((tm, D_MODEL), lambda i, j: (i, 0)),
        compiler_params=_cparams(("parallel", "arbitrary")),
        name="mlp",
    )(h2, x1, mod, lw['w_ff1'], lw['w_ff2'])


def _permute_w_in(w):
    base = OFF_QCN
    ckv0 = base + MLA_HEADS * MLA_QK
    qc = w[:, :, base:ckv0].reshape(DEPTH, D_MODEL, MLA_HEADS, MLA_QK)
    nope = qc[..., :MLA_NOPE].reshape(DEPTH, D_MODEL, MLA_HEADS * MLA_NOPE)
    rope = qc[..., MLA_NOPE:].reshape(DEPTH, D_MODEL, MLA_HEADS * MLA_ROPE)
    kr = w[:, :, ckv0 + MLA_KV_RANK:]
    parts = [w[:, :, :base], nope, rope, w[:, :, ckv0:ckv0 + MLA_KV_RANK], kr, kr]
    return jnp.concatenate(parts, axis=2).astype(BF16)


def _rope_tables(seq):
    t = jnp.arange(seq)
    row = (t // GRID_W).astype(F32)
    col = (t % GRID_W).astype(F32)

    def table(d):
        quarter = d // 4
        inv = ROPE_BASE ** (-(jnp.arange(quarter, dtype=F32) / quarter))
        ar = row[:, None] * inv[None, :]
        ac = col[:, None] * inv[None, :]
        cos = jnp.concatenate([jnp.cos(ar), jnp.cos(ar), jnp.cos(ac), jnp.cos(ac)], axis=-1)
        sin = jnp.concatenate([-jnp.sin(ar), jnp.sin(ar), -jnp.sin(ac), jnp.sin(ac)], axis=-1)
        reps = LANES // d
        return jnp.tile(cos, (1, reps)), jnp.tile(sin, (1, reps))

    cos_b, sin_b = table(GQA_HEAD_DIM)
    cos_c, sin_c = table(MLA_ROPE)
    return {'cos_b': cos_b, 'sin_b': sin_b, 'cos_c': cos_c, 'sin_c': sin_c}


def _ssm_params(lam_re, lam_im, log_dt, b_re, b_im, c_re, c_im):
    a = lam_re.astype(F32)
    w = lam_im.astype(F32)
    dt = jnp.exp(log_dt.astype(F32))[..., None]
    mag = jnp.exp(a * dt)
    lbr = mag * jnp.cos(w * dt)
    lbi = mag * jnp.sin(w * dt)
    den = a * a + w * w
    cr = (((lbr - 1.0) * a + lbi * w) / den)[..., None]
    ci = ((lbi * a - (lbr - 1.0) * w) / den)[..., None]
    bre = b_re.astype(F32)
    bim = b_im.astype(F32)
    bb_re = cr * bre - ci * bim
    bb_im = cr * bim + ci * bre
    lam_ri = jnp.stack([lbr.reshape(DEPTH, 2, SSM_N), lbi.reshape(DEPTH, 2, SSM_N)], axis=2)
    lam_b = jnp.broadcast_to(lam_ri[:, :, :, None, :], (DEPTH, 2, 2, SUBLANES, SSM_N))
    gh = SSM_GROUPS // 2
    eye = jnp.eye(gh, dtype=F32)[:, None, :, None]

    def blockdiag(x, rows_per_g, cols_per_g):
        x = x.reshape(DEPTH, 2, 2, gh, rows_per_g, 1, cols_per_g) * eye
        return x.reshape(DEPTH, 2, 2, gh * rows_per_g, gh * cols_per_g)

    def bmat(x):
        return blockdiag(jnp.swapaxes(x, -1, -2), SSM_GROUP, SSM_STATE)

    def cmat(x):
        return blockdiag(jnp.swapaxes(x, -1, -2), SSM_STATE, SSM_GROUP)

    b_cat = jnp.concatenate([bmat(bb_re), bmat(bb_im)], axis=-1).astype(BF16)
    return {'lam': lam_b, 'b': b_cat, 'c_re': cmat(c_re.astype(F32)).astype(BF16),
            'c_im': cmat(c_im.astype(F32)).astype(BF16)}


def _trunk_layer(x2d, mod, lw, sp, tabs, latent, ctx, layer):
    seq = DEC_SEQ if latent else SEQ
    bsz = x2d.shape[0] // seq
    u, qb, kb, vb, qc, ckv, kr, krg = _in_proj(x2d, mod, latent, lw, tabs, TM_PROJ, layer)

    u3 = u.reshape(bsz, seq, SSM_CH)
    y_tm, h_t = _ssm_scan(u3, ctx['h0'], sp, layer)
    ya = _ssm_glu(u3, y_tm, lw['ssm_d'], lw['ssm_w_glu'], layer).reshape(bsz * seq, SSM_CH)

    if latent:
        yb = _gqa_attn(qb, kb, vb, True, ctx['k'], ctx['v'], layer)
        yc = _mla_attn(qc, ckv, kr, krg, lw, True, ctx['ckv'], ctx['kr'], layer)
    else:
        yb = _gqa_attn(qb, kb, vb, False)
        yc = _mla_attn(qc, ckv, kr, krg, lw, False, layer=layer)

    x1, h2 = _out_proj(ya, yb, yc, x2d, mod, lw, latent, TM_PROJ, layer)
    x2 = _mlp(h2, x1, mod, lw, latent, TM_MLP, TF_MLP, layer)
    return x2, (kb, vb, ckv, kr, h_t)


def kernel(x_prompt, x_sample, cache_attn_k, cache_attn_v, cache_mla_ckv, cache_mla_krope, state_ssm, c, c_ctx, w_mod, b_mod, norm_mix, norm_mlp, w_in, gqa_q_norm, gqa_k_norm, mla_kv_norm, mla_q_norm, mla_k_norm, mla_w_uk, mla_w_uv, ssm_lam_re, ssm_lam_im, ssm_log_dt, ssm_b_re, ssm_b_im, ssm_c_re, ssm_c_im, ssm_d, ssm_w_glu, w_out, w_ff1, w_ff2):
    cvec = jnp.zeros((N_MOD, D_MODEL), F32).at[0].set(c_ctx).at[1:1 + DEC_BATCH].set(c)
    mod = _adaln(cvec, w_mod, b_mod.reshape(DEPTH, 1, 6 * D_MODEL)).reshape(DEPTH, N_MOD, 6, D_MODEL)

    tabs = _rope_tables(DEC_SEQ)
    cache_k = cache_attn_k.reshape(DEC_BATCH, DEPTH, PAST_LEN, GQA_KV_HEADS * GQA_HEAD_DIM)
    cache_v = cache_attn_v.reshape(DEC_BATCH, DEPTH, PAST_LEN, GQA_KV_HEADS * GQA_HEAD_DIM)

    xp = x_prompt.reshape(BATCH * SEQ, D_MODEL)
    xs = x_sample.reshape(DEC_BATCH * DEC_SEQ, D_MODEL)
    new_k, new_v, new_ckv, new_kr, new_ssm = [], [], [], [], []
    h0_zero = jnp.zeros((BATCH // SUBLANES, 2, 2, SUBLANES, SSM_N), F32)

    row3 = lambda a: a.reshape(DEPTH, 1, -1)
    dup = lambda a: jnp.concatenate([a, a], axis=-1)
    lw = {
        'norm_mix': row3(norm_mix), 'norm_mlp': row3(norm_mlp),
        'w_in': _permute_w_in(w_in),
        'w_out': w_out.astype(BF16), 'w_ff1': w_ff1.astype(BF16), 'w_ff2': w_ff2.astype(BF16),
        'gqa_q_norm': row3(gqa_q_norm), 'gqa_k_norm': row3(gqa_k_norm), 'mla_kv_norm': row3(mla_kv_norm),
        'mla_q_nope_g': row3(mla_q_norm[:, :MLA_NOPE]),
        'mla_q_rope_g': row3(dup(mla_q_norm[:, MLA_NOPE:])),
        'mla_k_nope_g': row3(mla_k_norm[:, :MLA_NOPE]),
        'mla_k_rope_g': row3(dup(mla_k_norm[:, MLA_NOPE:])),
        'mla_k_rope_g64': row3(mla_k_norm[:, MLA_NOPE:]),
        'mla_w_uk': mla_w_uk.reshape(DEPTH, MLA_KV_RANK, MLA_HEADS * MLA_NOPE).astype(BF16),
        'mla_w_uv': mla_w_uv.reshape(DEPTH, MLA_KV_RANK, MLA_HEADS * MLA_V).astype(BF16),
        'ssm_d': row3(ssm_d), 'ssm_w_glu': ssm_w_glu.astype(BF16),
    }
    sp = _ssm_params(ssm_lam_re, ssm_lam_im, ssm_log_dt, ssm_b_re, ssm_b_im, ssm_c_re, ssm_c_im)
    h0_lat = state_ssm.reshape(DEC_BATCH, DEPTH, 2, SSM_N, 2).transpose(1, 2, 4, 0, 3)[:, None]

    for l in range(DEPTH):
        xp, (k, v, ckv_n, kr, h_t) = _trunk_layer(xp, mod, lw, sp, tabs, False, {'h0': h0_zero}, l)
        new_k.append(k.reshape(BATCH, SEQ, GQA_KV_HEADS, GQA_HEAD_DIM))
        new_v.append(v.reshape(BATCH, SEQ, GQA_KV_HEADS, GQA_HEAD_DIM))
        new_ckv.append(ckv_n.reshape(BATCH, SEQ, MLA_KV_RANK))
        new_kr.append(kr.reshape(BATCH, SEQ, MLA_ROPE))
        hs = h_t.transpose(0, 3, 1, 4, 2).reshape(BATCH, 2, SSM_GROUPS, SSM_STATE, 2)
        new_ssm.append(hs)

        ctx = {'k': cache_k, 'v': cache_v, 'ckv': cache_mla_ckv, 'kr': cache_mla_krope, 'h0': h0_lat[l]}
        xs, _ = _trunk_layer(xs, mod, lw, sp, tabs, True, ctx, l)

    return (xp.reshape(BATCH, SEQ, D_MODEL), xs.reshape(DEC_BATCH, DEC_SEQ, D_MODEL),
            jnp.stack(new_k, axis=1), jnp.stack(new_v, axis=1), jnp.stack(new_ckv, axis=1),
            jnp.stack(new_kr, axis=1), jnp.stack(new_ssm, axis=1))
```

```python
import functools
import math

import numpy as np
import jax
import jax.numpy as jnp
from jax import lax
from jax.experimental import pallas as pl
from jax.experimental.pallas import tpu as pltpu

D_MODEL = 2048
BATCH = 32
SEQ = 256
DEPTH = 2
DEC_BATCH = 8
DEC_SEQ = 1024
PAST_LEN = 512
GRID_W = 64
ROPE_BASE = 10000.0
EPS = 1e-6
SSM_CH = 512
SSM_GROUP = 16
SSM_GROUPS = SSM_CH // SSM_GROUP
SSM_STATE = 64
SSM_N = SSM_GROUPS * SSM_STATE
GQA_HEADS = 6
GQA_KV_HEADS = 2
GQA_GROUP = GQA_HEADS // GQA_KV_HEADS
GQA_HEAD_DIM = 128
MLA_HEADS = 6
MLA_NOPE = 128
MLA_ROPE = 64
MLA_QK = MLA_NOPE + MLA_ROPE
MLA_QK_PAD = 256
MLA_V = 128
MLA_KV_RANK = 512
D_FF = 4 * D_MODEL
N_MOD = 16

OFF_U = 0
OFF_QB = OFF_U + SSM_CH
OFF_KB = OFF_QB + GQA_HEADS * GQA_HEAD_DIM
OFF_VB = OFF_KB + GQA_KV_HEADS * GQA_HEAD_DIM
OFF_QCN = OFF_VB + GQA_KV_HEADS * GQA_HEAD_DIM
OFF_QCR = OFF_QCN + MLA_HEADS * MLA_NOPE
OFF_CKV = OFF_QCR + MLA_HEADS * MLA_ROPE
OFF_KR = OFF_CKV + MLA_KV_RANK
IN_WIDTH_P = OFF_KR + 2 * MLA_ROPE

LANES = 128
SUBLANES = 8
VMEM_LIMIT = 56 * 1024 * 1024

TM_PROJ = 512
TM_MLP = 1024
TF_MLP = 512
ROW_PIECES = 2

BF16 = jnp.bfloat16
F32 = jnp.float32


def _cparams(sem):
    return pltpu.CompilerParams(dimension_semantics=sem, vmem_limit_bytes=VMEM_LIMIT)


def _dot(a, b):
    return jnp.dot(a, b, preferred_element_type=F32)


def _dot_nt(a, b):
    return lax.dot_general(a, b, (((1,), (1,)), ((), ())), preferred_element_type=F32)


def _adaln_kernel(c_ref, w_ref, b_ref, o_ref):
    c = c_ref[...]
    s = (c * jax.nn.sigmoid(c)).astype(BF16)
    o_ref[...] = _dot(s, w_ref[...].astype(BF16)) + b_ref[...]


def _adaln(cvec, w_mod, b_mod):
    tn = 1024
    return pl.pallas_call(
        _adaln_kernel,
        out_shape=jax.ShapeDtypeStruct((DEPTH, N_MOD, 6 * D_MODEL), F32),
        grid=(DEPTH, 6 * D_MODEL // tn),
        in_specs=[
            pl.BlockSpec((N_MOD, D_MODEL), lambda l, j: (0, 0)),
            pl.BlockSpec((None, D_MODEL, tn), lambda l, j: (l, 0, j)),
            pl.BlockSpec((None, 1, tn), lambda l, j: (l, 0, j)),
        ],
        out_specs=pl.BlockSpec((None, N_MOD, tn), lambda l, j: (l, 0, j)),
        compiler_params=_cparams(("parallel", "parallel")),
        name="adaln",
    )(cvec, w_mod, b_mod)


def _swap_halves(x, block):
    lane = lax.broadcasted_iota(jnp.int32, x.shape, 1)
    first = (lane % (2 * block)) < block
    return jnp.where(first, pltpu.roll(x, LANES - block, 1), pltpu.roll(x, block, 1))


def _rope(x, cos, sin_signed, block):
    return x * cos + _swap_halves(x, block) * sin_signed


def _in_proj_kernel(latent, x_ref, mod_ref, gmix_ref, w_ref, gq_ref, gk_ref, gkv_ref, gqn_ref, gqr_ref, gkr_ref,
                    cosb_ref, sinb_ref, cosc_ref, sinc_ref,
                    u_ref, qb_ref, kb_ref, vb_ref, qc_ref, ckv_ref, kr_ref, krg_ref):
    x = x_ref[...]
    ms = jnp.mean(x * x, axis=-1, keepdims=True)
    y = x * lax.rsqrt(ms + EPS) * gmix_ref[...]
    h = (y * (1.0 + mod_ref[1:2, :]) + mod_ref[0:1, :]).astype(BF16)

    def proj(off, width):
        return _dot(h, w_ref[:, off:off + width])

    scale_b = 1.0 / math.sqrt(GQA_HEAD_DIM)
    zq = proj(OFF_QB, GQA_HEADS * GQA_HEAD_DIM)
    for hd in range(GQA_HEADS):
        col = zq[:, hd * LANES:(hd + 1) * LANES]
        q = col * lax.rsqrt(jnp.mean(col * col, axis=-1, keepdims=True) + EPS) * gq_ref[...]
        if latent:
            q = _rope(q, cosb_ref[...], sinb_ref[...], GQA_HEAD_DIM // 4)
        qb_ref[:, hd * LANES:(hd + 1) * LANES] = (q * scale_b).astype(qb_ref.dtype)
    zk = proj(OFF_KB, GQA_KV_HEADS * GQA_HEAD_DIM)
    for hd in range(GQA_KV_HEADS):
        col = zk[:, hd * LANES:(hd + 1) * LANES]
        k = col * lax.rsqrt(jnp.mean(col * col, axis=-1, keepdims=True) + EPS) * gk_ref[...]
        if latent:
            k = _rope(k, cosb_ref[...], sinb_ref[...], GQA_HEAD_DIM // 4)
        kb_ref[:, hd * LANES:(hd + 1) * LANES] = k.astype(kb_ref.dtype)

    scale_c = 1.0 / math.sqrt(MLA_QK)
    zn = proj(OFF_QCN, MLA_HEADS * MLA_NOPE)
    zr = proj(OFF_QCR, MLA_HEADS * MLA_ROPE)
    lane = lax.broadcasted_iota(jnp.int32, (x.shape[0], LANES), 1)
    low = lane < MLA_ROPE
    for pair in range(MLA_HEADS // 2):
        colr = zr[:, pair * LANES:(pair + 1) * LANES]
        sq = colr * colr
        ss_lo = jnp.sum(jnp.where(low, sq, 0.0), axis=-1, keepdims=True)
        ss_hi = jnp.sum(jnp.where(low, 0.0, sq), axis=-1, keepdims=True)
        rs = []
        for half, ss_r in ((0, ss_lo), (1, ss_hi)):
            hd = 2 * pair + half
            coln = zn[:, hd * LANES:(hd + 1) * LANES]
            ss = jnp.sum(coln * coln, axis=-1, keepdims=True) + ss_r
            r = lax.rsqrt(ss * (1.0 / MLA_QK) + EPS)
            rs.append(r)
            qc_ref[hd, :, :MLA_NOPE] = (coln * r * gqn_ref[...] * scale_c).astype(qc_ref.dtype)
        qr = colr * jnp.where(low, rs[0], rs[1]) * gqr_ref[...]
        if latent:
            qr = _rope(qr, cosc_ref[...], sinc_ref[...], MLA_ROPE // 4)
        qr = qr * scale_c
        zeros = jnp.zeros((x.shape[0], MLA_QK_PAD - MLA_QK), qc_ref.dtype)
        for half in range(2):
            hd = 2 * pair + half
            qc_ref[hd, :, MLA_NOPE:MLA_QK] = qr[:, half * MLA_ROPE:(half + 1) * MLA_ROPE].astype(qc_ref.dtype)
            qc_ref[hd, :, MLA_QK:] = zeros

    zc = proj(OFF_CKV, MLA_KV_RANK)
    ckv_ref[...] = (zc * lax.rsqrt(jnp.mean(zc * zc, axis=-1, keepdims=True) + EPS) * gkv_ref[...]).astype(ckv_ref.dtype)
    zkr = proj(OFF_KR, 2 * MLA_ROPE)
    kr_ref[...] = zkr[:, :MLA_ROPE]
    krg = zkr * gkr_ref[...]
    if latent:
        krg = _rope(krg, cosc_ref[...], sinc_ref[...], MLA_ROPE // 4)
    krg_ref[...] = krg[:, :MLA_ROPE]

    vb_ref[...] = proj(OFF_VB, GQA_KV_HEADS * GQA_HEAD_DIM).astype(vb_ref.dtype)
    zu = proj(OFF_U, SSM_CH)
    for q in range(SSM_CH // LANES):
        u_ref[q] = zu[:, q * LANES:(q + 1) * LANES]


def _lspec(block_tail, layer):
    zeros = (0,) * len(block_tail)
    return pl.BlockSpec((None,) + tuple(block_tail), lambda *_: (layer,) + zeros)


def _mod_spec(latent, tm, layer):
    tiles_per_seq = DEC_SEQ // tm
    if latent:
        return pl.BlockSpec((None, None, 6, D_MODEL), lambda i, *_: (layer, 1 + i // tiles_per_seq, 0, 0))
    return pl.BlockSpec((None, None, 6, D_MODEL), lambda i, *_: (layer, 0, 0, 0))


def _in_proj(x2d, mod, latent, lw, tabs, tm, layer):
    rows = x2d.shape[0]
    n_tiles = rows // tm
    tiles_per_seq = DEC_SEQ // tm
    if latent:
        tab_map = lambda i: (i % tiles_per_seq, 0)
    else:
        tab_map = lambda i: (0, 0)
    row = lambda i: (i, 0)
    act_dt = BF16 if latent else F32
    in_specs = [
        pl.BlockSpec((tm, D_MODEL), row),
        _mod_spec(latent, tm, layer),
        _lspec((1, D_MODEL), layer),
        _lspec((D_MODEL, IN_WIDTH_P), layer),
        _lspec((1, LANES), layer),
        _lspec((1, LANES), layer),
        _lspec((1, MLA_KV_RANK), layer),
        _lspec((1, LANES), layer),
        _lspec((1, LANES), layer),
        _lspec((1, LANES), layer),
        pl.BlockSpec((tm, LANES), tab_map),
        pl.BlockSpec((tm, LANES), tab_map),
        pl.BlockSpec((tm, LANES), tab_map),
        pl.BlockSpec((tm, LANES), tab_map),
    ]
    out_shape = (
        jax.ShapeDtypeStruct((SSM_CH // LANES, rows, LANES), F32),
        jax.ShapeDtypeStruct((rows, GQA_HEADS * GQA_HEAD_DIM), BF16),
        jax.ShapeDtypeStruct((rows, GQA_KV_HEADS * GQA_HEAD_DIM), act_dt),
        jax.ShapeDtypeStruct((rows, GQA_KV_HEADS * GQA_HEAD_DIM), act_dt),
        jax.ShapeDtypeStruct((MLA_HEADS, rows, MLA_QK_PAD), BF16),
        jax.ShapeDtypeStruct((rows, MLA_KV_RANK), act_dt),
        jax.ShapeDtypeStruct((rows, MLA_ROPE), F32),
        jax.ShapeDtypeStruct((rows, MLA_ROPE), F32),
    )
    out_specs = (
        pl.BlockSpec((SSM_CH // LANES, tm, LANES), lambda i: (0, i, 0)),
        pl.BlockSpec((tm, GQA_HEADS * GQA_HEAD_DIM), row),
        pl.BlockSpec((tm, GQA_KV_HEADS * GQA_HEAD_DIM), row),
        pl.BlockSpec((tm, GQA_KV_HEADS * GQA_HEAD_DIM), row),
        pl.BlockSpec((MLA_HEADS, tm, MLA_QK_PAD), lambda i: (0, i, 0)),
        pl.BlockSpec((tm, MLA_KV_RANK), row),
        pl.BlockSpec((tm, MLA_ROPE), row),
        pl.BlockSpec((tm, MLA_ROPE), row),
    )
    return pl.pallas_call(
        functools.partial(_in_proj_kernel, latent),
        out_shape=out_shape,
        grid=(n_tiles,),
        in_specs=in_specs,
        out_specs=out_specs,
        compiler_params=_cparams(("parallel",)),
        name="in_proj_lat" if latent else "in_proj_ctx",
    )(x2d, mod, lw['norm_mix'], lw['w_in'], lw['gqa_q_norm'], lw['gqa_k_norm'], lw['mla_kv_norm'],
      lw['mla_q_nope_g'], lw['mla_q_rope_g'], lw['mla_k_rope_g'],
      tabs['cos_b'], tabs['sin_b'], tabs['cos_c'], tabs['sin_c'])


SSM_L = 16
SSM_ROWS = 1024
SSM_NCH = SSM_ROWS // SSM_L
SSM_PITCH = SSM_NCH + SUBLANES
SSM_W = SSM_L * SSM_GROUP
N_SLAB = 2 * SSM_GROUPS


def _ssm_kernel(n_seq, u_ref, h0_ref, t_ref, f_ref, e_ref, a_ref, wglu_ref, o_ref, hT_ref,
                ug_s, y_s, s_s, hs_s, yt_s, ytok_s):
    nch = SSM_NCH
    jn = nch // n_seq
    qn = SSM_CH // LANES
    gq = LANES // SSM_GROUP

    for s in range(SSM_L):
        for q in range(qn):
            us = u_ref.at[q][pl.ds(s, nch, stride=SSM_L), :]
            for gl in range(gq):
                ug_s[q * gq + gl, :, s * SSM_GROUP:(s + 1) * SSM_GROUP] = us[:, gl * SSM_GROUP:(gl + 1) * SSM_GROUP]

    for g in range(SSM_GROUPS):
        ug = ug_s[g].astype(BF16)
        y_s[g] = _dot(ug, t_ref[g])
        sg = _dot(ug, f_ref[g])
        s_s[pl.ds(g * SSM_PITCH, nch), :] = sg[:, :LANES]
        s_s[pl.ds((SSM_GROUPS + g) * SSM_PITCH, nch), :] = sg[:, LANES:]

    n_sg = N_SLAB // SUBLANES
    for sq in range(n_seq):
        def step(jj, hs, sq=sq):
            out = []
            for sg_i in range(n_sg):
                backward = sg_i >= n_sg // 2
                j = (jn - 1 - jj) if backward else jj
                rows = pl.ds(sg_i * SUBLANES * SSM_PITCH + sq * jn + j, SUBLANES, stride=SSM_PITCH)
                h = hs[sg_i]
                hs_s[rows, :] = h
                out.append(a_ref[0, sg_i] * h + a_ref[1, sg_i] * pltpu.roll(h, SSM_STATE, 1) + s_s[rows, :])
            return tuple(out)

        h_init = tuple(h0_ref[sq, sg_i * SUBLANES:(sg_i + 1) * SUBLANES, :] for sg_i in range(n_sg))
        h_fin = lax.fori_loop(0, jn, step, h_init)
        for sg_i in range(n_sg):
            hT_ref[sq, sg_i * SUBLANES:(sg_i + 1) * SUBLANES, :] = h_fin[sg_i]

    for g in range(SSM_GROUPS):
        hg = jnp.concatenate([hs_s[pl.ds(g * SSM_PITCH, nch), :],
                              hs_s[pl.ds((SSM_GROUPS + g) * SSM_PITCH, nch), :]], axis=-1).astype(BF16)
        yg = y_s[g] + _dot(hg, e_ref[g])
        q, gl = divmod(g, gq)
        for t in range(SSM_L):
            yt_s[q, t, :, gl * SSM_GROUP:(gl + 1) * SSM_GROUP] = yg[:, t * SSM_GROUP:(t + 1) * SSM_GROUP]
    for q in range(qn):
        for t in range(SSM_L):
            ytok_s.at[q][pl.ds(t, nch, stride=SSM_L), :] = yt_s[q, t]

    ycat = jnp.concatenate([ytok_s[q] for q in range(qn)], axis=-1).astype(BF16)
    zg = _dot(ycat, wglu_ref[...])
    o_ref[...] = (zg[:, :SSM_CH] * jax.nn.sigmoid(zg[:, SSM_CH:])).astype(o_ref.dtype)


def _ssm(u_slab, h0, sp, w_glu, seq, layer):
    rows = u_slab.shape[1]
    n_seq = SSM_ROWS // seq
    steps = rows // SSM_ROWS
    return pl.pallas_call(
        functools.partial(_ssm_kernel, n_seq),
        out_shape=(jax.ShapeDtypeStruct((rows, SSM_CH), BF16),
                   jax.ShapeDtypeStruct((steps * n_seq, N_SLAB, LANES), F32)),
        grid=(steps,),
        in_specs=[
            pl.BlockSpec((SSM_CH // LANES, SSM_ROWS, LANES), lambda i: (0, i, 0)),
            pl.BlockSpec((n_seq, N_SLAB, LANES), lambda i: (i, 0, 0)),
            _lspec((SSM_GROUPS, SSM_W, SSM_W), layer),
            _lspec((SSM_GROUPS, SSM_W, 4 * SSM_STATE), layer),
            _lspec((SSM_GROUPS, 4 * SSM_STATE, SSM_W), layer),
            _lspec((2, N_SLAB // SUBLANES, SUBLANES, LANES), layer),
            _lspec((SSM_CH, 2 * SSM_CH), layer),
        ],
        out_specs=(pl.BlockSpec((SSM_ROWS, SSM_CH), lambda i: (i, 0)),
                   pl.BlockSpec((n_seq, N_SLAB, LANES), lambda i: (i, 0, 0))),
        scratch_shapes=[
            pltpu.VMEM((SSM_GROUPS, SSM_NCH, SSM_W), F32),
            pltpu.VMEM((SSM_GROUPS, SSM_NCH, SSM_W), F32),
            pltpu.VMEM((N_SLAB * SSM_PITCH, LANES), F32),
            pltpu.VMEM((N_SLAB * SSM_PITCH, LANES), F32),
            pltpu.VMEM((SSM_CH // LANES, SSM_L, SSM_NCH, LANES), F32),
            pltpu.VMEM((SSM_CH // LANES, SSM_ROWS, LANES), F32),
        ],
        compiler_params=_cparams(("parallel",)),
        name="ssm",
    )(u_slab, h0, sp['t'], sp['f'], sp['e'], sp['a'], w_glu)


def _ones_column(n):
    return jnp.ones((n, LANES), BF16)


def _softmax_pv(scores, v_ext):
    m = jnp.max(scores, axis=-1, keepdims=True)
    p = jnp.exp((scores - m).astype(BF16))
    o = _dot(p, v_ext)
    return o[:, :LANES] / o[:, LANES:]


ATTN_TQ = 256


def _for_each_q_block(nb, seq, body):
    nq = seq // ATTN_TQ
    for s in range(nb):
        for qi in range(nq):
            body(s, s * seq + qi * ATTN_TQ)


def _gqa_kernel(latent, nb, seq, *refs):
    if latent:
        q_ref, k_ref, v_ref, kc_ref, vc_ref, o_ref, k_s, v_s = refs
    else:
        q_ref, k_ref, v_ref, o_ref, k_s, v_s = refs
    tq = ATTN_TQ
    for s in range(nb):
        for h in range(GQA_KV_HEADS):
            hl = slice(h * LANES, (h + 1) * LANES)
            k_s[s, h, 0:seq, :] = k_ref[s * seq:(s + 1) * seq, hl].astype(BF16)
            v_s[s, h, 0:seq, :LANES] = v_ref[s * seq:(s + 1) * seq, hl].astype(BF16)
            if latent:
                k_s[s, h, seq:, :] = kc_ref[:, hl].astype(BF16)
                v_s[s, h, seq:, :LANES] = vc_ref[:, hl].astype(BF16)
            v_s[s, h, :, LANES:] = _ones_column(v_s.shape[2])

    def body(s, r0):
        for h in range(GQA_KV_HEADS):
            heads = [h * GQA_GROUP + g for g in range(GQA_GROUP)]
            q3 = jnp.concatenate([q_ref[pl.ds(r0, tq), hd * LANES:(hd + 1) * LANES] for hd in heads], axis=0)
            o = _softmax_pv(_dot_nt(q3, k_s[s, h]), v_s[s, h])
            for g, hd in enumerate(heads):
                o_ref[pl.ds(r0, tq), hd * LANES:(hd + 1) * LANES] = o[g * tq:(g + 1) * tq].astype(o_ref.dtype)

    _for_each_q_block(nb, seq, body)


def _gqa_attn(qb, kb, vb, latent, cache_k=None, cache_v=None, layer=0):
    rows = qb.shape[0]
    seq = DEC_SEQ if latent else SEQ
    nb = 1 if latent else 4
    t_all = seq + (PAST_LEN if latent else 0)
    qw = GQA_HEADS * GQA_HEAD_DIM
    kw = GQA_KV_HEADS * GQA_HEAD_DIM
    row = lambda i: (i, 0)
    in_specs = [
        pl.BlockSpec((nb * seq, qw), row),
        pl.BlockSpec((nb * seq, kw), row),
        pl.BlockSpec((nb * seq, kw), row),
    ]
    args = [qb, kb, vb]
    if latent:
        cspec = pl.BlockSpec((None, None, PAST_LEN, kw), lambda i: (i, layer, 0, 0))
        in_specs += [cspec, cspec]
        args += [cache_k, cache_v]
    return pl.pallas_call(
        functools.partial(_gqa_kernel, latent, nb, seq),
        out_shape=jax.ShapeDtypeStruct((rows, qw), BF16),
        grid=(rows // (nb * seq),),
        in_specs=in_specs,
        out_specs=pl.BlockSpec((nb * seq, qw), row),
        scratch_shapes=[
            pltpu.VMEM((nb, GQA_KV_HEADS, t_all, GQA_HEAD_DIM), BF16),
            pltpu.VMEM((nb, GQA_KV_HEADS, t_all, 2 * LANES), BF16),
        ],
        compiler_params=_cparams(("parallel",)),
        name="gqa_lat" if latent else "gqa_ctx",
    )(*args)


def _mla_kernel(latent, nb, seq, *refs):
    if latent:
        (q_ref, ckv_ref, kr_ref, krg_ref, ckvc_ref, krc_ref, wuk_ref, wuv_ref, gn_ref, gr_ref,
         o_ref, k_s, v_s) = refs
    else:
        (q_ref, ckv_ref, kr_ref, krg_ref, wuk_ref, wuv_ref, gn_ref, gr_ref, o_ref, k_s, v_s) = refs
    tq = ATTN_TQ

    def expand(ckv, kr_raw, krg, place):
        c = ckv.astype(BF16)
        kn_all = _dot(c, wuk_ref[...])
        v_all = _dot(c, wuv_ref[...])
        ss_kr = jnp.sum(kr_raw * kr_raw, axis=-1, keepdims=True)
        for h in range(MLA_HEADS):
            hl = slice(h * LANES, (h + 1) * LANES)
            kn = kn_all[:, hl]
            rs = lax.rsqrt((jnp.sum(kn * kn, axis=-1, keepdims=True) + ss_kr) * (1.0 / MLA_QK) + EPS)
            kn = (kn * rs * gn_ref[...]).astype(BF16)
            kp = (krg * rs).astype(BF16)
            vv = v_all[:, hl].astype(BF16)
            for s, lo, n, rows in place:
                k_s[s, h, lo:lo + n, :MLA_NOPE] = kn[rows]
                k_s[s, h, lo:lo + n, MLA_NOPE:MLA_QK] = kp[rows]
                k_s[s, h, lo:lo + n, MLA_QK:] = jnp.zeros((n, MLA_QK_PAD - MLA_QK), BF16)
                v_s[s, h, lo:lo + n, :MLA_V] = vv[rows]
                v_s[s, h, lo:lo + n, MLA_V:] = _ones_column(n)

    expand(ckv_ref[...], kr_ref[...], krg_ref[...],
           [(s, 0, seq, slice(s * seq, (s + 1) * seq)) for s in range(nb)])
    if latent:
        krc = krc_ref[...]
        expand(ckvc_ref[...], krc, krc * gr_ref[...], [(0, seq, PAST_LEN, slice(0, PAST_LEN))])

    def body(s, r0):
        for h in range(MLA_HEADS):
            sc = _dot_nt(q_ref[h, pl.ds(r0, tq), :], k_s[s, h])
            o_ref[pl.ds(r0, tq), h * LANES:(h + 1) * LANES] = _softmax_pv(sc, v_s[s, h]).astype(o_ref.dtype)

    _for_each_q_block(nb, seq, body)


def _mla_attn(qc, ckv, kr, krg, lw, latent, cache_ckv=None, cache_kr=None, layer=0):
    rows = qc.shape[1]
    seq = DEC_SEQ if latent else SEQ
    nb = 1 if latent else 4
    t_all = seq + (PAST_LEN if latent else 0)
    row = lambda i: (i, 0)
    full = lambda i: (0, 0)
    in_specs = [
        pl.BlockSpec((MLA_HEADS, nb * seq, MLA_QK_PAD), lambda i: (0, i, 0)),
        pl.BlockSpec((nb * seq, MLA_KV_RANK), row),
        pl.BlockSpec((nb * seq, MLA_ROPE), row),
        pl.BlockSpec((nb * seq, MLA_ROPE), row),
    ]
    args = [qc, ckv, kr, krg]
    if latent:
        in_specs += [
            pl.BlockSpec((None, None, PAST_LEN, MLA_KV_RANK), lambda i: (i, layer, 0, 0)),
            pl.BlockSpec((None, None, PAST_LEN, MLA_ROPE), lambda i: (i, layer, 0, 0)),
        ]
        args += [cache_ckv, cache_kr]
    in_specs += [
        _lspec((MLA_KV_RANK, MLA_HEADS * MLA_NOPE), layer),
        _lspec((MLA_KV_RANK, MLA_HEADS * MLA_V), layer),
        _lspec((1, MLA_NOPE), layer),
        _lspec((1, MLA_ROPE), layer),
    ]
    args += [lw['mla_w_uk'], lw['mla_w_uv'], lw['mla_k_nope_g'], lw['mla_k_rope_g64']]
    return pl.pallas_call(
        functools.partial(_mla_kernel, latent, nb, seq),
        out_shape=jax.ShapeDtypeStruct((rows, MLA_HEADS * MLA_V), BF16),
        grid=(rows // (nb * seq),),
        in_specs=in_specs,
        out_specs=pl.BlockSpec((nb * seq, MLA_HEADS * MLA_V), row),
        scratch_shapes=[
            pltpu.VMEM((nb, MLA_HEADS, t_all, MLA_QK_PAD), BF16),
            pltpu.VMEM((nb, MLA_HEADS, t_all, MLA_V + LANES), BF16),
        ],
        compiler_params=_cparams(("parallel",)),
        name="mla_lat" if latent else "mla_ctx",
    )(*args)


def _out_proj_kernel(ya_ref, yb_ref, yc_ref, x_ref, mod_ref, w_ref, g_ref, x1_ref, h2_ref):
    wa = SSM_CH
    wb = wa + GQA_HEADS * GQA_HEAD_DIM
    piece = x_ref.shape[0] // ROW_PIECES
    pieces = [slice(r * piece, (r + 1) * piece) for r in range(ROW_PIECES)]
    outs = [(_dot(ya_ref[rs, :], w_ref[0:wa, :]) + _dot(yb_ref[rs, :], w_ref[wa:wb, :])
             + _dot(yc_ref[rs, :], w_ref[wb:, :])) for rs in pieces]
    for rs, o in zip(pieces, outs):
        x1 = x_ref[rs, :] + mod_ref[2:3, :] * o
        x1_ref[rs, :] = x1
        ms = jnp.mean(x1 * x1, axis=-1, keepdims=True)
        y = x1 * lax.rsqrt(ms + EPS) * g_ref[...]
        h2_ref[rs, :] = (y * (1.0 + mod_ref[4:5, :]) + mod_ref[3:4, :]).astype(h2_ref.dtype)


def _out_proj(ya, yb, yc, x2d, mod, lw, latent, tm, layer):
    rows = x2d.shape[0]
    row = lambda i: (i, 0)
    return pl.pallas_call(
        _out_proj_kernel,
        out_shape=(jax.ShapeDtypeStruct((rows, D_MODEL), F32), jax.ShapeDtypeStruct((rows, D_MODEL), BF16)),
        grid=(rows // tm,),
        in_specs=[
            pl.BlockSpec((tm, SSM_CH), row),
            pl.BlockSpec((tm, GQA_HEADS * GQA_HEAD_DIM), row),
            pl.BlockSpec((tm, MLA_HEADS * MLA_V), row),
            pl.BlockSpec((tm, D_MODEL), row),
            _mod_spec(latent, tm, layer),
            _lspec((D_MODEL, D_MODEL), layer),
            _lspec((1, D_MODEL), layer),
        ],
        out_specs=(pl.BlockSpec((tm, D_MODEL), row), pl.BlockSpec((tm, D_MODEL), row)),
        compiler_params=_cparams(("parallel",)),
        name="out_proj",
    )(ya, yb, yc, x2d, mod, lw['w_out'], lw['norm_mlp'])


def _mlp_kernel(h2_ref, x1_ref, mod_ref, w1_ref, w2_ref, o_ref):
    j = pl.program_id(1)

    @pl.when(j == 0)
    def _():
        o_ref[...] = jnp.zeros_like(o_ref)

    f = jnp.maximum(_dot(h2_ref[...], w1_ref[...]), 0.0)
    o_ref[...] += _dot((f * f).astype(BF16), w2_ref[...])

    @pl.when(j == pl.num_programs(1) - 1)
    def _():
        o_ref[...] = x1_ref[...] + mod_ref[5:6, :] * o_ref[...]


def _mlp(h2, x1, mod, lw, latent, tm, tf, layer):
    rows = h2.shape[0]
    return pl.pallas_call(
        _mlp_kernel,
        out_shape=jax.ShapeDtypeStruct((rows, D_MODEL), F32),
        grid=(rows // tm, D_FF // tf),
        in_specs=[
            pl.BlockSpec((tm, D_MODEL), lambda i, j: (i, 0)),
            pl.BlockSpec((tm, D_MODEL), lambda i, j: (i, 0), pipeline_mode=pl.Buffered(1)),
            _mod_spec(latent, tm, layer),
            pl.BlockSpec((None, D_MODEL, tf), lambda i, j: (layer, 0, j)),
            pl.BlockSpec((None, tf, D_MODEL), lambda i, j: (layer, j, 0)),
        ],
        out_specs=pl.BlockSpec((tm, D_MODEL), lambda i, j: (i, 0)),
        compiler_params=_cparams(("parallel", "arbitrary")),
        name="mlp",
    )(h2, x1, mod, lw['w_ff1'], lw['w_ff2'])


def _permute_w_in(w):
    base = OFF_QCN
    ckv0 = base + MLA_HEADS * MLA_QK
    qc = w[:, :, base:ckv0].reshape(DEPTH, D_MODEL, MLA_HEADS, MLA_QK)
    nope = qc[..., :MLA_NOPE].reshape(DEPTH, D_MODEL, MLA_HEADS * MLA_NOPE)
    rope = qc[..., MLA_NOPE:].reshape(DEPTH, D_MODEL, MLA_HEADS * MLA_ROPE)
    kr = w[:, :, ckv0 + MLA_KV_RANK:]
    parts = [w[:, :, :base], nope, rope, w[:, :, ckv0:ckv0 + MLA_KV_RANK], kr, kr]
    return jnp.concatenate(parts, axis=2).astype(BF16)


def _rope_tables(seq):
    t = jnp.arange(seq)
    row = (t // GRID_W).astype(F32)
    col = (t % GRID_W).astype(F32)

    def table(d):
        quarter = d // 4
        inv = ROPE_BASE ** (-(jnp.arange(quarter, dtype=F32) / quarter))
        ar = row[:, None] * inv[None, :]
        ac = col[:, None] * inv[None, :]
        cos = jnp.concatenate([jnp.cos(ar), jnp.cos(ar), jnp.cos(ac), jnp.cos(ac)], axis=-1)
        sin = jnp.concatenate([-jnp.sin(ar), jnp.sin(ar), -jnp.sin(ac), jnp.sin(ac)], axis=-1)
        reps = LANES // d
        return jnp.tile(cos, (1, reps)), jnp.tile(sin, (1, reps))

    cos_b, sin_b = table(GQA_HEAD_DIM)
    cos_c, sin_c = table(MLA_ROPE)
    return {'cos_b': cos_b, 'sin_b': sin_b, 'cos_c': cos_c, 'sin_c': sin_c}


def _ssm_params(lam_re, lam_im, log_dt, b_re, b_im, c_re, c_im, d_skip):
    hi = lax.Precision.HIGHEST
    a = lam_re.astype(F32)
    w = lam_im.astype(F32)
    dt = jnp.exp(log_dt.astype(F32))[..., None]
    k = jnp.arange(SSM_L + 1, dtype=F32)
    mag = jnp.exp(a[..., None] * dt[..., None] * k)
    ang = w[..., None] * dt[..., None] * k
    pr = mag * jnp.cos(ang)
    pi = mag * jnp.sin(ang)
    lbr, lbi = pr[..., 1], pi[..., 1]
    den = a * a + w * w
    cr = (((lbr - 1.0) * a + lbi * w) / den)[..., None]
    ci = ((lbi * a - (lbr - 1.0) * w) / den)[..., None]
    bre = b_re.astype(F32)
    bim = b_im.astype(F32)
    bb_re = cr * bre - ci * bim
    bb_im = cr * bim + ci * bre
    cre = c_re.astype(F32)
    cim = c_im.astype(F32)
    mr = cre[..., None] * pr[:, :, :, None] - cim[..., None] * pi[:, :, :, None]
    mi = cre[..., None] * pi[:, :, :, None] + cim[..., None] * pr[:, :, :, None]
    kk = (jnp.einsum('ldgcpk,ldgpe->ldgkce', mr, bb_re, precision=hi)
          - jnp.einsum('ldgcpk,ldgpe->ldgkce', mi, bb_im, precision=hi))
    lag = np.arange(SSM_L)[None, :] - np.arange(SSM_L)[:, None]
    kf = kk[:, 0][:, :, np.clip(lag, 0, SSM_L - 1)] * jnp.asarray(lag >= 0, F32)[:, :, None, None]
    kb = kk[:, 1][:, :, np.clip(-lag, 0, SSM_L - 1)] * jnp.asarray(lag <= 0, F32)[:, :, None, None]
    dd = d_skip.astype(F32).reshape(DEPTH, SSM_GROUPS, SSM_GROUP)
    skip = (jnp.eye(SSM_L, dtype=F32)[:, :, None, None] * jnp.eye(SSM_GROUP, dtype=F32)[None, None]
            * dd[:, :, None, None, :, None])
    t_mat = (kf + kb + skip).transpose(0, 1, 2, 5, 3, 4).reshape(DEPTH, SSM_GROUPS, SSM_W, SSM_W)

    def f_cols(d, powers):
        qr = pr[:, d][..., powers][..., None]
        qi = pi[:, d][..., powers][..., None]
        br = bb_re[:, d][:, :, :, None, :]
        bi = bb_im[:, d][:, :, :, None, :]
        return (qr * br - qi * bi).transpose(0, 1, 3, 4, 2), (qr * bi + qi * br).transpose(0, 1, 3, 4, 2)

    ffr, ffi = f_cols(0, np.arange(SSM_L - 1, -1, -1))
    fbr, fbi = f_cols(1, np.arange(SSM_L))
    f_mat = jnp.concatenate([ffr, ffi, fbr, fbi], axis=-1).reshape(DEPTH, SSM_GROUPS, SSM_W, 4 * SSM_STATE)

    def e_rows(d, powers):
        return (mr[:, d][..., powers].transpose(0, 1, 3, 4, 2), -mi[:, d][..., powers].transpose(0, 1, 3, 4, 2))

    efr, efi = e_rows(0, np.arange(1, SSM_L + 1))
    ebr, ebi = e_rows(1, np.arange(SSM_L, 0, -1))
    e_mat = jnp.concatenate([efr, efi, ebr, ebi], axis=2).reshape(DEPTH, SSM_GROUPS, 4 * SSM_STATE, SSM_W)

    ar = pr[..., SSM_L].reshape(DEPTH, N_SLAB, SSM_STATE)
    ai = pi[..., SSM_L].reshape(DEPTH, N_SLAB, SSM_STATE)
    a1 = jnp.concatenate([ar, ar], axis=-1)
    a2 = jnp.concatenate([-ai, ai], axis=-1)
    a_mat = jnp.stack([a1, a2], axis=1).reshape(DEPTH, 2, N_SLAB // SUBLANES, SUBLANES, LANES)
    return {'t': t_mat.astype(BF16), 'f': f_mat.astype(BF16), 'e': e_mat.astype(BF16), 'a': a_mat}


def _trunk_layer(x2d, mod, lw, sp, tabs, latent, ctx, layer):
    seq = DEC_SEQ if latent else SEQ
    u, qb, kb, vb, qc, ckv, kr, krg = _in_proj(x2d, mod, latent, lw, tabs, TM_PROJ, layer)

    ya, h_t = _ssm(u, ctx['h0'], sp, lw['ssm_w_glu'], seq, layer)

    if latent:
        yb = _gqa_attn(qb, kb, vb, True, ctx['k'], ctx['v'], layer)
        yc = _mla_attn(qc, ckv, kr, krg, lw, True, ctx['ckv'], ctx['kr'], layer)
    else:
        yb = _gqa_attn(qb, kb, vb, False)
        yc = _mla_attn(qc, ckv, kr, krg, lw, False, layer=layer)

    x1, h2 = _out_proj(ya, yb, yc, x2d, mod, lw, latent, TM_PROJ, layer)
    x2 = _mlp(h2, x1, mod, lw, latent, TM_MLP, TF_MLP, layer)
    return x2, (kb, vb, ckv, kr, h_t)


def kernel(x_prompt, x_sample, cache_attn_k, cache_attn_v, cache_mla_ckv, cache_mla_krope, state_ssm, c, c_ctx, w_mod, b_mod, norm_mix, norm_mlp, w_in, gqa_q_norm, gqa_k_norm, mla_kv_norm, mla_q_norm, mla_k_norm, mla_w_uk, mla_w_uv, ssm_lam_re, ssm_lam_im, ssm_log_dt, ssm_b_re, ssm_b_im, ssm_c_re, ssm_c_im, ssm_d, ssm_w_glu, w_out, w_ff1, w_ff2):
    cvec = jnp.zeros((N_MOD, D_MODEL), F32).at[0].set(c_ctx).at[1:1 + DEC_BATCH].set(c)
    mod = _adaln(cvec, w_mod, b_mod.reshape(DEPTH, 1, 6 * D_MODEL)).reshape(DEPTH, N_MOD, 6, D_MODEL)

    tabs = _rope_tables(DEC_SEQ)
    cache_k = cache_attn_k.reshape(DEC_BATCH, DEPTH, PAST_LEN, GQA_KV_HEADS * GQA_HEAD_DIM)
    cache_v = cache_attn_v.reshape(DEC_BATCH, DEPTH, PAST_LEN, GQA_KV_HEADS * GQA_HEAD_DIM)

    xp = x_prompt.reshape(BATCH * SEQ, D_MODEL)
    xs = x_sample.reshape(DEC_BATCH * DEC_SEQ, D_MODEL)
    new_k, new_v, new_ckv, new_kr, new_ssm = [], [], [], [], []
    h0_zero = jnp.zeros((BATCH, N_SLAB, LANES), F32)

    row3 = lambda a: a.reshape(DEPTH, 1, -1)
    dup = lambda a: jnp.concatenate([a, a], axis=-1)
    lw = {
        'norm_mix': row3(norm_mix), 'norm_mlp': row3(norm_mlp),
        'w_in': _permute_w_in(w_in),
        'w_out': w_out.astype(BF16), 'w_ff1': w_ff1.astype(BF16), 'w_ff2': w_ff2.astype(BF16),
        'gqa_q_norm': row3(gqa_q_norm), 'gqa_k_norm': row3(gqa_k_norm), 'mla_kv_norm': row3(mla_kv_norm),
        'mla_q_nope_g': row3(mla_q_norm[:, :MLA_NOPE]),
        'mla_q_rope_g': row3(dup(mla_q_norm[:, MLA_NOPE:])),
        'mla_k_nope_g': row3(mla_k_norm[:, :MLA_NOPE]),
        'mla_k_rope_g': row3(dup(mla_k_norm[:, MLA_NOPE:])),
        'mla_k_rope_g64': row3(mla_k_norm[:, MLA_NOPE:]),
        'mla_w_uk': mla_w_uk.reshape(DEPTH, MLA_KV_RANK, MLA_HEADS * MLA_NOPE).astype(BF16),
        'mla_w_uv': mla_w_uv.reshape(DEPTH, MLA_KV_RANK, MLA_HEADS * MLA_V).astype(BF16),
        'ssm_w_glu': ssm_w_glu.astype(BF16),
    }
    sp = _ssm_params(ssm_lam_re, ssm_lam_im, ssm_log_dt, ssm_b_re, ssm_b_im, ssm_c_re, ssm_c_im, ssm_d)
    h0_lat = state_ssm.transpose(1, 0, 2, 3, 5, 4).reshape(DEPTH, DEC_BATCH, N_SLAB, LANES)

    for l in range(DEPTH):
        xp, (k, v, ckv_n, kr, h_t) = _trunk_layer(xp, mod, lw, sp, tabs, False, {'h0': h0_zero}, l)
        new_k.append(k.reshape(BATCH, SEQ, GQA_KV_HEADS, GQA_HEAD_DIM))
        new_v.append(v.reshape(BATCH, SEQ, GQA_KV_HEADS, GQA_HEAD_DIM))
        new_ckv.append(ckv_n.reshape(BATCH, SEQ, MLA_KV_RANK))
        new_kr.append(kr.reshape(BATCH, SEQ, MLA_ROPE))
        new_ssm.append(h_t.reshape(BATCH, 2, SSM_GROUPS, 2, SSM_STATE).transpose(0, 1, 2, 4, 3))

        ctx = {'k': cache_k, 'v': cache_v, 'ckv': cache_mla_ckv, 'kr': cache_mla_krope, 'h0': h0_lat[l]}
        xs, _ = _trunk_layer(xs, mod, lw, sp, tabs, True, ctx, l)

    return (xp.reshape(BATCH, SEQ, D_MODEL), xs.reshape(DEC_BATCH, DEC_SEQ, D_MODEL),
            jnp.stack(new_k, axis=1), jnp.stack(new_v, axis=1), jnp.stack(new_ckv, axis=1),
            jnp.stack(new_kr, axis=1), jnp.stack(new_ssm, axis=1))
```

```python
import functools
import math

import jax
import jax.numpy as jnp
from jax import lax
from jax.experimental import pallas as pl
from jax.experimental.pallas import tpu as pltpu

D_MODEL = 2048
BATCH = 32
SEQ = 256
DEPTH = 2
DEC_BATCH = 8
DEC_SEQ = 1024
PAST_LEN = 512
GRID_W = 64
ROPE_BASE = 10000.0
EPS = 1e-6
SSM_CH = 512
SSM_GROUP = 16
SSM_GROUPS = SSM_CH // SSM_GROUP
SSM_STATE = 64
SSM_N = SSM_GROUPS * SSM_STATE
GQA_HEADS = 6
GQA_KV_HEADS = 2
GQA_GROUP = GQA_HEADS // GQA_KV_HEADS
GQA_HEAD_DIM = 128
MLA_HEADS = 6
MLA_NOPE = 128
MLA_ROPE = 64
MLA_QK = MLA_NOPE + MLA_ROPE
MLA_QK_PAD = 256
MLA_V = 128
MLA_KV_RANK = 512
D_FF = 4 * D_MODEL
N_MOD = 16

OFF_U = 0
OFF_QB = OFF_U + SSM_CH
OFF_KB = OFF_QB + GQA_HEADS * GQA_HEAD_DIM
OFF_VB = OFF_KB + GQA_KV_HEADS * GQA_HEAD_DIM
OFF_QCN = OFF_VB + GQA_KV_HEADS * GQA_HEAD_DIM
OFF_QCR = OFF_QCN + MLA_HEADS * MLA_NOPE
OFF_CKV = OFF_QCR + MLA_HEADS * MLA_ROPE
OFF_KR = OFF_CKV + MLA_KV_RANK
IN_WIDTH_P = OFF_KR + 2 * MLA_ROPE

LANES = 128
SUBLANES = 8
VMEM_LIMIT = 56 * 1024 * 1024

TM_PROJ = 512
TM_MLP = 512
TF_MLP = 1024
ROW_PIECES = 2

BF16 = jnp.bfloat16
F32 = jnp.float32


def _cparams(sem):
    return pltpu.CompilerParams(dimension_semantics=sem, vmem_limit_bytes=VMEM_LIMIT)


def _dot(a, b):
    return jnp.dot(a, b, preferred_element_type=F32)


def _dot_nt(a, b):
    return lax.dot_general(a, b, (((1,), (1,)), ((), ())), preferred_element_type=F32)


def _adaln_kernel(c_ref, w_ref, b_ref, o_ref):
    c = c_ref[...]
    s = (c * jax.nn.sigmoid(c)).astype(BF16)
    o_ref[...] = _dot(s, w_ref[...].astype(BF16)) + b_ref[...]


def _adaln(cvec, w_mod, b_mod):
    tn = 1024
    return pl.pallas_call(
        _adaln_kernel,
        out_shape=jax.ShapeDtypeStruct((DEPTH, N_MOD, 6 * D_MODEL), F32),
        grid=(DEPTH, 6 * D_MODEL // tn),
        in_specs=[
            pl.BlockSpec((N_MOD, D_MODEL), lambda l, j: (0, 0)),
            pl.BlockSpec((None, D_MODEL, tn), lambda l, j: (l, 0, j)),
            pl.BlockSpec((None, 1, tn), lambda l, j: (l, 0, j)),
        ],
        out_specs=pl.BlockSpec((None, N_MOD, tn), lambda l, j: (l, 0, j)),
        compiler_params=_cparams(("parallel", "parallel")),
        name="adaln",
    )(cvec, w_mod, b_mod)


def _swap_halves(x, block):
    lane = lax.broadcasted_iota(jnp.int32, x.shape, 1)
    first = (lane % (2 * block)) < block
    return jnp.where(first, pltpu.roll(x, LANES - block, 1), pltpu.roll(x, block, 1))


def _rope(x, cos, sin_signed, block):
    return x * cos + _swap_halves(x, block) * sin_signed


def _in_proj_kernel(latent, x_ref, mod_ref, gmix_ref, w_ref, gq_ref, gk_ref, gkv_ref, gqn_ref, gqr_ref, gkr_ref,
                    cosb_ref, sinb_ref, cosc_ref, sinc_ref,
                    u_ref, qb_ref, kb_ref, vb_ref, qc_ref, ckv_ref, kr_ref, krg_ref):
    x = x_ref[...]
    ms = jnp.mean(x * x, axis=-1, keepdims=True)
    y = x * lax.rsqrt(ms + EPS) * gmix_ref[...]
    h = (y * (1.0 + mod_ref[1:2, :]) + mod_ref[0:1, :]).astype(BF16)

    def proj(off, width):
        return _dot(h, w_ref[:, off:off + width])

    scale_b = 1.0 / math.sqrt(GQA_HEAD_DIM)
    zq = proj(OFF_QB, GQA_HEADS * GQA_HEAD_DIM)
    for hd in range(GQA_HEADS):
        col = zq[:, hd * LANES:(hd + 1) * LANES]
        q = col * lax.rsqrt(jnp.mean(col * col, axis=-1, keepdims=True) + EPS) * gq_ref[...]
        if latent:
            q = _rope(q, cosb_ref[...], sinb_ref[...], GQA_HEAD_DIM // 4)
        qb_ref[:, hd * LANES:(hd + 1) * LANES] = (q * scale_b).astype(qb_ref.dtype)
    zk = proj(OFF_KB, GQA_KV_HEADS * GQA_HEAD_DIM)
    for hd in range(GQA_KV_HEADS):
        col = zk[:, hd * LANES:(hd + 1) * LANES]
        k = col * lax.rsqrt(jnp.mean(col * col, axis=-1, keepdims=True) + EPS) * gk_ref[...]
        if latent:
            k = _rope(k, cosb_ref[...], sinb_ref[...], GQA_HEAD_DIM // 4)
        kb_ref[:, hd * LANES:(hd + 1) * LANES] = k.astype(kb_ref.dtype)

    scale_c = 1.0 / math.sqrt(MLA_QK)
    zn = proj(OFF_QCN, MLA_HEADS * MLA_NOPE)
    zr = proj(OFF_QCR, MLA_HEADS * MLA_ROPE)
    lane = lax.broadcasted_iota(jnp.int32, (x.shape[0], LANES), 1)
    low = lane < MLA_ROPE
    for pair in range(MLA_HEADS // 2):
        colr = zr[:, pair * LANES:(pair + 1) * LANES]
        sq = colr * colr
        ss_lo = jnp.sum(jnp.where(low, sq, 0.0), axis=-1, keepdims=True)
        ss_hi = jnp.sum(jnp.where(low, 0.0, sq), axis=-1, keepdims=True)
        rs = []
        for half, ss_r in ((0, ss_lo), (1, ss_hi)):
            hd = 2 * pair + half
            coln = zn[:, hd * LANES:(hd + 1) * LANES]
            ss = jnp.sum(coln * coln, axis=-1, keepdims=True) + ss_r
            r = lax.rsqrt(ss * (1.0 / MLA_QK) + EPS)
            rs.append(r)
            qc_ref[hd, :, :MLA_NOPE] = (coln * r * gqn_ref[...] * scale_c).astype(qc_ref.dtype)
        qr = colr * jnp.where(low, rs[0], rs[1]) * gqr_ref[...]
        if latent:
            qr = _rope(qr, cosc_ref[...], sinc_ref[...], MLA_ROPE // 4)
        qr = qr * scale_c
        zeros = jnp.zeros((x.shape[0], MLA_QK_PAD - MLA_QK), qc_ref.dtype)
        for half in range(2):
            hd = 2 * pair + half
            qc_ref[hd, :, MLA_NOPE:MLA_QK] = qr[:, half * MLA_ROPE:(half + 1) * MLA_ROPE].astype(qc_ref.dtype)
            qc_ref[hd, :, MLA_QK:] = zeros

    zc = proj(OFF_CKV, MLA_KV_RANK)
    ckv_ref[...] = (zc * lax.rsqrt(jnp.mean(zc * zc, axis=-1, keepdims=True) + EPS) * gkv_ref[...]).astype(ckv_ref.dtype)
    zkr = proj(OFF_KR, 2 * MLA_ROPE)
    kr_ref[...] = zkr[:, :MLA_ROPE]
    krg = zkr * gkr_ref[...]
    if latent:
        krg = _rope(krg, cosc_ref[...], sinc_ref[...], MLA_ROPE // 4)
    krg_ref[...] = krg[:, :MLA_ROPE]

    vb_ref[...] = proj(OFF_VB, GQA_KV_HEADS * GQA_HEAD_DIM).astype(vb_ref.dtype)
    u_ref[...] = proj(OFF_U, SSM_CH)


def _lspec(block_tail, layer):
    zeros = (0,) * len(block_tail)
    return pl.BlockSpec((None,) + tuple(block_tail), lambda *_: (layer,) + zeros)


def _mod_spec(latent, tm, layer):
    tiles_per_seq = DEC_SEQ // tm
    if latent:
        return pl.BlockSpec((None, None, 6, D_MODEL), lambda i, *_: (layer, 1 + i // tiles_per_seq, 0, 0))
    return pl.BlockSpec((None, None, 6, D_MODEL), lambda i, *_: (layer, 0, 0, 0))


def _in_proj(x2d, mod, latent, lw, tabs, tm, layer):
    rows = x2d.shape[0]
    n_tiles = rows // tm
    tiles_per_seq = DEC_SEQ // tm
    if latent:
        tab_map = lambda i: (i % tiles_per_seq, 0)
    else:
        tab_map = lambda i: (0, 0)
    row = lambda i: (i, 0)
    act_dt = BF16 if latent else F32
    in_specs = [
        pl.BlockSpec((tm, D_MODEL), row),
        _mod_spec(latent, tm, layer),
        _lspec((1, D_MODEL), layer),
        _lspec((D_MODEL, IN_WIDTH_P), layer),
        _lspec((1, LANES), layer),
        _lspec((1, LANES), layer),
        _lspec((1, MLA_KV_RANK), layer),
        _lspec((1, LANES), layer),
        _lspec((1, LANES), layer),
        _lspec((1, LANES), layer),
        pl.BlockSpec((tm, LANES), tab_map),
        pl.BlockSpec((tm, LANES), tab_map),
        pl.BlockSpec((tm, LANES), tab_map),
        pl.BlockSpec((tm, LANES), tab_map),
    ]
    out_shape = (
        jax.ShapeDtypeStruct((rows, SSM_CH), F32),
        jax.ShapeDtypeStruct((rows, GQA_HEADS * GQA_HEAD_DIM), BF16),
        jax.ShapeDtypeStruct((rows, GQA_KV_HEADS * GQA_HEAD_DIM), act_dt),
        jax.ShapeDtypeStruct((rows, GQA_KV_HEADS * GQA_HEAD_DIM), act_dt),
        jax.ShapeDtypeStruct((MLA_HEADS, rows, MLA_QK_PAD), BF16),
        jax.ShapeDtypeStruct((rows, MLA_KV_RANK), act_dt),
        jax.ShapeDtypeStruct((rows, MLA_ROPE), F32),
        jax.ShapeDtypeStruct((rows, MLA_ROPE), F32),
    )
    out_specs = (
        pl.BlockSpec((tm, SSM_CH), row),
        pl.BlockSpec((tm, GQA_HEADS * GQA_HEAD_DIM), row),
        pl.BlockSpec((tm, GQA_KV_HEADS * GQA_HEAD_DIM), row),
        pl.BlockSpec((tm, GQA_KV_HEADS * GQA_HEAD_DIM), row),
        pl.BlockSpec((MLA_HEADS, tm, MLA_QK_PAD), lambda i: (0, i, 0)),
        pl.BlockSpec((tm, MLA_KV_RANK), row),
        pl.BlockSpec((tm, MLA_ROPE), row),
        pl.BlockSpec((tm, MLA_ROPE), row),
    )
    return pl.pallas_call(
        functools.partial(_in_proj_kernel, latent),
        out_shape=out_shape,
        grid=(n_tiles,),
        in_specs=in_specs,
        out_specs=out_specs,
        compiler_params=_cparams(("parallel",)),
        name="in_proj_lat" if latent else "in_proj_ctx",
    )(x2d, mod, lw['norm_mix'], lw['w_in'], lw['gqa_q_norm'], lw['gqa_k_norm'], lw['mla_kv_norm'],
      lw['mla_q_nope_g'], lw['mla_q_rope_g'], lw['mla_k_rope_g'],
      tabs['cos_b'], tabs['sin_b'], tabs['cos_c'], tabs['sin_c'])


SSM_TC = 128
SSM_LANE_SPLIT = 2
SSM_UNROLL = 8


def _ssm_scan_kernel(n_chunks, u_ref, h0_ref, lam_ref, b_ref, cre_ref, cim_ref, y_ref, hT_ref,
                     utm_ref, sre_ref, sim_ref, h_ref):
    d = pl.program_id(1)
    i = pl.program_id(2)
    tc = SSM_TC

    @pl.when(i == 0)
    def _():
        h_ref[...] = h0_ref[...]

    for b in range(SUBLANES):
        for j in range(SSM_CH // LANES):
            utm_ref[j, pl.ds(b, tc, stride=SUBLANES), :] = u_ref[b, :, j * LANES:(j + 1) * LANES]

    half_n = SSM_N // 2
    half_c = SSM_CH // 2
    for k in range(2):
        uk = jnp.concatenate([utm_ref[2 * k], utm_ref[2 * k + 1]], axis=-1).astype(BF16)
        r = _dot(uk, b_ref[k])
        sre_ref[:, k * half_n:(k + 1) * half_n] = r[:, :half_n]
        sim_ref[:, k * half_n:(k + 1) * half_n] = r[:, half_n:]

    piece = SSM_N // SSM_LANE_SPLIT
    for p in range(SSM_LANE_SPLIT):
        sl = slice(p * piece, (p + 1) * piece)
        lr = lam_ref[0, :, sl]
        li = lam_ref[1, :, sl]

        def step(s, carry):
            hr, hi = carry
            t = s + d * (tc - 1 - 2 * s)
            rowi = pl.multiple_of(t * SUBLANES, SUBLANES)
            nr = lr * hr - li * hi + sre_ref[pl.ds(rowi, SUBLANES), sl]
            ni = lr * hi + li * hr + sim_ref[pl.ds(rowi, SUBLANES), sl]
            sre_ref[pl.ds(rowi, SUBLANES), sl] = nr
            sim_ref[pl.ds(rowi, SUBLANES), sl] = ni
            return nr, ni

        hr, hi = lax.fori_loop(0, tc, step, (h_ref[0, :, sl], h_ref[1, :, sl]), unroll=SSM_UNROLL)
        h_ref[0, :, sl] = hr
        h_ref[1, :, sl] = hi

    for k in range(2):
        yk = (_dot(sre_ref[:, k * half_n:(k + 1) * half_n].astype(BF16), cre_ref[k])
              - _dot(sim_ref[:, k * half_n:(k + 1) * half_n].astype(BF16), cim_ref[k]))
        y_ref[2 * k] = yk[:, :LANES]
        y_ref[2 * k + 1] = yk[:, LANES:]

    @pl.when(i == n_chunks - 1)
    def _():
        hT_ref[...] = h_ref[...]


def _ssm_scan(u3, h0, sp, layer):
    bsz, t, _ = u3.shape
    n_groups = bsz // SUBLANES
    n_chunks = t // SSM_TC
    tc = SSM_TC

    def chunk(d, i):
        return i + d * (n_chunks - 1 - 2 * i)

    return pl.pallas_call(
        functools.partial(_ssm_scan_kernel, n_chunks),
        out_shape=(
            jax.ShapeDtypeStruct((2, n_groups, SSM_CH // LANES, t * SUBLANES, LANES), F32),
            jax.ShapeDtypeStruct((n_groups, 2, 2, SUBLANES, SSM_N), F32),
        ),
        grid=(n_groups, 2, n_chunks),
        in_specs=[
            pl.BlockSpec((SUBLANES, tc, SSM_CH), lambda g, d, i: (g, chunk(d, i), 0)),
            pl.BlockSpec((None, None, 2, SUBLANES, SSM_N), lambda g, d, i: (g, d, 0, 0, 0)),
            pl.BlockSpec((None, None, 2, SUBLANES, SSM_N), lambda g, d, i: (layer, d, 0, 0, 0)),
            pl.BlockSpec((None, None, 2, SSM_CH // 2, SSM_N), lambda g, d, i: (layer, d, 0, 0, 0)),
            pl.BlockSpec((None, None, 2, SSM_N // 2, SSM_CH // 2), lambda g, d, i: (layer, d, 0, 0, 0)),
            pl.BlockSpec((None, None, 2, SSM_N // 2, SSM_CH // 2), lambda g, d, i: (layer, d, 0, 0, 0)),
        ],
        out_specs=(
            pl.BlockSpec((None, None, SSM_CH // LANES, tc * SUBLANES, LANES),
                         lambda g, d, i: (d, g, 0, chunk(d, i), 0)),
            pl.BlockSpec((None, None, 2, SUBLANES, SSM_N), lambda g, d, i: (g, d, 0, 0, 0)),
        ),
        scratch_shapes=[
            pltpu.VMEM((SSM_CH // LANES, tc * SUBLANES, LANES), F32),
            pltpu.VMEM((tc * SUBLANES, SSM_N), F32),
            pltpu.VMEM((tc * SUBLANES, SSM_N), F32),
            pltpu.VMEM((2, SUBLANES, SSM_N), F32),
        ],
        compiler_params=_cparams(("parallel", "parallel", "arbitrary")),
        name="ssm_scan",
    )(u3, h0, sp['lam'], sp['b'], sp['c_re'], sp['c_im'])


def _ssm_glu_kernel(u_ref, y_ref, d_ref, w_ref, o_ref, ycat_ref):
    tc = SSM_TC
    for b in range(SUBLANES):
        for j in range(SSM_CH // LANES):
            sl = slice(j * LANES, (j + 1) * LANES)
            rows_b = pl.ds(b, tc, stride=SUBLANES)
            ycat_ref[b * tc:(b + 1) * tc, sl] = (y_ref[0, j, rows_b, :] + y_ref[1, j, rows_b, :]
                                                + d_ref[:, sl] * u_ref[b, :, sl])
    zg = _dot(ycat_ref[...].astype(BF16), w_ref[...])
    out = zg[:, :SSM_CH] * jax.nn.sigmoid(zg[:, SSM_CH:])
    o_ref[...] = out.reshape(SUBLANES, tc, SSM_CH).astype(o_ref.dtype)


def _ssm_glu(u3, y, d_skip, w_glu, layer):
    bsz, t, _ = u3.shape
    n_groups = bsz // SUBLANES
    n_chunks = t // SSM_TC
    tc = SSM_TC
    return pl.pallas_call(
        _ssm_glu_kernel,
        out_shape=jax.ShapeDtypeStruct((bsz, t, SSM_CH), BF16),
        grid=(n_groups, n_chunks),
        in_specs=[
            pl.BlockSpec((SUBLANES, tc, SSM_CH), lambda g, i: (g, i, 0)),
            pl.BlockSpec((2, None, SSM_CH // LANES, tc * SUBLANES, LANES), lambda g, i: (0, g, 0, i, 0)),
            _lspec((1, SSM_CH), layer),
            _lspec((SSM_CH, 2 * SSM_CH), layer),
        ],
        out_specs=pl.BlockSpec((SUBLANES, tc, SSM_CH), lambda g, i: (g, i, 0)),
        scratch_shapes=[pltpu.VMEM((SUBLANES * tc, SSM_CH), F32)],
        compiler_params=_cparams(("parallel", "parallel")),
        name="ssm_glu",
    )(u3, y, d_skip, w_glu)


def _ones_column(n):
    return jnp.ones((n, LANES), BF16)


def _softmax_pv(scores, v_ext):
    m = jnp.max(scores, axis=-1, keepdims=True)
    p = jnp.exp((scores - m).astype(BF16))
    o = _dot(p, v_ext)
    return o[:, :LANES] / o[:, LANES:]


ATTN_TQ = 256


def _for_each_q_block(nb, seq, body):
    nq = seq // ATTN_TQ
    for s in range(nb):
        for qi in range(nq):
            body(s, s * seq + qi * ATTN_TQ)


def _gqa_kernel(latent, nb, seq, *refs):
    if latent:
        q_ref, k_ref, v_ref, kc_ref, vc_ref, o_ref, k_s, v_s = refs
    else:
        q_ref, k_ref, v_ref, o_ref, k_s, v_s = refs
    tq = ATTN_TQ
    for s in range(nb):
        for h in range(GQA_KV_HEADS):
            hl = slice(h * LANES, (h + 1) * LANES)
            k_s[s, h, 0:seq, :] = k_ref[s * seq:(s + 1) * seq, hl].astype(BF16)
            v_s[s, h, 0:seq, :LANES] = v_ref[s * seq:(s + 1) * seq, hl].astype(BF16)
            if latent:
                k_s[s, h, seq:, :] = kc_ref[:, hl].astype(BF16)
                v_s[s, h, seq:, :LANES] = vc_ref[:, hl].astype(BF16)
            v_s[s, h, :, LANES:] = _ones_column(v_s.shape[2])

    def body(s, r0):
        for h in range(GQA_KV_HEADS):
            heads = [h * GQA_GROUP + g for g in range(GQA_GROUP)]
            q3 = jnp.concatenate([q_ref[pl.ds(r0, tq), hd * LANES:(hd + 1) * LANES] for hd in heads], axis=0)
            o = _softmax_pv(_dot_nt(q3, k_s[s, h]), v_s[s, h])
            for g, hd in enumerate(heads):
                o_ref[pl.ds(r0, tq), hd * LANES:(hd + 1) * LANES] = o[g * tq:(g + 1) * tq].astype(o_ref.dtype)

    _for_each_q_block(nb, seq, body)


def _gqa_attn(qb, kb, vb, latent, cache_k=None, cache_v=None, layer=0):
    rows = qb.shape[0]
    seq = DEC_SEQ if latent else SEQ
    nb = 1 if latent else 4
    t_all = seq + (PAST_LEN if latent else 0)
    qw = GQA_HEADS * GQA_HEAD_DIM
    kw = GQA_KV_HEADS * GQA_HEAD_DIM
    row = lambda i: (i, 0)
    in_specs = [
        pl.BlockSpec((nb * seq, qw), row),
        pl.BlockSpec((nb * seq, kw), row),
        pl.BlockSpec((nb * seq, kw), row),
    ]
    args = [qb, kb, vb]
    if latent:
        cspec = pl.BlockSpec((None, None, PAST_LEN, kw), lambda i: (i, layer, 0, 0))
        in_specs += [cspec, cspec]
        args += [cache_k, cache_v]
    return pl.pallas_call(
        functools.partial(_gqa_kernel, latent, nb, seq),
        out_shape=jax.ShapeDtypeStruct((rows, qw), BF16),
        grid=(rows // (nb * seq),),
        in_specs=in_specs,
        out_specs=pl.BlockSpec((nb * seq, qw), row),
        scratch_shapes=[
            pltpu.VMEM((nb, GQA_KV_HEADS, t_all, GQA_HEAD_DIM), BF16),
            pltpu.VMEM((nb, GQA_KV_HEADS, t_all, 2 * LANES), BF16),
        ],
        compiler_params=_cparams(("parallel",)),
        name="gqa_lat" if latent else "gqa_ctx",
    )(*args)


def _mla_kernel(latent, nb, seq, *refs):
    if latent:
        (q_ref, ckv_ref, kr_ref, krg_ref, ckvc_ref, krc_ref, wuk_ref, wuv_ref, gn_ref, gr_ref,
         o_ref, k_s, v_s) = refs
    else:
        (q_ref, ckv_ref, kr_ref, krg_ref, wuk_ref, wuv_ref, gn_ref, gr_ref, o_ref, k_s, v_s) = refs
    tq = ATTN_TQ

    def expand(ckv, kr_raw, krg, place):
        c = ckv.astype(BF16)
        kn_all = _dot(c, wuk_ref[...])
        v_all = _dot(c, wuv_ref[...])
        ss_kr = jnp.sum(kr_raw * kr_raw, axis=-1, keepdims=True)
        for h in range(MLA_HEADS):
            hl = slice(h * LANES, (h + 1) * LANES)
            kn = kn_all[:, hl]
            rs = lax.rsqrt((jnp.sum(kn * kn, axis=-1, keepdims=True) + ss_kr) * (1.0 / MLA_QK) + EPS)
            kn = (kn * rs * gn_ref[...]).astype(BF16)
            kp = (krg * rs).astype(BF16)
            vv = v_all[:, hl].astype(BF16)
            for s, lo, n, rows in place:
                k_s[s, h, lo:lo + n, :MLA_NOPE] = kn[rows]
                k_s[s, h, lo:lo + n, MLA_NOPE:MLA_QK] = kp[rows]
                k_s[s, h, lo:lo + n, MLA_QK:] = jnp.zeros((n, MLA_QK_PAD - MLA_QK), BF16)
                v_s[s, h, lo:lo + n, :MLA_V] = vv[rows]
                v_s[s, h, lo:lo + n, MLA_V:] = _ones_column(n)

    expand(ckv_ref[...], kr_ref[...], krg_ref[...],
           [(s, 0, seq, slice(s * seq, (s + 1) * seq)) for s in range(nb)])
    if latent:
        krc = krc_ref[...]
        expand(ckvc_ref[...], krc, krc * gr_ref[...], [(0, seq, PAST_LEN, slice(0, PAST_LEN))])

    def body(s, r0):
        for h in range(MLA_HEADS):
            sc = _dot_nt(q_ref[h, pl.ds(r0, tq), :], k_s[s, h])
            o_ref[pl.ds(r0, tq), h * LANES:(h + 1) * LANES] = _softmax_pv(sc, v_s[s, h]).astype(o_ref.dtype)

    _for_each_q_block(nb, seq, body)


def _mla_attn(qc, ckv, kr, krg, lw, latent, cache_ckv=None, cache_kr=None, layer=0):
    rows = qc.shape[1]
    seq = DEC_SEQ if latent else SEQ
    nb = 1 if latent else 4
    t_all = seq + (PAST_LEN if latent else 0)
    row = lambda i: (i, 0)
    in_specs = [
        pl.BlockSpec((MLA_HEADS, nb * seq, MLA_QK_PAD), lambda i: (0, i, 0)),
        pl.BlockSpec((nb * seq, MLA_KV_RANK), row),
        pl.BlockSpec((nb * seq, MLA_ROPE), row),
        pl.BlockSpec((nb * seq, MLA_ROPE), row),
    ]
    args = [qc, ckv, kr, krg]
    if latent:
        in_specs += [
            pl.BlockSpec((None, None, PAST_LEN, MLA_KV_RANK), lambda i: (i, layer, 0, 0)),
            pl.BlockSpec((None, None, PAST_LEN, MLA_ROPE), lambda i: (i, layer, 0, 0)),
        ]
        args += [cache_ckv, cache_kr]
    in_specs += [
        _lspec((MLA_KV_RANK, MLA_HEADS * MLA_NOPE), layer),
        _lspec((MLA_KV_RANK, MLA_HEADS * MLA_V), layer),
        _lspec((1, MLA_NOPE), layer),
        _lspec((1, MLA_ROPE), layer),
    ]
    args += [lw['mla_w_uk'], lw['mla_w_uv'], lw['mla_k_nope_g'], lw['mla_k_rope_g64']]
    return pl.pallas_call(
        functools.partial(_mla_kernel, latent, nb, seq),
        out_shape=jax.ShapeDtypeStruct((rows, MLA_HEADS * MLA_V), BF16),
        grid=(rows // (nb * seq),),
        in_specs=in_specs,
        out_specs=pl.BlockSpec((nb * seq, MLA_HEADS * MLA_V), row),
        scratch_shapes=[
            pltpu.VMEM((nb, MLA_HEADS, t_all, MLA_QK_PAD), BF16),
            pltpu.VMEM((nb, MLA_HEADS, t_all, MLA_V + LANES), BF16),
        ],
        compiler_params=_cparams(("parallel",)),
        name="mla_lat" if latent else "mla_ctx",
    )(*args)


def _out_proj_kernel(ya_ref, yb_ref, yc_ref, x_ref, mod_ref, w_ref, g_ref, x1_ref, h2_ref):
    wa = SSM_CH
    wb = wa + GQA_HEADS * GQA_HEAD_DIM
    piece = x_ref.shape[0] // ROW_PIECES
    pieces = [slice(r * piece, (r + 1) * piece) for r in range(ROW_PIECES)]
    outs = [(_dot(ya_ref[rs, :], w_ref[0:wa, :]) + _dot(yb_ref[rs, :], w_ref[wa:wb, :])
             + _dot(yc_ref[rs, :], w_ref[wb:, :])) for rs in pieces]
    for rs, o in zip(pieces, outs):
        x1 = x_ref[rs, :] + mod_ref[2:3, :] * o
        x1_ref[rs, :] = x1
        ms = jnp.mean(x1 * x1, axis=-1, keepdims=True)
        y = x1 * lax.rsqrt(ms + EPS) * g_ref[...]
        h2_ref[rs, :] = (y * (1.0 + mod_ref[4:5, :]) + mod_ref[3:4, :]).astype(h2_ref.dtype)


def _out_proj(ya, yb, yc, x2d, mod, lw, latent, tm, layer):
    rows = x2d.shape[0]
    row = lambda i: (i, 0)
    return pl.pallas_call(
        _out_proj_kernel,
        out_shape=(jax.ShapeDtypeStruct((rows, D_MODEL), F32), jax.ShapeDtypeStruct((rows, D_MODEL), BF16)),
        grid=(rows // tm,),
        in_specs=[
            pl.BlockSpec((tm, SSM_CH), row),
            pl.BlockSpec((tm, GQA_HEADS * GQA_HEAD_DIM), row),
            pl.BlockSpec((tm, MLA_HEADS * MLA_V), row),
            pl.BlockSpec((tm, D_MODEL), row),
            _mod_spec(latent, tm, layer),
            _lspec((D_MODEL, D_MODEL), layer),
            _lspec((1, D_MODEL), layer),
        ],
        out_specs=(pl.BlockSpec((tm, D_MODEL), row), pl.BlockSpec((tm, D_MODEL), row)),
        compiler_params=_cparams(("parallel",)),
        name="out_proj",
    )(ya, yb, yc, x2d, mod, lw['w_out'], lw['norm_mlp'])


def _mlp_kernel(h2_ref, x1_ref, mod_ref, w1_ref, w2_ref, o_ref):
    j = pl.program_id(1)

    @pl.when(j == 0)
    def _():
        o_ref[...] = jnp.zeros_like(o_ref)

    f = jnp.maximum(_dot(h2_ref[...], w1_ref[...]), 0.0)
    o_ref[...] += _dot((f * f).astype(BF16), w2_ref[...])

    @pl.when(j == pl.num_programs(1) - 1)
    def _():
        o_ref[...] = x1_ref[...] + mod_ref[5:6, :] * o_ref[...]


def _mlp(h2, x1, mod, lw, latent, tm, tf, layer):
    rows = h2.shape[0]
    return pl.pallas_call(
        _mlp_kernel,
        out_shape=jax.ShapeDtypeStruct((rows, D_MODEL), F32),
        grid=(rows // tm, D_FF // tf),
        in_specs=[
            pl.BlockSpec((tm, D_MODEL), lambda i, j: (i, 0)),
            pl.BlockSpec((tm, D_MODEL), lambda i, j: (i, 0), pipeline_mode=pl.Buffered(1)),
            _mod_spec(latent, tm, layer),
            pl.BlockSpec((None, D_MODEL, tf), lambda i, j: (layer, 0, j)),
            pl.BlockSpec((None, tf, D_MODEL), lambda i, j: (layer, j, 0)),
        ],
        out_specs=pl.BlockSpec((tm, D_MODEL), lambda i, j: (i, 0)),
        compiler_params=_cparams(("parallel", "arbitrary")),
        name="mlp",
    )(h2, x1, mod, lw['w_ff1'], lw['w_ff2'])


def _permute_w_in(w):
    base = OFF_QCN
    ckv0 = base + MLA_HEADS * MLA_QK
    qc = w[:, :, base:ckv0].reshape(DEPTH, D_MODEL, MLA_HEADS, MLA_QK)
    nope = qc[..., :MLA_NOPE].reshape(DEPTH, D_MODEL, MLA_HEADS * MLA_NOPE)
    rope = qc[..., MLA_NOPE:].reshape(DEPTH, D_MODEL, MLA_HEADS * MLA_ROPE)
    kr = w[:, :, ckv0 + MLA_KV_RANK:]
    parts = [w[:, :, :base], nope, rope, w[:, :, ckv0:ckv0 + MLA_KV_RANK], kr, kr]
    return jnp.concatenate(parts, axis=2).astype(BF16)


def _rope_tables(seq):
    t = jnp.arange(seq)
    row = (t // GRID_W).astype(F32)
    col = (t % GRID_W).astype(F32)

    def table(d):
        quarter = d // 4
        inv = ROPE_BASE ** (-(jnp.arange(quarter, dtype=F32) / quarter))
        ar = row[:, None] * inv[None, :]
        ac = col[:, None] * inv[None, :]
        cos = jnp.concatenate([jnp.cos(ar), jnp.cos(ar), jnp.cos(ac), jnp.cos(ac)], axis=-1)
        sin = jnp.concatenate([-jnp.sin(ar), jnp.sin(ar), -jnp.sin(ac), jnp.sin(ac)], axis=-1)
        reps = LANES // d
        return jnp.tile(cos, (1, reps)), jnp.tile(sin, (1, reps))

    cos_b, sin_b = table(GQA_HEAD_DIM)
    cos_c, sin_c = table(MLA_ROPE)
    return {'cos_b': cos_b, 'sin_b': sin_b, 'cos_c': cos_c, 'sin_c': sin_c}


def _ssm_params(lam_re, lam_im, log_dt, b_re, b_im, c_re, c_im):
    a = lam_re.astype(F32)
    w = lam_im.astype(F32)
    dt = jnp.exp(log_dt.astype(F32))[..., None]
    mag = jnp.exp(a * dt)
    lbr = mag * jnp.cos(w * dt)
    lbi = mag * jnp.sin(w * dt)
    den = a * a + w * w
    cr = (((lbr - 1.0) * a + lbi * w) / den)[..., None]
    ci = ((lbi * a - (lbr - 1.0) * w) / den)[..., None]
    bre = b_re.astype(F32)
    bim = b_im.astype(F32)
    bb_re = cr * bre - ci * bim
    bb_im = cr * bim + ci * bre
    lam_ri = jnp.stack([lbr.reshape(DEPTH, 2, SSM_N), lbi.reshape(DEPTH, 2, SSM_N)], axis=2)
    lam_b = jnp.broadcast_to(lam_ri[:, :, :, None, :], (DEPTH, 2, 2, SUBLANES, SSM_N))
    gh = SSM_GROUPS // 2
    eye = jnp.eye(gh, dtype=F32)[:, None, :, None]

    def blockdiag(x, rows_per_g, cols_per_g):
        x = x.reshape(DEPTH, 2, 2, gh, rows_per_g, 1, cols_per_g) * eye
        return x.reshape(DEPTH, 2, 2, gh * rows_per_g, gh * cols_per_g)

    def bmat(x):
        return blockdiag(jnp.swapaxes(x, -1, -2), SSM_GROUP, SSM_STATE)

    def cmat(x):
        return blockdiag(jnp.swapaxes(x, -1, -2), SSM_STATE, SSM_GROUP)

    b_cat = jnp.concatenate([bmat(bb_re), bmat(bb_im)], axis=-1).astype(BF16)
    return {'lam': lam_b, 'b': b_cat, 'c_re': cmat(c_re.astype(F32)).astype(BF16),
            'c_im': cmat(c_im.astype(F32)).astype(BF16)}


def _trunk_layer(x2d, mod, lw, sp, tabs, latent, ctx, layer):
    seq = DEC_SEQ if latent else SEQ
    bsz = x2d.shape[0] // seq
    u, qb, kb, vb, qc, ckv, kr, krg = _in_proj(x2d, mod, latent, lw, tabs, TM_PROJ, layer)

    u3 = u.reshape(bsz, seq, SSM_CH)
    y_tm, h_t = _ssm_scan(u3, ctx['h0'], sp, layer)
    ya = _ssm_glu(u3, y_tm, lw['ssm_d'], lw['ssm_w_glu'], layer).reshape(bsz * seq, SSM_CH)

    if latent:
        yb = _gqa_attn(qb, kb, vb, True, ctx['k'], ctx['v'], layer)
        yc = _mla_attn(qc, ckv, kr, krg, lw, True, ctx['ckv'], ctx['kr'], layer)
    else:
        yb = _gqa_attn(qb, kb, vb, False)
        yc = _mla_attn(qc, ckv, kr, krg, lw, False, layer=layer)

    x1, h2 = _out_proj(ya, yb, yc, x2d, mod, lw, latent, TM_PROJ, layer)
    x2 = _mlp(h2, x1, mod, lw, latent, TM_MLP, TF_MLP, layer)
    return x2, (kb, vb, ckv, kr, h_t)


def kernel(x_prompt, x_sample, cache_attn_k, cache_attn_v, cache_mla_ckv, cache_mla_krope, state_ssm, c, c_ctx, w_mod, b_mod, norm_mix, norm_mlp, w_in, gqa_q_norm, gqa_k_norm, mla_kv_norm, mla_q_norm, mla_k_norm, mla_w_uk, mla_w_uv, ssm_lam_re, ssm_lam_im, ssm_log_dt, ssm_b_re, ssm_b_im, ssm_c_re, ssm_c_im, ssm_d, ssm_w_glu, w_out, w_ff1, w_ff2):
    cvec = jnp.zeros((N_MOD, D_MODEL), F32).at[0].set(c_ctx).at[1:1 + DEC_BATCH].set(c)
    mod = _adaln(cvec, w_mod, b_mod.reshape(DEPTH, 1, 6 * D_MODEL)).reshape(DEPTH, N_MOD, 6, D_MODEL)

    tabs = _rope_tables(DEC_SEQ)
    cache_k = cache_attn_k.reshape(DEC_BATCH, DEPTH, PAST_LEN, GQA_KV_HEADS * GQA_HEAD_DIM)
    cache_v = cache_attn_v.reshape(DEC_BATCH, DEPTH, PAST_LEN, GQA_KV_HEADS * GQA_HEAD_DIM)

    xp = x_prompt.reshape(BATCH * SEQ, D_MODEL)
    xs = x_sample.reshape(DEC_BATCH * DEC_SEQ, D_MODEL)
    new_k, new_v, new_ckv, new_kr, new_ssm = [], [], [], [], []
    h0_zero = jnp.zeros((BATCH // SUBLANES, 2, 2, SUBLANES, SSM_N), F32)

    row3 = lambda a: a.reshape(DEPTH, 1, -1)
    dup = lambda a: jnp.concatenate([a, a], axis=-1)
    lw = {
        'norm_mix': row3(norm_mix), 'norm_mlp': row3(norm_mlp),
        'w_in': _permute_w_in(w_in),
        'w_out': w_out.astype(BF16), 'w_ff1': w_ff1.astype(BF16), 'w_ff2': w_ff2.astype(BF16),
        'gqa_q_norm': row3(gqa_q_norm), 'gqa_k_norm': row3(gqa_k_norm), 'mla_kv_norm': row3(mla_kv_norm),
        'mla_q_nope_g': row3(mla_q_norm[:, :MLA_NOPE]),
        'mla_q_rope_g': row3(dup(mla_q_norm[:, MLA_NOPE:])),
        'mla_k_nope_g': row3(mla_k_norm[:, :MLA_NOPE]),
        'mla_k_rope_g': row3(dup(mla_k_norm[:, MLA_NOPE:])),
        'mla_k_rope_g64': row3(mla_k_norm[:, MLA_NOPE:]),
        'mla_w_uk': mla_w_uk.reshape(DEPTH, MLA_KV_RANK, MLA_HEADS * MLA_NOPE).astype(BF16),
        'mla_w_uv': mla_w_uv.reshape(DEPTH, MLA_KV_RANK, MLA_HEADS * MLA_V).astype(BF16),
        'ssm_d': row3(ssm_d), 'ssm_w_glu': ssm_w_glu.astype(BF16),
    }
    sp = _ssm_params(ssm_lam_re, ssm_lam_im, ssm_log_dt, ssm_b_re, ssm_b_im, ssm_c_re, ssm_c_im)
    h0_lat = state_ssm.reshape(DEC_BATCH, DEPTH, 2, SSM_N, 2).transpose(1, 2, 4, 0, 3)[:, None]

    for l in range(DEPTH):
        xp, (k, v, ckv_n, kr, h_t) = _trunk_layer(xp, mod, lw, sp, tabs, False, {'h0': h0_zero}, l)
        new_k.append(k.reshape(BATCH, SEQ, GQA_KV_HEADS, GQA_HEAD_DIM))
        new_v.append(v.reshape(BATCH, SEQ, GQA_KV_HEADS, GQA_HEAD_DIM))
        new_ckv.append(ckv_n.reshape(BATCH, SEQ, MLA_KV_RANK))
        new_kr.append(kr.reshape(BATCH, SEQ, MLA_ROPE))
        hs = h_t.transpose(0, 3, 1, 4, 2).reshape(BATCH, 2, SSM_GROUPS, SSM_STATE, 2)
        new_ssm.append(hs)

        ctx = {'k': cache_k, 'v': cache_v, 'ckv': cache_mla_ckv, 'kr': cache_mla_krope, 'h0': h0_lat[l]}
        xs, _ = _trunk_layer(xs, mod, lw, sp, tabs, True, ctx, l)

    return (xp.reshape(BATCH, SEQ, D_MODEL), xs.reshape(DEC_BATCH, DEC_SEQ, D_MODEL),
            jnp.stack(new_k, axis=1), jnp.stack(new_v, axis=1), jnp.stack(new_ckv, axis=1),
            jnp.stack(new_kr, axis=1), jnp.stack(new_ssm, axis=1))
```

```python
import functools
import math

import jax
import jax.numpy as jnp
from jax import lax
from jax.experimental import pallas as pl
from jax.experimental.pallas import tpu as pltpu

D_MODEL = 2048
BATCH = 32
SEQ = 256
DEPTH = 2
DEC_BATCH = 8
DEC_SEQ = 1024
PAST_LEN = 512
GRID_W = 64
ROPE_BASE = 10000.0
EPS = 1e-6
SSM_CH = 512
SSM_GROUP = 16
SSM_GROUPS = SSM_CH // SSM_GROUP
SSM_STATE = 64
SSM_N = SSM_GROUPS * SSM_STATE
GQA_HEADS = 6
GQA_KV_HEADS = 2
GQA_GROUP = GQA_HEADS // GQA_KV_HEADS
GQA_HEAD_DIM = 128
MLA_HEADS = 6
MLA_NOPE = 128
MLA_ROPE = 64
MLA_QK = MLA_NOPE + MLA_ROPE
MLA_QK_PAD = 256
MLA_V = 128
MLA_KV_RANK = 512
D_FF = 4 * D_MODEL
N_MOD = 16

OFF_U = 0
OFF_QB = OFF_U + SSM_CH
OFF_KB = OFF_QB + GQA_HEADS * GQA_HEAD_DIM
OFF_VB = OFF_KB + GQA_KV_HEADS * GQA_HEAD_DIM
OFF_QCN = OFF_VB + GQA_KV_HEADS * GQA_HEAD_DIM
OFF_QCR = OFF_QCN + MLA_HEADS * MLA_NOPE
OFF_CKV = OFF_QCR + MLA_HEADS * MLA_ROPE
OFF_KR = OFF_CKV + MLA_KV_RANK
IN_WIDTH_P = OFF_KR + 2 * MLA_ROPE

LANES = 128
SUBLANES = 8
VMEM_LIMIT = 56 * 1024 * 1024

TM_PROJ = 512
TM_MLP = 1024
TF_MLP = 512
ROW_PIECES = 2

BF16 = jnp.bfloat16
F32 = jnp.float32


def _cparams(sem):
    return pltpu.CompilerParams(dimension_semantics=sem, vmem_limit_bytes=VMEM_LIMIT)


def _dot(a, b):
    return jnp.dot(a, b, preferred_element_type=F32)


def _dot_nt(a, b):
    return lax.dot_general(a, b, (((1,), (1,)), ((), ())), preferred_element_type=F32)


def _adaln_kernel(c_ref, w_ref, b_ref, o_ref):
    c = c_ref[...]
    s = (c * jax.nn.sigmoid(c)).astype(BF16)
    o_ref[...] = _dot(s, w_ref[...].astype(BF16)) + b_ref[...]


def _adaln(cvec, w_mod, b_mod):
    tn = 1024
    return pl.pallas_call(
        _adaln_kernel,
        out_shape=jax.ShapeDtypeStruct((DEPTH, N_MOD, 6 * D_MODEL), F32),
        grid=(DEPTH, 6 * D_MODEL // tn),
        in_specs=[
            pl.BlockSpec((N_MOD, D_MODEL), lambda l, j: (0, 0)),
            pl.BlockSpec((None, D_MODEL, tn), lambda l, j: (l, 0, j)),
            pl.BlockSpec((None, 1, tn), lambda l, j: (l, 0, j)),
        ],
        out_specs=pl.BlockSpec((None, N_MOD, tn), lambda l, j: (l, 0, j)),
        compiler_params=_cparams(("parallel", "parallel")),
        name="adaln",
    )(cvec, w_mod, b_mod)


def _swap_halves(x, block):
    lane = lax.broadcasted_iota(jnp.int32, x.shape, 1)
    first = (lane % (2 * block)) < block
    return jnp.where(first, pltpu.roll(x, LANES - block, 1), pltpu.roll(x, block, 1))


def _rope(x, cos, sin_signed, block):
    return x * cos + _swap_halves(x, block) * sin_signed


def _in_proj_kernel(latent, x_ref, mod_ref, gmix_ref, w_ref, gq_ref, gk_ref, gkv_ref, gqn_ref, gqr_ref, gkr_ref,
                    cosb_ref, sinb_ref, cosc_ref, sinc_ref,
                    u_ref, qb_ref, kb_ref, vb_ref, qc_ref, ckv_ref, kr_ref, krg_ref):
    x = x_ref[...]
    ms = jnp.mean(x * x, axis=-1, keepdims=True)
    y = x * lax.rsqrt(ms + EPS) * gmix_ref[...]
    h = (y * (1.0 + mod_ref[1:2, :]) + mod_ref[0:1, :]).astype(BF16)

    def proj(off, width):
        return _dot(h, w_ref[:, off:off + width])

    scale_b = 1.0 / math.sqrt(GQA_HEAD_DIM)
    zq = proj(OFF_QB, GQA_HEADS * GQA_HEAD_DIM)
    for hd in range(GQA_HEADS):
        col = zq[:, hd * LANES:(hd + 1) * LANES]
        q = col * lax.rsqrt(jnp.mean(col * col, axis=-1, keepdims=True) + EPS) * gq_ref[...]
        if latent:
            q = _rope(q, cosb_ref[...], sinb_ref[...], GQA_HEAD_DIM // 4)
        qb_ref[:, hd * LANES:(hd + 1) * LANES] = (q * scale_b).astype(qb_ref.dtype)
    zk = proj(OFF_KB, GQA_KV_HEADS * GQA_HEAD_DIM)
    for hd in range(GQA_KV_HEADS):
        col = zk[:, hd * LANES:(hd + 1) * LANES]
        k = col * lax.rsqrt(jnp.mean(col * col, axis=-1, keepdims=True) + EPS) * gk_ref[...]
        if latent:
            k = _rope(k, cosb_ref[...], sinb_ref[...], GQA_HEAD_DIM // 4)
        kb_ref[:, hd * LANES:(hd + 1) * LANES] = k.astype(kb_ref.dtype)

    scale_c = 1.0 / math.sqrt(MLA_QK)
    zn = proj(OFF_QCN, MLA_HEADS * MLA_NOPE)
    zr = proj(OFF_QCR, MLA_HEADS * MLA_ROPE)
    lane = lax.broadcasted_iota(jnp.int32, (x.shape[0], LANES), 1)
    low = lane < MLA_ROPE
    for pair in range(MLA_HEADS // 2):
        colr = zr[:, pair * LANES:(pair + 1) * LANES]
        sq = colr * colr
        ss_lo = jnp.sum(jnp.where(low, sq, 0.0), axis=-1, keepdims=True)
        ss_hi = jnp.sum(jnp.where(low, 0.0, sq), axis=-1, keepdims=True)
        rs = []
        for half, ss_r in ((0, ss_lo), (1, ss_hi)):
            hd = 2 * pair + half
            coln = zn[:, hd * LANES:(hd + 1) * LANES]
            ss = jnp.sum(coln * coln, axis=-1, keepdims=True) + ss_r
            r = lax.rsqrt(ss * (1.0 / MLA_QK) + EPS)
            rs.append(r)
            qc_ref[hd, :, :MLA_NOPE] = (coln * r * gqn_ref[...] * scale_c).astype(qc_ref.dtype)
        qr = colr * jnp.where(low, rs[0], rs[1]) * gqr_ref[...]
        if latent:
            qr = _rope(qr, cosc_ref[...], sinc_ref[...], MLA_ROPE // 4)
        qr = qr * scale_c
        zeros = jnp.zeros((x.shape[0], MLA_QK_PAD - MLA_QK), qc_ref.dtype)
        for half in range(2):
            hd = 2 * pair + half
            qc_ref[hd, :, MLA_NOPE:MLA_QK] = qr[:, half * MLA_ROPE:(half + 1) * MLA_ROPE].astype(qc_ref.dtype)
            qc_ref[hd, :, MLA_QK:] = zeros

    zc = proj(OFF_CKV, MLA_KV_RANK)
    ckv_ref[...] = (zc * lax.rsqrt(jnp.mean(zc * zc, axis=-1, keepdims=True) + EPS) * gkv_ref[...]).astype(ckv_ref.dtype)
    zkr = proj(OFF_KR, 2 * MLA_ROPE)
    kr_ref[...] = zkr[:, :MLA_ROPE]
    krg = zkr * gkr_ref[...]
    if latent:
        krg = _rope(krg, cosc_ref[...], sinc_ref[...], MLA_ROPE // 4)
    krg_ref[...] = krg[:, :MLA_ROPE]

    vb_ref[...] = proj(OFF_VB, GQA_KV_HEADS * GQA_HEAD_DIM).astype(vb_ref.dtype)
    u_ref[...] = proj(OFF_U, SSM_CH)


def _lspec(block_tail, layer):
    zeros = (0,) * len(block_tail)
    return pl.BlockSpec((None,) + tuple(block_tail), lambda *_: (layer,) + zeros)


def _mod_spec(latent, tm, layer):
    tiles_per_seq = DEC_SEQ // tm
    if latent:
        return pl.BlockSpec((None, None, 6, D_MODEL), lambda i, *_: (layer, 1 + i // tiles_per_seq, 0, 0))
    return pl.BlockSpec((None, None, 6, D_MODEL), lambda i, *_: (layer, 0, 0, 0))


def _in_proj(x2d, mod, latent, lw, tabs, tm, layer):
    rows = x2d.shape[0]
    n_tiles = rows // tm
    tiles_per_seq = DEC_SEQ // tm
    if latent:
        tab_map = lambda i: (i % tiles_per_seq, 0)
    else:
        tab_map = lambda i: (0, 0)
    row = lambda i: (i, 0)
    act_dt = BF16 if latent else F32
    in_specs = [
        pl.BlockSpec((tm, D_MODEL), row),
        _mod_spec(latent, tm, layer),
        _lspec((1, D_MODEL), layer),
        _lspec((D_MODEL, IN_WIDTH_P), layer),
        _lspec((1, LANES), layer),
        _lspec((1, LANES), layer),
        _lspec((1, MLA_KV_RANK), layer),
        _lspec((1, LANES), layer),
        _lspec((1, LANES), layer),
        _lspec((1, LANES), layer),
        pl.BlockSpec((tm, LANES), tab_map),
        pl.BlockSpec((tm, LANES), tab_map),
        pl.BlockSpec((tm, LANES), tab_map),
        pl.BlockSpec((tm, LANES), tab_map),
    ]
    out_shape = (
        jax.ShapeDtypeStruct((rows, SSM_CH), F32),
        jax.ShapeDtypeStruct((rows, GQA_HEADS * GQA_HEAD_DIM), BF16),
        jax.ShapeDtypeStruct((rows, GQA_KV_HEADS * GQA_HEAD_DIM), act_dt),
        jax.ShapeDtypeStruct((rows, GQA_KV_HEADS * GQA_HEAD_DIM), act_dt),
        jax.ShapeDtypeStruct((MLA_HEADS, rows, MLA_QK_PAD), BF16),
        jax.ShapeDtypeStruct((rows, MLA_KV_RANK), act_dt),
        jax.ShapeDtypeStruct((rows, MLA_ROPE), F32),
        jax.ShapeDtypeStruct((rows, MLA_ROPE), F32),
    )
    out_specs = (
        pl.BlockSpec((tm, SSM_CH), row),
        pl.BlockSpec((tm, GQA_HEADS * GQA_HEAD_DIM), row),
        pl.BlockSpec((tm, GQA_KV_HEADS * GQA_HEAD_DIM), row),
        pl.BlockSpec((tm, GQA_KV_HEADS * GQA_HEAD_DIM), row),
        pl.BlockSpec((MLA_HEADS, tm, MLA_QK_PAD), lambda i: (0, i, 0)),
        pl.BlockSpec((tm, MLA_KV_RANK), row),
        pl.BlockSpec((tm, MLA_ROPE), row),
        pl.BlockSpec((tm, MLA_ROPE), row),
    )
    return pl.pallas_call(
        functools.partial(_in_proj_kernel, latent),
        out_shape=out_shape,
        grid=(n_tiles,),
        in_specs=in_specs,
        out_specs=out_specs,
        compiler_params=_cparams(("parallel",)),
        name="in_proj_lat" if latent else "in_proj_ctx",
    )(x2d, mod, lw['norm_mix'], lw['w_in'], lw['gqa_q_norm'], lw['gqa_k_norm'], lw['mla_kv_norm'],
      lw['mla_q_nope_g'], lw['mla_q_rope_g'], lw['mla_k_rope_g'],
      tabs['cos_b'], tabs['sin_b'], tabs['cos_c'], tabs['sin_c'])


SSM_TC = 128
SSM_HALVES = 2


def _ssm_scan_kernel(n_chunks, backward, u_ref, h0_ref, lam_ref, b_ref, cre_ref, cim_ref, y_ref, hT_ref,
                     utm_ref, sre0_ref, sim0_ref, sre1_ref, sim1_ref, h0s_ref, h1s_ref):
    i = pl.program_id(1)
    tc = SSM_TC
    half_n = SSM_N // SSM_HALVES
    halves = ((sre0_ref, sim0_ref, h0s_ref), (sre1_ref, sim1_ref, h1s_ref))

    @pl.when(i == 0)
    def _():
        for k, (_, _, hk_ref) in enumerate(halves):
            hk_ref[...] = h0_ref[:, :, k * half_n:(k + 1) * half_n]

    for b in range(SUBLANES):
        for j in range(SSM_CH // LANES):
            utm_ref[j, pl.ds(b, tc, stride=SUBLANES), :] = u_ref[b, :, j * LANES:(j + 1) * LANES]

    for k, (sre_ref, sim_ref, _) in enumerate(halves):
        uk = jnp.concatenate([utm_ref[2 * k], utm_ref[2 * k + 1]], axis=-1).astype(BF16)
        r = _dot(uk, b_ref[k])
        sre_ref[...] = r[:, :half_n]
        sim_ref[...] = r[:, half_n:]

    for k, (sre_ref, sim_ref, hk_ref) in enumerate(halves):
        sl = slice(k * half_n, (k + 1) * half_n)
        lr = lam_ref[0, :, sl]
        li = lam_ref[1, :, sl]
        hr = hk_ref[0]
        hi = hk_ref[1]
        for s in range(tc):
            t = (tc - 1 - s) if backward else s
            rows = slice(t * SUBLANES, (t + 1) * SUBLANES)
            hr, hi = (lr * hr - li * hi + sre_ref[rows, :], lr * hi + li * hr + sim_ref[rows, :])
            sre_ref[rows, :] = hr
            sim_ref[rows, :] = hi
        hk_ref[0] = hr
        hk_ref[1] = hi

    for k, (sre_ref, sim_ref, _) in enumerate(halves):
        yk = _dot(sre_ref[...].astype(BF16), cre_ref[k]) - _dot(sim_ref[...].astype(BF16), cim_ref[k])
        y_ref[2 * k] = yk[:, :LANES]
        y_ref[2 * k + 1] = yk[:, LANES:]

    @pl.when(i == n_chunks - 1)
    def _():
        for k, (_, _, hk_ref) in enumerate(halves):
            hT_ref[:, :, k * half_n:(k + 1) * half_n] = hk_ref[...]


def _ssm_scan(u3, h0, sp, layer):
    bsz, t, _ = u3.shape
    n_groups = bsz // SUBLANES
    n_chunks = t // SSM_TC
    tc = SSM_TC

    def one_direction(d):
        chunk = (lambda i: n_chunks - 1 - i) if d else (lambda i: i)
        return pl.pallas_call(
            functools.partial(_ssm_scan_kernel, n_chunks, d),
            out_shape=(
                jax.ShapeDtypeStruct((n_groups, SSM_CH // LANES, t * SUBLANES, LANES), F32),
                jax.ShapeDtypeStruct((n_groups, 2, SUBLANES, SSM_N), F32),
            ),
            grid=(n_groups, n_chunks),
            in_specs=[
                pl.BlockSpec((SUBLANES, tc, SSM_CH), lambda g, i: (g, chunk(i), 0)),
                pl.BlockSpec((None, None, 2, SUBLANES, SSM_N), lambda g, i: (g, d, 0, 0, 0)),
                pl.BlockSpec((None, None, 2, SUBLANES, SSM_N), lambda g, i: (layer, d, 0, 0, 0)),
                pl.BlockSpec((None, None, 2, SSM_CH // 2, SSM_N), lambda g, i: (layer, d, 0, 0, 0)),
                pl.BlockSpec((None, None, 2, SSM_N // 2, SSM_CH // 2), lambda g, i: (layer, d, 0, 0, 0)),
                pl.BlockSpec((None, None, 2, SSM_N // 2, SSM_CH // 2), lambda g, i: (layer, d, 0, 0, 0)),
            ],
            out_specs=(
                pl.BlockSpec((None, SSM_CH // LANES, tc * SUBLANES, LANES), lambda g, i: (g, 0, chunk(i), 0)),
                pl.BlockSpec((None, 2, SUBLANES, SSM_N), lambda g, i: (g, 0, 0, 0)),
            ),
            scratch_shapes=_ssm_scratch(tc),
            compiler_params=_cparams(("parallel", "arbitrary")),
            name="ssm_scan_bwd" if d else "ssm_scan_fwd",
        )(u3, h0, sp['lam'], sp['b'], sp['c_re'], sp['c_im'])

    (y_f, h_f), (y_b, h_b) = one_direction(0), one_direction(1)
    return y_f, y_b, jnp.stack([h_f, h_b], axis=1)


def _ssm_scratch(tc):
    return [
            pltpu.VMEM((SSM_CH // LANES, tc * SUBLANES, LANES), F32),
            pltpu.VMEM((tc * SUBLANES, SSM_N // SSM_HALVES), F32),
            pltpu.VMEM((tc * SUBLANES, SSM_N // SSM_HALVES), F32),
            pltpu.VMEM((tc * SUBLANES, SSM_N // SSM_HALVES), F32),
            pltpu.VMEM((tc * SUBLANES, SSM_N // SSM_HALVES), F32),
            pltpu.VMEM((2, SUBLANES, SSM_N // SSM_HALVES), F32),
            pltpu.VMEM((2, SUBLANES, SSM_N // SSM_HALVES), F32),
        ]


def _ssm_glu_kernel(u_ref, yf_ref, yb_ref, d_ref, w_ref, o_ref, ycat_ref):
    tc = SSM_TC
    for b in range(SUBLANES):
        for j in range(SSM_CH // LANES):
            sl = slice(j * LANES, (j + 1) * LANES)
            rows_b = pl.ds(b, tc, stride=SUBLANES)
            ycat_ref[b * tc:(b + 1) * tc, sl] = (yf_ref.at[j][rows_b, :] + yb_ref.at[j][rows_b, :]
                                                + d_ref[:, sl] * u_ref[b, :, sl])
    zg = _dot(ycat_ref[...].astype(BF16), w_ref[...])
    out = zg[:, :SSM_CH] * jax.nn.sigmoid(zg[:, SSM_CH:])
    o_ref[...] = out.reshape(SUBLANES, tc, SSM_CH).astype(o_ref.dtype)


def _ssm_glu(u3, y_f, y_b, d_skip, w_glu, layer):
    bsz, t, _ = u3.shape
    n_groups = bsz // SUBLANES
    n_chunks = t // SSM_TC
    tc = SSM_TC
    yspec = pl.BlockSpec((None, SSM_CH // LANES, tc * SUBLANES, LANES), lambda g, i: (g, 0, i, 0))
    return pl.pallas_call(
        _ssm_glu_kernel,
        out_shape=jax.ShapeDtypeStruct((bsz, t, SSM_CH), BF16),
        grid=(n_groups, n_chunks),
        in_specs=[
            pl.BlockSpec((SUBLANES, tc, SSM_CH), lambda g, i: (g, i, 0)),
            yspec,
            yspec,
            _lspec((1, SSM_CH), layer),
            _lspec((SSM_CH, 2 * SSM_CH), layer),
        ],
        out_specs=pl.BlockSpec((SUBLANES, tc, SSM_CH), lambda g, i: (g, i, 0)),
        scratch_shapes=[pltpu.VMEM((SUBLANES * tc, SSM_CH), F32)],
        compiler_params=_cparams(("parallel", "parallel")),
        name="ssm_glu",
    )(u3, y_f, y_b, d_skip, w_glu)


def _ones_column(n):
    return jnp.ones((n, LANES), BF16)


def _softmax_pv(scores, v_ext):
    m = jnp.max(scores, axis=-1, keepdims=True)
    p = jnp.exp((scores - m).astype(BF16))
    o = _dot(p, v_ext)
    return o[:, :LANES] / o[:, LANES:]


ATTN_TQ = 256


def _for_each_q_block(nb, seq, body):
    nq = seq // ATTN_TQ
    for s in range(nb):
        for qi in range(nq):
            body(s, s * seq + qi * ATTN_TQ)


def _gqa_kernel(latent, nb, seq, *refs):
    if latent:
        q_ref, k_ref, v_ref, kc_ref, vc_ref, o_ref, k_s, v_s = refs
    else:
        q_ref, k_ref, v_ref, o_ref, k_s, v_s = refs
    tq = ATTN_TQ
    for s in range(nb):
        for h in range(GQA_KV_HEADS):
            hl = slice(h * LANES, (h + 1) * LANES)
            k_s[s, h, 0:seq, :] = k_ref[s * seq:(s + 1) * seq, hl].astype(BF16)
            v_s[s, h, 0:seq, :LANES] = v_ref[s * seq:(s + 1) * seq, hl].astype(BF16)
            if latent:
                k_s[s, h, seq:, :] = kc_ref[:, hl].astype(BF16)
                v_s[s, h, seq:, :LANES] = vc_ref[:, hl].astype(BF16)
            v_s[s, h, :, LANES:] = _ones_column(v_s.shape[2])

    def body(s, r0):
        for h in range(GQA_KV_HEADS):
            heads = [h * GQA_GROUP + g for g in range(GQA_GROUP)]
            q3 = jnp.concatenate([q_ref[pl.ds(r0, tq), hd * LANES:(hd + 1) * LANES] for hd in heads], axis=0)
            o = _softmax_pv(_dot_nt(q3, k_s[s, h]), v_s[s, h])
            for g, hd in enumerate(heads):
                o_ref[pl.ds(r0, tq), hd * LANES:(hd + 1) * LANES] = o[g * tq:(g + 1) * tq].astype(o_ref.dtype)

    _for_each_q_block(nb, seq, body)


def _gqa_attn(qb, kb, vb, latent, cache_k=None, cache_v=None, layer=0):
    rows = qb.shape[0]
    seq = DEC_SEQ if latent else SEQ
    nb = 1 if latent else 4
    t_all = seq + (PAST_LEN if latent else 0)
    qw = GQA_HEADS * GQA_HEAD_DIM
    kw = GQA_KV_HEADS * GQA_HEAD_DIM
    row = lambda i: (i, 0)
    in_specs = [
        pl.BlockSpec((nb * seq, qw), row),
        pl.BlockSpec((nb * seq, kw), row),
        pl.BlockSpec((nb * seq, kw), row),
    ]
    args = [qb, kb, vb]
    if latent:
        cspec = pl.BlockSpec((None, None, PAST_LEN, kw), lambda i: (i, layer, 0, 0))
        in_specs += [cspec, cspec]
        args += [cache_k, cache_v]
    return pl.pallas_call(
        functools.partial(_gqa_kernel, latent, nb, seq),
        out_shape=jax.ShapeDtypeStruct((rows, qw), BF16),
        grid=(rows // (nb * seq),),
        in_specs=in_specs,
        out_specs=pl.BlockSpec((nb * seq, qw), row),
        scratch_shapes=[
            pltpu.VMEM((nb, GQA_KV_HEADS, t_all, GQA_HEAD_DIM), BF16),
            pltpu.VMEM((nb, GQA_KV_HEADS, t_all, 2 * LANES), BF16),
        ],
        compiler_params=_cparams(("parallel",)),
        name="gqa_lat" if latent else "gqa_ctx",
    )(*args)


def _mla_kernel(latent, nb, seq, *refs):
    if latent:
        (q_ref, ckv_ref, kr_ref, krg_ref, ckvc_ref, krc_ref, wuk_ref, wuv_ref, gn_ref, gr_ref,
         o_ref, k_s, v_s) = refs
    else:
        (q_ref, ckv_ref, kr_ref, krg_ref, wuk_ref, wuv_ref, gn_ref, gr_ref, o_ref, k_s, v_s) = refs
    tq = ATTN_TQ

    def expand(ckv, kr_raw, krg, place):
        c = ckv.astype(BF16)
        kn_all = _dot(c, wuk_ref[...])
        v_all = _dot(c, wuv_ref[...])
        ss_kr = jnp.sum(kr_raw * kr_raw, axis=-1, keepdims=True)
        for h in range(MLA_HEADS):
            hl = slice(h * LANES, (h + 1) * LANES)
            kn = kn_all[:, hl]
            rs = lax.rsqrt((jnp.sum(kn * kn, axis=-1, keepdims=True) + ss_kr) * (1.0 / MLA_QK) + EPS)
            kn = (kn * rs * gn_ref[...]).astype(BF16)
            kp = (krg * rs).astype(BF16)
            vv = v_all[:, hl].astype(BF16)
            for s, lo, n, rows in place:
                k_s[s, h, lo:lo + n, :MLA_NOPE] = kn[rows]
                k_s[s, h, lo:lo + n, MLA_NOPE:MLA_QK] = kp[rows]
                k_s[s, h, lo:lo + n, MLA_QK:] = jnp.zeros((n, MLA_QK_PAD - MLA_QK), BF16)
                v_s[s, h, lo:lo + n, :MLA_V] = vv[rows]
                v_s[s, h, lo:lo + n, MLA_V:] = _ones_column(n)

    expand(ckv_ref[...], kr_ref[...], krg_ref[...],
           [(s, 0, seq, slice(s * seq, (s + 1) * seq)) for s in range(nb)])
    if latent:
        krc = krc_ref[...]
        expand(ckvc_ref[...], krc, krc * gr_ref[...], [(0, seq, PAST_LEN, slice(0, PAST_LEN))])

    def body(s, r0):
        for h in range(MLA_HEADS):
            sc = _dot_nt(q_ref[h, pl.ds(r0, tq), :], k_s[s, h])
            o_ref[pl.ds(r0, tq), h * LANES:(h + 1) * LANES] = _softmax_pv(sc, v_s[s, h]).astype(o_ref.dtype)

    _for_each_q_block(nb, seq, body)


def _mla_attn(qc, ckv, kr, krg, lw, latent, cache_ckv=None, cache_kr=None, layer=0):
    rows = qc.shape[1]
    seq = DEC_SEQ if latent else SEQ
    nb = 1 if latent else 4
    t_all = seq + (PAST_LEN if latent else 0)
    row = lambda i: (i, 0)
    in_specs = [
        pl.BlockSpec((MLA_HEADS, nb * seq, MLA_QK_PAD), lambda i: (0, i, 0)),
        pl.BlockSpec((nb * seq, MLA_KV_RANK), row),
        pl.BlockSpec((nb * seq, MLA_ROPE), row),
        pl.BlockSpec((nb * seq, MLA_ROPE), row),
    ]
    args = [qc, ckv, kr, krg]
    if latent:
        in_specs += [
            pl.BlockSpec((None, None, PAST_LEN, MLA_KV_RANK), lambda i: (i, layer, 0, 0)),
            pl.BlockSpec((None, None, PAST_LEN, MLA_ROPE), lambda i: (i, layer, 0, 0)),
        ]
        args += [cache_ckv, cache_kr]
    in_specs += [
        _lspec((MLA_KV_RANK, MLA_HEADS * MLA_NOPE), layer),
        _lspec((MLA_KV_RANK, MLA_HEADS * MLA_V), layer),
        _lspec((1, MLA_NOPE), layer),
        _lspec((1, MLA_ROPE), layer),
    ]
    args += [lw['mla_w_uk'], lw['mla_w_uv'], lw['mla_k_nope_g'], lw['mla_k_rope_g64']]
    return pl.pallas_call(
        functools.partial(_mla_kernel, latent, nb, seq),
        out_shape=jax.ShapeDtypeStruct((rows, MLA_HEADS * MLA_V), BF16),
        grid=(rows // (nb * seq),),
        in_specs=in_specs,
        out_specs=pl.BlockSpec((nb * seq, MLA_HEADS * MLA_V), row),
        scratch_shapes=[
            pltpu.VMEM((nb, MLA_HEADS, t_all, MLA_QK_PAD), BF16),
            pltpu.VMEM((nb, MLA_HEADS, t_all, MLA_V + LANES), BF16),
        ],
        compiler_params=_cparams(("parallel",)),
        name="mla_lat" if latent else "mla_ctx",
    )(*args)


def _out_proj_kernel(ya_ref, yb_ref, yc_ref, x_ref, mod_ref, w_ref, g_ref, x1_ref, h2_ref):
    wa = SSM_CH
    wb = wa + GQA_HEADS * GQA_HEAD_DIM
    piece = x_ref.shape[0] // ROW_PIECES
    pieces = [slice(r * piece, (r + 1) * piece) for r in range(ROW_PIECES)]
    outs = [(_dot(ya_ref[rs, :], w_ref[0:wa, :]) + _dot(yb_ref[rs, :], w_ref[wa:wb, :])
             + _dot(yc_ref[rs, :], w_ref[wb:, :])) for rs in pieces]
    for rs, o in zip(pieces, outs):
        x1 = x_ref[rs, :] + mod_ref[2:3, :] * o
        x1_ref[rs, :] = x1
        ms = jnp.mean(x1 * x1, axis=-1, keepdims=True)
        y = x1 * lax.rsqrt(ms + EPS) * g_ref[...]
        h2_ref[rs, :] = (y * (1.0 + mod_ref[4:5, :]) + mod_ref[3:4, :]).astype(h2_ref.dtype)


def _out_proj(ya, yb, yc, x2d, mod, lw, latent, tm, layer):
    rows = x2d.shape[0]
    row = lambda i: (i, 0)
    return pl.pallas_call(
        _out_proj_kernel,
        out_shape=(jax.ShapeDtypeStruct((rows, D_MODEL), F32), jax.ShapeDtypeStruct((rows, D_MODEL), BF16)),
        grid=(rows // tm,),
        in_specs=[
            pl.BlockSpec((tm, SSM_CH), row),
            pl.BlockSpec((tm, GQA_HEADS * GQA_HEAD_DIM), row),
            pl.BlockSpec((tm, MLA_HEADS * MLA_V), row),
            pl.BlockSpec((tm, D_MODEL), row),
            _mod_spec(latent, tm, layer),
            _lspec((D_MODEL, D_MODEL), layer),
            _lspec((1, D_MODEL), layer),
        ],
        out_specs=(pl.BlockSpec((tm, D_MODEL), row), pl.BlockSpec((tm, D_MODEL), row)),
        compiler_params=_cparams(("parallel",)),
        name="out_proj",
    )(ya, yb, yc, x2d, mod, lw['w_out'], lw['norm_mlp'])


def _mlp_kernel(h2_ref, x1_ref, mod_ref, w1_ref, w2_ref, o_ref):
    j = pl.program_id(1)

    @pl.when(j == 0)
    def _():
        o_ref[...] = jnp.zeros_like(o_ref)

    f = jnp.maximum(_dot(h2_ref[...], w1_ref[...]), 0.0)
    o_ref[...] += _dot((f * f).astype(BF16), w2_ref[...])

    @pl.when(j == pl.num_programs(1) - 1)
    def _():
        o_ref[...] = x1_ref[...] + mod_ref[5:6, :] * o_ref[...]


def _mlp(h2, x1, mod, lw, latent, tm, tf, layer):
    rows = h2.shape[0]
    return pl.pallas_call(
        _mlp_kernel,
        out_shape=jax.ShapeDtypeStruct((rows, D_MODEL), F32),
        grid=(rows // tm, D_FF // tf),
        in_specs=[
            pl.BlockSpec((tm, D_MODEL), lambda i, j: (i, 0)),
            pl.BlockSpec((tm, D_MODEL), lambda i, j: (i, 0), pipeline_mode=pl.Buffered(1)),
            _mod_spec(latent, tm, layer),
            pl.BlockSpec((None, D_MODEL, tf), lambda i, j: (layer, 0, j)),
            pl.BlockSpec((None, tf, D_MODEL), lambda i, j: (layer, j, 0)),
        ],
        out_specs=pl.BlockSpec((tm, D_MODEL), lambda i, j: (i, 0)),
        compiler_params=_cparams(("parallel", "arbitrary")),
        name="mlp",
    )(h2, x1, mod, lw['w_ff1'], lw['w_ff2'])


def _permute_w_in(w):
    base = OFF_QCN
    ckv0 = base + MLA_HEADS * MLA_QK
    qc = w[:, :, base:ckv0].reshape(DEPTH, D_MODEL, MLA_HEADS, MLA_QK)
    nope = qc[..., :MLA_NOPE].reshape(DEPTH, D_MODEL, MLA_HEADS * MLA_NOPE)
    rope = qc[..., MLA_NOPE:].reshape(DEPTH, D_MODEL, MLA_HEADS * MLA_ROPE)
    kr = w[:, :, ckv0 + MLA_KV_RANK:]
    parts = [w[:, :, :base], nope, rope, w[:, :, ckv0:ckv0 + MLA_KV_RANK], kr, kr]
    return jnp.concatenate(parts, axis=2).astype(BF16)


def _rope_tables(seq):
    t = jnp.arange(seq)
    row = (t // GRID_W).astype(F32)
    col = (t % GRID_W).astype(F32)

    def table(d):
        quarter = d // 4
        inv = ROPE_BASE ** (-(jnp.arange(quarter, dtype=F32) / quarter))
        ar = row[:, None] * inv[None, :]
        ac = col[:, None] * inv[None, :]
        cos = jnp.concatenate([jnp.cos(ar), jnp.cos(ar), jnp.cos(ac), jnp.cos(ac)], axis=-1)
        sin = jnp.concatenate([-jnp.sin(ar), jnp.sin(ar), -jnp.sin(ac), jnp.sin(ac)], axis=-1)
        reps = LANES // d
        return jnp.tile(cos, (1, reps)), jnp.tile(sin, (1, reps))

    cos_b, sin_b = table(GQA_HEAD_DIM)
    cos_c, sin_c = table(MLA_ROPE)
    return {'cos_b': cos_b, 'sin_b': sin_b, 'cos_c': cos_c, 'sin_c': sin_c}


def _ssm_params(lam_re, lam_im, log_dt, b_re, b_im, c_re, c_im):
    a = lam_re.astype(F32)
    w = lam_im.astype(F32)
    dt = jnp.exp(log_dt.astype(F32))[..., None]
    mag = jnp.exp(a * dt)
    lbr = mag * jnp.cos(w * dt)
    lbi = mag * jnp.sin(w * dt)
    den = a * a + w * w
    cr = (((lbr - 1.0) * a + lbi * w) / den)[..., None]
    ci = ((lbi * a - (lbr - 1.0) * w) / den)[..., None]
    bre = b_re.astype(F32)
    bim = b_im.astype(F32)
    bb_re = cr * bre - ci * bim
    bb_im = cr * bim + ci * bre
    lam_ri = jnp.stack([lbr.reshape(DEPTH, 2, SSM_N), lbi.reshape(DEPTH, 2, SSM_N)], axis=2)
    lam_b = jnp.broadcast_to(lam_ri[:, :, :, None, :], (DEPTH, 2, 2, SUBLANES, SSM_N))
    gh = SSM_GROUPS // 2
    eye = jnp.eye(gh, dtype=F32)[:, None, :, None]

    def blockdiag(x, rows_per_g, cols_per_g):
        x = x.reshape(DEPTH, 2, 2, gh, rows_per_g, 1, cols_per_g) * eye
        return x.reshape(DEPTH, 2, 2, gh * rows_per_g, gh * cols_per_g)

    def bmat(x):
        return blockdiag(jnp.swapaxes(x, -1, -2), SSM_GROUP, SSM_STATE)

    def cmat(x):
        return blockdiag(jnp.swapaxes(x, -1, -2), SSM_STATE, SSM_GROUP)

    b_cat = jnp.concatenate([bmat(bb_re), bmat(bb_im)], axis=-1).astype(BF16)
    return {'lam': lam_b, 'b': b_cat, 'c_re': cmat(c_re.astype(F32)).astype(BF16),
            'c_im': cmat(c_im.astype(F32)).astype(BF16)}


def _trunk_layer(x2d, mod, lw, sp, tabs, latent, ctx, layer):
    seq = DEC_SEQ if latent else SEQ
    bsz = x2d.shape[0] // seq
    u, qb, kb, vb, qc, ckv, kr, krg = _in_proj(x2d, mod, latent, lw, tabs, TM_PROJ, layer)

    u3 = u.reshape(bsz, seq, SSM_CH)
    y_f, y_b, h_t = _ssm_scan(u3, ctx['h0'], sp, layer)
    ya = _ssm_glu(u3, y_f, y_b, lw['ssm_d'], lw['ssm_w_glu'], layer).reshape(bsz * seq, SSM_CH)

    if latent:
        yb = _gqa_attn(qb, kb, vb, True, ctx['k'], ctx['v'], layer)
        yc = _mla_attn(qc, ckv, kr, krg, lw, True, ctx['ckv'], ctx['kr'], layer)
    else:
        yb = _gqa_attn(qb, kb, vb, False)
        yc = _mla_attn(qc, ckv, kr, krg, lw, False, layer=layer)

    x1, h2 = _out_proj(ya, yb, yc, x2d, mod, lw, latent, TM_PROJ, layer)
    x2 = _mlp(h2, x1, mod, lw, latent, TM_MLP, TF_MLP, layer)
    return x2, (kb, vb, ckv, kr, h_t)


def kernel(x_prompt, x_sample, cache_attn_k, cache_attn_v, cache_mla_ckv, cache_mla_krope, state_ssm, c, c_ctx, w_mod, b_mod, norm_mix, norm_mlp, w_in, gqa_q_norm, gqa_k_norm, mla_kv_norm, mla_q_norm, mla_k_norm, mla_w_uk, mla_w_uv, ssm_lam_re, ssm_lam_im, ssm_log_dt, ssm_b_re, ssm_b_im, ssm_c_re, ssm_c_im, ssm_d, ssm_w_glu, w_out, w_ff1, w_ff2):
    cvec = jnp.zeros((N_MOD, D_MODEL), F32).at[0].set(c_ctx).at[1:1 + DEC_BATCH].set(c)
    mod = _adaln(cvec, w_mod, b_mod.reshape(DEPTH, 1, 6 * D_MODEL)).reshape(DEPTH, N_MOD, 6, D_MODEL)

    tabs = _rope_tables(DEC_SEQ)
    cache_k = cache_attn_k.reshape(DEC_BATCH, DEPTH, PAST_LEN, GQA_KV_HEADS * GQA_HEAD_DIM)
    cache_v = cache_attn_v.reshape(DEC_BATCH, DEPTH, PAST_LEN, GQA_KV_HEADS * GQA_HEAD_DIM)

    xp = x_prompt.reshape(BATCH * SEQ, D_MODEL)
    xs = x_sample.reshape(DEC_BATCH * DEC_SEQ, D_MODEL)
    new_k, new_v, new_ckv, new_kr, new_ssm = [], [], [], [], []
    h0_zero = jnp.zeros((BATCH // SUBLANES, 2, 2, SUBLANES, SSM_N), F32)

    row3 = lambda a: a.reshape(DEPTH, 1, -1)
    dup = lambda a: jnp.concatenate([a, a], axis=-1)
    lw = {
        'norm_mix': row3(norm_mix), 'norm_mlp': row3(norm_mlp),
        'w_in': _permute_w_in(w_in),
        'w_out': w_out.astype(BF16), 'w_ff1': w_ff1.astype(BF16), 'w_ff2': w_ff2.astype(BF16),
        'gqa_q_norm': row3(gqa_q_norm), 'gqa_k_norm': row3(gqa_k_norm), 'mla_kv_norm': row3(mla_kv_norm),
        'mla_q_nope_g': row3(mla_q_norm[:, :MLA_NOPE]),
        'mla_q_rope_g': row3(dup(mla_q_norm[:, MLA_NOPE:])),
        'mla_k_nope_g': row3(mla_k_norm[:, :MLA_NOPE]),
        'mla_k_rope_g': row3(dup(mla_k_norm[:, MLA_NOPE:])),
        'mla_k_rope_g64': row3(mla_k_norm[:, MLA_NOPE:]),
        'mla_w_uk': mla_w_uk.reshape(DEPTH, MLA_KV_RANK, MLA_HEADS * MLA_NOPE).astype(BF16),
        'mla_w_uv': mla_w_uv.reshape(DEPTH, MLA_KV_RANK, MLA_HEADS * MLA_V).astype(BF16),
        'ssm_d': row3(ssm_d), 'ssm_w_glu': ssm_w_glu.astype(BF16),
    }
    sp = _ssm_params(ssm_lam_re, ssm_lam_im, ssm_log_dt, ssm_b_re, ssm_b_im, ssm_c_re, ssm_c_im)
    h0_lat = state_ssm.reshape(DEC_BATCH, DEPTH, 2, SSM_N, 2).transpose(1, 2, 4, 0, 3)[:, None]

    for l in range(DEPTH):
        xp, (k, v, ckv_n, kr, h_t) = _trunk_layer(xp, mod, lw, sp, tabs, False, {'h0': h0_zero}, l)
        new_k.append(k.reshape(BATCH, SEQ, GQA_KV_HEADS, GQA_HEAD_DIM))
        new_v.append(v.reshape(BATCH, SEQ, GQA_KV_HEADS, GQA_HEAD_DIM))
        new_ckv.append(ckv_n.reshape(BATCH, SEQ, MLA_KV_RANK))
        new_kr.append(kr.reshape(BATCH, SEQ, MLA_ROPE))
        hs = h_t.transpose(0, 3, 1, 4, 2).reshape(BATCH, 2, SSM_GROUPS, SSM_STATE, 2)
        new_ssm.append(hs)

        ctx = {'k': cache_k, 'v': cache_v, 'ckv': cache_mla_ckv, 'kr': cache_mla_krope, 'h0': h0_lat[l]}
        xs, _ = _trunk_layer(xs, mod, lw, sp, tabs, True, ctx, l)

    return (xp.reshape(BATCH, SEQ, D_MODEL), xs.reshape(DEC_BATCH, DEC_SEQ, D_MODEL),
            jnp.stack(new_k, axis=1), jnp.stack(new_v, axis=1), jnp.stack(new_ckv, axis=1),
            jnp.stack(new_kr, axis=1), jnp.stack(new_ssm, axis=1))
```

```python
import functools
import math

import jax
import jax.numpy as jnp
from jax import lax
from jax.experimental import pallas as pl
from jax.experimental.pallas import tpu as pltpu

D_MODEL = 2048
BATCH = 32
SEQ = 256
DEPTH = 2
DEC_BATCH = 8
DEC_SEQ = 1024
PAST_LEN = 512
GRID_W = 64
ROPE_BASE = 10000.0
EPS = 1e-6
SSM_CH = 512
SSM_GROUP = 16
SSM_GROUPS = SSM_CH // SSM_GROUP
SSM_STATE = 64
SSM_N = SSM_GROUPS * SSM_STATE
GQA_HEADS = 6
GQA_KV_HEADS = 2
GQA_GROUP = GQA_HEADS // GQA_KV_HEADS
GQA_HEAD_DIM = 128
MLA_HEADS = 6
MLA_NOPE = 128
MLA_ROPE = 64
MLA_QK = MLA_NOPE + MLA_ROPE
MLA_QK_PAD = 256
MLA_V = 128
MLA_KV_RANK = 512
D_FF = 4 * D_MODEL
N_MOD = 16

OFF_U = 0
OFF_QB = OFF_U + SSM_CH
OFF_KB = OFF_QB + GQA_HEADS * GQA_HEAD_DIM
OFF_VB = OFF_KB + GQA_KV_HEADS * GQA_HEAD_DIM
OFF_QCN = OFF_VB + GQA_KV_HEADS * GQA_HEAD_DIM
OFF_QCR = OFF_QCN + MLA_HEADS * MLA_NOPE
OFF_CKV = OFF_QCR + MLA_HEADS * MLA_ROPE
OFF_KR = OFF_CKV + MLA_KV_RANK
IN_WIDTH_P = OFF_KR + 2 * MLA_ROPE

LANES = 128
SUBLANES = 8
VMEM_LIMIT = 56 * 1024 * 1024

TM_PROJ = 512
TM_MLP = 1024
TF_MLP = 512
ROW_PIECES = 2
IN_PROJ_PIECES = 2

BF16 = jnp.bfloat16
F32 = jnp.float32


def _cparams(sem):
    return pltpu.CompilerParams(dimension_semantics=sem, vmem_limit_bytes=VMEM_LIMIT)


def _dot(a, b):
    return jnp.dot(a, b, preferred_element_type=F32)


def _dot_nt(a, b):
    return lax.dot_general(a, b, (((1,), (1,)), ((), ())), preferred_element_type=F32)


def _adaln_kernel(c_ref, w_ref, b_ref, o_ref):
    c = c_ref[...]
    s = (c * jax.nn.sigmoid(c)).astype(BF16)
    o_ref[...] = _dot(s, w_ref[...].astype(BF16)) + b_ref[...]


def _adaln(cvec, w_mod, b_mod):
    tn = 1024
    return pl.pallas_call(
        _adaln_kernel,
        out_shape=jax.ShapeDtypeStruct((DEPTH, N_MOD, 6 * D_MODEL), F32),
        grid=(DEPTH, 6 * D_MODEL // tn),
        in_specs=[
            pl.BlockSpec((N_MOD, D_MODEL), lambda l, j: (0, 0)),
            pl.BlockSpec((None, D_MODEL, tn), lambda l, j: (l, 0, j)),
            pl.BlockSpec((None, 1, tn), lambda l, j: (l, 0, j)),
        ],
        out_specs=pl.BlockSpec((None, N_MOD, tn), lambda l, j: (l, 0, j)),
        compiler_params=_cparams(("parallel", "parallel")),
        name="adaln",
    )(cvec, w_mod, b_mod)


def _swap_halves(x, block):
    lane = lax.broadcasted_iota(jnp.int32, x.shape, 1)
    first = (lane % (2 * block)) < block
    return jnp.where(first, pltpu.roll(x, LANES - block, 1), pltpu.roll(x, block, 1))


def _rope(x, cos, sin_signed, block):
    return x * cos + _swap_halves(x, block) * sin_signed


def _in_proj_kernel(latent, x_ref, mod_ref, gmix_ref, w_ref, gq_ref, gk_ref, gkv_ref, gqn_ref, gqr_ref, gkr_ref,
                    cosb_ref, sinb_ref, cosc_ref, sinc_ref,
                    u_ref, qb_ref, kb_ref, vb_ref, qc_ref, ckv_ref, kr_ref, krg_ref):
    piece = x_ref.shape[0] // IN_PROJ_PIECES
    for r in range(IN_PROJ_PIECES):
        rs = pl.ds(r * piece, piece)
        _in_proj_rows(latent, x_ref.at[rs], mod_ref, gmix_ref, w_ref, gq_ref, gk_ref, gkv_ref, gqn_ref, gqr_ref,
                      gkr_ref, cosb_ref.at[rs], sinb_ref.at[rs], cosc_ref.at[rs], sinc_ref.at[rs],
                      u_ref.at[rs], qb_ref.at[rs], kb_ref.at[rs], vb_ref.at[rs], qc_ref.at[:, rs], ckv_ref.at[rs],
                      kr_ref.at[rs], krg_ref.at[rs])


def _in_proj_rows(latent, x_ref, mod_ref, gmix_ref, w_ref, gq_ref, gk_ref, gkv_ref, gqn_ref, gqr_ref, gkr_ref,
                  cosb_ref, sinb_ref, cosc_ref, sinc_ref,
                  u_ref, qb_ref, kb_ref, vb_ref, qc_ref, ckv_ref, kr_ref, krg_ref):
    x = x_ref[...]
    ms = jnp.mean(x * x, axis=-1, keepdims=True)
    y = x * lax.rsqrt(ms + EPS) * gmix_ref[...]
    h = (y * (1.0 + mod_ref[1:2, :]) + mod_ref[0:1, :]).astype(BF16)

    def proj(off, width):
        return _dot(h, w_ref[:, off:off + width])

    scale_b = 1.0 / math.sqrt(GQA_HEAD_DIM)
    zq = proj(OFF_QB, GQA_HEADS * GQA_HEAD_DIM)
    for hd in range(GQA_HEADS):
        col = zq[:, hd * LANES:(hd + 1) * LANES]
        q = col * lax.rsqrt(jnp.mean(col * col, axis=-1, keepdims=True) + EPS) * gq_ref[...]
        if latent:
            q = _rope(q, cosb_ref[...], sinb_ref[...], GQA_HEAD_DIM // 4)
        qb_ref[:, hd * LANES:(hd + 1) * LANES] = (q * scale_b).astype(qb_ref.dtype)
    zk = proj(OFF_KB, GQA_KV_HEADS * GQA_HEAD_DIM)
    for hd in range(GQA_KV_HEADS):
        col = zk[:, hd * LANES:(hd + 1) * LANES]
        k = col * lax.rsqrt(jnp.mean(col * col, axis=-1, keepdims=True) + EPS) * gk_ref[...]
        if latent:
            k = _rope(k, cosb_ref[...], sinb_ref[...], GQA_HEAD_DIM // 4)
        kb_ref[:, hd * LANES:(hd + 1) * LANES] = k.astype(kb_ref.dtype)

    scale_c = 1.0 / math.sqrt(MLA_QK)
    zn = proj(OFF_QCN, MLA_HEADS * MLA_NOPE)
    zr = proj(OFF_QCR, MLA_HEADS * MLA_ROPE)
    lane = lax.broadcasted_iota(jnp.int32, (x.shape[0], LANES), 1)
    low = lane < MLA_ROPE
    for pair in range(MLA_HEADS // 2):
        colr = zr[:, pair * LANES:(pair + 1) * LANES]
        sq = colr * colr
        ss_lo = jnp.sum(jnp.where(low, sq, 0.0), axis=-1, keepdims=True)
        ss_hi = jnp.sum(jnp.where(low, 0.0, sq), axis=-1, keepdims=True)
        rs = []
        for half, ss_r in ((0, ss_lo), (1, ss_hi)):
            hd = 2 * pair + half
            coln = zn[:, hd * LANES:(hd + 1) * LANES]
            ss = jnp.sum(coln * coln, axis=-1, keepdims=True) + ss_r
            r = lax.rsqrt(ss * (1.0 / MLA_QK) + EPS)
            rs.append(r)
            qc_ref[hd, :, :MLA_NOPE] = (coln * r * gqn_ref[...] * scale_c).astype(qc_ref.dtype)
        qr = colr * jnp.where(low, rs[0], rs[1]) * gqr_ref[...]
        if latent:
            qr = _rope(qr, cosc_ref[...], sinc_ref[...], MLA_ROPE // 4)
        qr = qr * scale_c
        zeros = jnp.zeros((x.shape[0], MLA_QK_PAD - MLA_QK), qc_ref.dtype)
        for half in range(2):
            hd = 2 * pair + half
            qc_ref[hd, :, MLA_NOPE:MLA_QK] = qr[:, half * MLA_ROPE:(half + 1) * MLA_ROPE].astype(qc_ref.dtype)
            qc_ref[hd, :, MLA_QK:] = zeros

    zc = proj(OFF_CKV, MLA_KV_RANK)
    ckv_ref[...] = (zc * lax.rsqrt(jnp.mean(zc * zc, axis=-1, keepdims=True) + EPS) * gkv_ref[...]).astype(ckv_ref.dtype)
    zkr = proj(OFF_KR, 2 * MLA_ROPE)
    kr_ref[...] = zkr[:, :MLA_ROPE]
    krg = zkr * gkr_ref[...]
    if latent:
        krg = _rope(krg, cosc_ref[...], sinc_ref[...], MLA_ROPE // 4)
    krg_ref[...] = krg[:, :MLA_ROPE]

    vb_ref[...] = proj(OFF_VB, GQA_KV_HEADS * GQA_HEAD_DIM).astype(vb_ref.dtype)
    u_ref[...] = proj(OFF_U, SSM_CH)


def _lspec(block_tail, layer):
    zeros = (0,) * len(block_tail)
    return pl.BlockSpec((None,) + tuple(block_tail), lambda *_: (layer,) + zeros)


def _mod_spec(latent, tm, layer):
    tiles_per_seq = DEC_SEQ // tm
    if latent:
        return pl.BlockSpec((None, None, 6, D_MODEL), lambda i, *_: (layer, 1 + i // tiles_per_seq, 0, 0))
    return pl.BlockSpec((None, None, 6, D_MODEL), lambda i, *_: (layer, 0, 0, 0))


def _in_proj(x2d, mod, latent, lw, tabs, tm, layer):
    rows = x2d.shape[0]
    n_tiles = rows // tm
    tiles_per_seq = DEC_SEQ // tm
    if latent:
        tab_map = lambda i: (i % tiles_per_seq, 0)
    else:
        tab_map = lambda i: (0, 0)
    row = lambda i: (i, 0)
    act_dt = BF16 if latent else F32
    in_specs = [
        pl.BlockSpec((tm, D_MODEL), row),
        _mod_spec(latent, tm, layer),
        _lspec((1, D_MODEL), layer),
        _lspec((D_MODEL, IN_WIDTH_P), layer),
        _lspec((1, LANES), layer),
        _lspec((1, LANES), layer),
        _lspec((1, MLA_KV_RANK), layer),
        _lspec((1, LANES), layer),
        _lspec((1, LANES), layer),
        _lspec((1, LANES), layer),
        pl.BlockSpec((tm, LANES), tab_map),
        pl.BlockSpec((tm, LANES), tab_map),
        pl.BlockSpec((tm, LANES), tab_map),
        pl.BlockSpec((tm, LANES), tab_map),
    ]
    out_shape = (
        jax.ShapeDtypeStruct((rows, SSM_CH), F32),
        jax.ShapeDtypeStruct((rows, GQA_HEADS * GQA_HEAD_DIM), BF16),
        jax.ShapeDtypeStruct((rows, GQA_KV_HEADS * GQA_HEAD_DIM), act_dt),
        jax.ShapeDtypeStruct((rows, GQA_KV_HEADS * GQA_HEAD_DIM), act_dt),
        jax.ShapeDtypeStruct((MLA_HEADS, rows, MLA_QK_PAD), BF16),
        jax.ShapeDtypeStruct((rows, MLA_KV_RANK), act_dt),
        jax.ShapeDtypeStruct((rows, MLA_ROPE), F32),
        jax.ShapeDtypeStruct((rows, MLA_ROPE), F32),
    )
    out_specs = (
        pl.BlockSpec((tm, SSM_CH), row),
        pl.BlockSpec((tm, GQA_HEADS * GQA_HEAD_DIM), row),
        pl.BlockSpec((tm, GQA_KV_HEADS * GQA_HEAD_DIM), row),
        pl.BlockSpec((tm, GQA_KV_HEADS * GQA_HEAD_DIM), row),
        pl.BlockSpec((MLA_HEADS, tm, MLA_QK_PAD), lambda i: (0, i, 0)),
        pl.BlockSpec((tm, MLA_KV_RANK), row),
        pl.BlockSpec((tm, MLA_ROPE), row),
        pl.BlockSpec((tm, MLA_ROPE), row),
    )
    return pl.pallas_call(
        functools.partial(_in_proj_kernel, latent),
        out_shape=out_shape,
        grid=(n_tiles,),
        in_specs=in_specs,
        out_specs=out_specs,
        compiler_params=_cparams(("parallel",)),
        name="in_proj_lat" if latent else "in_proj_ctx",
    )(x2d, mod, lw['norm_mix'], lw['w_in'], lw['gqa_q_norm'], lw['gqa_k_norm'], lw['mla_kv_norm'],
      lw['mla_q_nope_g'], lw['mla_q_rope_g'], lw['mla_k_rope_g'],
      tabs['cos_b'], tabs['sin_b'], tabs['cos_c'], tabs['sin_c'])


SSM_TC = 128
SSM_HALVES = 2


def _ssm_scan_kernel(n_chunks, backward, u_ref, h0_ref, lam_ref, b_ref, cre_ref, cim_ref, y_ref, hT_ref,
                     utm_ref, sre0_ref, sim0_ref, sre1_ref, sim1_ref, h0s_ref, h1s_ref):
    i = pl.program_id(1)
    tc = SSM_TC
    half_n = SSM_N // SSM_HALVES
    halves = ((sre0_ref, sim0_ref, h0s_ref), (sre1_ref, sim1_ref, h1s_ref))

    @pl.when(i == 0)
    def _():
        for k, (_, _, hk_ref) in enumerate(halves):
            hk_ref[...] = h0_ref[:, :, k * half_n:(k + 1) * half_n]

    for b in range(SUBLANES):
        for j in range(SSM_CH // LANES):
            utm_ref[j, pl.ds(b, tc, stride=SUBLANES), :] = u_ref[b, :, j * LANES:(j + 1) * LANES]

    for k, (sre_ref, sim_ref, _) in enumerate(halves):
        uk = jnp.concatenate([utm_ref[2 * k], utm_ref[2 * k + 1]], axis=-1).astype(BF16)
        r = _dot(uk, b_ref[k])
        sre_ref[...] = r[:, :half_n]
        sim_ref[...] = r[:, half_n:]

    for k, (sre_ref, sim_ref, hk_ref) in enumerate(halves):
        sl = slice(k * half_n, (k + 1) * half_n)
        lr = lam_ref[0, :, sl]
        li = lam_ref[1, :, sl]
        hr = hk_ref[0]
        hi = hk_ref[1]
        for s in range(tc):
            t = (tc - 1 - s) if backward else s
            rows = slice(t * SUBLANES, (t + 1) * SUBLANES)
            hr, hi = (lr * hr - li * hi + sre_ref[rows, :], lr * hi + li * hr + sim_ref[rows, :])
            sre_ref[rows, :] = hr
            sim_ref[rows, :] = hi
        hk_ref[0] = hr
        hk_ref[1] = hi

    for k, (sre_ref, sim_ref, _) in enumerate(halves):
        yk = _dot(sre_ref[...].astype(BF16), cre_ref[k]) - _dot(sim_ref[...].astype(BF16), cim_ref[k])
        y_ref[2 * k] = yk[:, :LANES]
        y_ref[2 * k + 1] = yk[:, LANES:]

    @pl.when(i == n_chunks - 1)
    def _():
        for k, (_, _, hk_ref) in enumerate(halves):
            hT_ref[:, :, k * half_n:(k + 1) * half_n] = hk_ref[...]


def _ssm_scan(u3, h0, sp, layer):
    bsz, t, _ = u3.shape
    n_groups = bsz // SUBLANES
    n_chunks = t // SSM_TC
    tc = SSM_TC

    def one_direction(d):
        chunk = (lambda i: n_chunks - 1 - i) if d else (lambda i: i)
        return pl.pallas_call(
            functools.partial(_ssm_scan_kernel, n_chunks, d),
            out_shape=(
                jax.ShapeDtypeStruct((n_groups, SSM_CH // LANES, t * SUBLANES, LANES), F32),
                jax.ShapeDtypeStruct((n_groups, 2, SUBLANES, SSM_N), F32),
            ),
            grid=(n_groups, n_chunks),
            in_specs=[
                pl.BlockSpec((SUBLANES, tc, SSM_CH), lambda g, i: (g, chunk(i), 0)),
                pl.BlockSpec((None, None, 2, SUBLANES, SSM_N), lambda g, i: (g, d, 0, 0, 0)),
                pl.BlockSpec((None, None, 2, SUBLANES, SSM_N), lambda g, i: (layer, d, 0, 0, 0)),
                pl.BlockSpec((None, None, 2, SSM_CH // 2, SSM_N), lambda g, i: (layer, d, 0, 0, 0)),
                pl.BlockSpec((None, None, 2, SSM_N // 2, SSM_CH // 2), lambda g, i: (layer, d, 0, 0, 0)),
                pl.BlockSpec((None, None, 2, SSM_N // 2, SSM_CH // 2), lambda g, i: (layer, d, 0, 0, 0)),
            ],
            out_specs=(
                pl.BlockSpec((None, SSM_CH // LANES, tc * SUBLANES, LANES), lambda g, i: (g, 0, chunk(i), 0)),
                pl.BlockSpec((None, 2, SUBLANES, SSM_N), lambda g, i: (g, 0, 0, 0)),
            ),
            scratch_shapes=_ssm_scratch(tc),
            compiler_params=_cparams(("parallel", "arbitrary")),
            name="ssm_scan_bwd" if d else "ssm_scan_fwd",
        )(u3, h0, sp['lam'], sp['b'], sp['c_re'], sp['c_im'])

    (y_f, h_f), (y_b, h_b) = one_direction(0), one_direction(1)
    return y_f, y_b, jnp.stack([h_f, h_b], axis=1)


def _ssm_scratch(tc):
    return [
            pltpu.VMEM((SSM_CH // LANES, tc * SUBLANES, LANES), F32),
            pltpu.VMEM((tc * SUBLANES, SSM_N // SSM_HALVES), F32),
            pltpu.VMEM((tc * SUBLANES, SSM_N // SSM_HALVES), F32),
            pltpu.VMEM((tc * SUBLANES, SSM_N // SSM_HALVES), F32),
            pltpu.VMEM((tc * SUBLANES, SSM_N // SSM_HALVES), F32),
            pltpu.VMEM((2, SUBLANES, SSM_N // SSM_HALVES), F32),
            pltpu.VMEM((2, SUBLANES, SSM_N // SSM_HALVES), F32),
        ]


def _ssm_glu_kernel(u_ref, yf_ref, yb_ref, d_ref, w_ref, o_ref, ycat_ref):
    tc = SSM_TC
    for b in range(SUBLANES):
        for j in range(SSM_CH // LANES):
            sl = slice(j * LANES, (j + 1) * LANES)
            rows_b = pl.ds(b, tc, stride=SUBLANES)
            ycat_ref[b * tc:(b + 1) * tc, sl] = (yf_ref.at[j][rows_b, :] + yb_ref.at[j][rows_b, :]
                                                + d_ref[:, sl] * u_ref[b, :, sl])
    zg = _dot(ycat_ref[...].astype(BF16), w_ref[...])
    out = zg[:, :SSM_CH] * jax.nn.sigmoid(zg[:, SSM_CH:])
    o_ref[...] = out.reshape(SUBLANES, tc, SSM_CH).astype(o_ref.dtype)


def _ssm_glu(u3, y_f, y_b, d_skip, w_glu, layer):
    bsz, t, _ = u3.shape
    n_groups = bsz // SUBLANES
    n_chunks = t // SSM_TC
    tc = SSM_TC
    yspec = pl.BlockSpec((None, SSM_CH // LANES, tc * SUBLANES, LANES), lambda g, i: (g, 0, i, 0))
    return pl.pallas_call(
        _ssm_glu_kernel,
        out_shape=jax.ShapeDtypeStruct((bsz, t, SSM_CH), BF16),
        grid=(n_groups, n_chunks),
        in_specs=[
            pl.BlockSpec((SUBLANES, tc, SSM_CH), lambda g, i: (g, i, 0)),
            yspec,
            yspec,
            _lspec((1, SSM_CH), layer),
            _lspec((SSM_CH, 2 * SSM_CH), layer),
        ],
        out_specs=pl.BlockSpec((SUBLANES, tc, SSM_CH), lambda g, i: (g, i, 0)),
        scratch_shapes=[pltpu.VMEM((SUBLANES * tc, SSM_CH), F32)],
        compiler_params=_cparams(("parallel", "parallel")),
        name="ssm_glu",
    )(u3, y_f, y_b, d_skip, w_glu)


def _ones_column(n):
    return jnp.ones((n, LANES), BF16)


def _softmax_pv(scores, v_ext):
    m = jnp.max(scores, axis=-1, keepdims=True)
    p = jnp.exp((scores - m).astype(BF16))
    o = _dot(p, v_ext)
    return o[:, :LANES] / o[:, LANES:]


ATTN_TQ = 256


def _for_each_q_block(nb, seq, body):
    nq = seq // ATTN_TQ
    for s in range(nb):
        for qi in range(nq):
            body(s, s * seq + qi * ATTN_TQ)


def _gqa_kernel(latent, nb, seq, *refs):
    if latent:
        q_ref, k_ref, v_ref, kc_ref, vc_ref, o_ref, k_s, v_s = refs
    else:
        q_ref, k_ref, v_ref, o_ref, k_s, v_s = refs
    tq = ATTN_TQ
    for s in range(nb):
        for h in range(GQA_KV_HEADS):
            hl = slice(h * LANES, (h + 1) * LANES)
            k_s[s, h, 0:seq, :] = k_ref[s * seq:(s + 1) * seq, hl].astype(BF16)
            v_s[s, h, 0:seq, :LANES] = v_ref[s * seq:(s + 1) * seq, hl].astype(BF16)
            if latent:
                k_s[s, h, seq:, :] = kc_ref[:, hl].astype(BF16)
                v_s[s, h, seq:, :LANES] = vc_ref[:, hl].astype(BF16)
            v_s[s, h, :, LANES:] = _ones_column(v_s.shape[2])

    def body(s, r0):
        for h in range(GQA_KV_HEADS):
            heads = [h * GQA_GROUP + g for g in range(GQA_GROUP)]
            q3 = jnp.concatenate([q_ref[pl.ds(r0, tq), hd * LANES:(hd + 1) * LANES] for hd in heads], axis=0)
            o = _softmax_pv(_dot_nt(q3, k_s[s, h]), v_s[s, h])
            for g, hd in enumerate(heads):
                o_ref[pl.ds(r0, tq), hd * LANES:(hd + 1) * LANES] = o[g * tq:(g + 1) * tq].astype(o_ref.dtype)

    _for_each_q_block(nb, seq, body)


def _gqa_attn(qb, kb, vb, latent, cache_k=None, cache_v=None, layer=0):
    rows = qb.shape[0]
    seq = DEC_SEQ if latent else SEQ
    nb = 1 if latent else 4
    t_all = seq + (PAST_LEN if latent else 0)
    qw = GQA_HEADS * GQA_HEAD_DIM
    kw = GQA_KV_HEADS * GQA_HEAD_DIM
    row = lambda i: (i, 0)
    in_specs = [
        pl.BlockSpec((nb * seq, qw), row),
        pl.BlockSpec((nb * seq, kw), row),
        pl.BlockSpec((nb * seq, kw), row),
    ]
    args = [qb, kb, vb]
    if latent:
        cspec = pl.BlockSpec((None, None, PAST_LEN, kw), lambda i: (i, layer, 0, 0))
        in_specs += [cspec, cspec]
        args += [cache_k, cache_v]
    return pl.pallas_call(
        functools.partial(_gqa_kernel, latent, nb, seq),
        out_shape=jax.ShapeDtypeStruct((rows, qw), BF16),
        grid=(rows // (nb * seq),),
        in_specs=in_specs,
        out_specs=pl.BlockSpec((nb * seq, qw), row),
        scratch_shapes=[
            pltpu.VMEM((nb, GQA_KV_HEADS, t_all, GQA_HEAD_DIM), BF16),
            pltpu.VMEM((nb, GQA_KV_HEADS, t_all, 2 * LANES), BF16),
        ],
        compiler_params=_cparams(("parallel",)),
        name="gqa_lat" if latent else "gqa_ctx",
    )(*args)


def _mla_kernel(latent, nb, seq, *refs):
    if latent:
        (q_ref, ckv_ref, kr_ref, krg_ref, ckvc_ref, krc_ref, wuk_ref, wuv_ref, gn_ref, gr_ref,
         o_ref, k_s, v_s) = refs
    else:
        (q_ref, ckv_ref, kr_ref, krg_ref, wuk_ref, wuv_ref, gn_ref, gr_ref, o_ref, k_s, v_s) = refs
    tq = ATTN_TQ

    def expand(ckv, kr_raw, krg, place):
        c = ckv.astype(BF16)
        kn_all = _dot(c, wuk_ref[...])
        v_all = _dot(c, wuv_ref[...])
        ss_kr = jnp.sum(kr_raw * kr_raw, axis=-1, keepdims=True)
        for h in range(MLA_HEADS):
            hl = slice(h * LANES, (h + 1) * LANES)
            kn = kn_all[:, hl]
            rs = lax.rsqrt((jnp.sum(kn * kn, axis=-1, keepdims=True) + ss_kr) * (1.0 / MLA_QK) + EPS)
            kn = (kn * rs * gn_ref[...]).astype(BF16)
            kp = (krg * rs).astype(BF16)
            vv = v_all[:, hl].astype(BF16)
            for s, lo, n, rows in place:
                k_s[s, h, lo:lo + n, :MLA_NOPE] = kn[rows]
                k_s[s, h, lo:lo + n, MLA_NOPE:MLA_QK] = kp[rows]
                k_s[s, h, lo:lo + n, MLA_QK:] = jnp.zeros((n, MLA_QK_PAD - MLA_QK), BF16)
                v_s[s, h, lo:lo + n, :MLA_V] = vv[rows]
                v_s[s, h, lo:lo + n, MLA_V:] = _ones_column(n)

    expand(ckv_ref[...], kr_ref[...], krg_ref[...],
           [(s, 0, seq, slice(s * seq, (s + 1) * seq)) for s in range(nb)])
    if latent:
        krc = krc_ref[...]
        expand(ckvc_ref[...], krc, krc * gr_ref[...], [(0, seq, PAST_LEN, slice(0, PAST_LEN))])

    def body(s, r0):
        for h in range(MLA_HEADS):
            sc = _dot_nt(q_ref[h, pl.ds(r0, tq), :], k_s[s, h])
            o_ref[pl.ds(r0, tq), h * LANES:(h + 1) * LANES] = _softmax_pv(sc, v_s[s, h]).astype(o_ref.dtype)

    _for_each_q_block(nb, seq, body)


def _mla_attn(qc, ckv, kr, krg, lw, latent, cache_ckv=None, cache_kr=None, layer=0):
    rows = qc.shape[1]
    seq = DEC_SEQ if latent else SEQ
    nb = 1 if latent else 4
    t_all = seq + (PAST_LEN if latent else 0)
    row = lambda i: (i, 0)
    in_specs = [
        pl.BlockSpec((MLA_HEADS, nb * seq, MLA_QK_PAD), lambda i: (0, i, 0)),
        pl.BlockSpec((nb * seq, MLA_KV_RANK), row),
        pl.BlockSpec((nb * seq, MLA_ROPE), row),
        pl.BlockSpec((nb * seq, MLA_ROPE), row),
    ]
    args = [qc, ckv, kr, krg]
    if latent:
        in_specs += [
            pl.BlockSpec((None, None, PAST_LEN, MLA_KV_RANK), lambda i: (i, layer, 0, 0)),
            pl.BlockSpec((None, None, PAST_LEN, MLA_ROPE), lambda i: (i, layer, 0, 0)),
        ]
        args += [cache_ckv, cache_kr]
    in_specs += [
        _lspec((MLA_KV_RANK, MLA_HEADS * MLA_NOPE), layer),
        _lspec((MLA_KV_RANK, MLA_HEADS * MLA_V), layer),
        _lspec((1, MLA_NOPE), layer),
        _lspec((1, MLA_ROPE), layer),
    ]
    args += [lw['mla_w_uk'], lw['mla_w_uv'], lw['mla_k_nope_g'], lw['mla_k_rope_g64']]
    return pl.pallas_call(
        functools.partial(_mla_kernel, latent, nb, seq),
        out_shape=jax.ShapeDtypeStruct((rows, MLA_HEADS * MLA_V), BF16),
        grid=(rows // (nb * seq),),
        in_specs=in_specs,
        out_specs=pl.BlockSpec((nb * seq, MLA_HEADS * MLA_V), row),
        scratch_shapes=[
            pltpu.VMEM((nb, MLA_HEADS, t_all, MLA_QK_PAD), BF16),
            pltpu.VMEM((nb, MLA_HEADS, t_all, MLA_V + LANES), BF16),
        ],
        compiler_params=_cparams(("parallel",)),
        name="mla_lat" if latent else "mla_ctx",
    )(*args)


def _out_proj_kernel(ya_ref, yb_ref, yc_ref, x_ref, mod_ref, w_ref, g_ref, x1_ref, h2_ref):
    wa = SSM_CH
    wb = wa + GQA_HEADS * GQA_HEAD_DIM
    piece = x_ref.shape[0] // ROW_PIECES
    pieces = [slice(r * piece, (r + 1) * piece) for r in range(ROW_PIECES)]
    outs = [(_dot(ya_ref[rs, :], w_ref[0:wa, :]) + _dot(yb_ref[rs, :], w_ref[wa:wb, :])
             + _dot(yc_ref[rs, :], w_ref[wb:, :])) for rs in pieces]
    for rs, o in zip(pieces, outs):
        x1 = x_ref[rs, :] + mod_ref[2:3, :] * o
        x1_ref[rs, :] = x1
        ms = jnp.mean(x1 * x1, axis=-1, keepdims=True)
        y = x1 * lax.rsqrt(ms + EPS) * g_ref[...]
        h2_ref[rs, :] = (y * (1.0 + mod_ref[4:5, :]) + mod_ref[3:4, :]).astype(h2_ref.dtype)


def _out_proj(ya, yb, yc, x2d, mod, lw, latent, tm, layer):
    rows = x2d.shape[0]
    row = lambda i: (i, 0)
    return pl.pallas_call(
        _out_proj_kernel,
        out_shape=(jax.ShapeDtypeStruct((rows, D_MODEL), F32), jax.ShapeDtypeStruct((rows, D_MODEL), BF16)),
        grid=(rows // tm,),
        in_specs=[
            pl.BlockSpec((tm, SSM_CH), row),
            pl.BlockSpec((tm, GQA_HEADS * GQA_HEAD_DIM), row),
            pl.BlockSpec((tm, MLA_HEADS * MLA_V), row),
            pl.BlockSpec((tm, D_MODEL), row),
            _mod_spec(latent, tm, layer),
            _lspec((D_MODEL, D_MODEL), layer),
            _lspec((1, D_MODEL), layer),
        ],
        out_specs=(pl.BlockSpec((tm, D_MODEL), row), pl.BlockSpec((tm, D_MODEL), row)),
        compiler_params=_cparams(("parallel",)),
        name="out_proj",
    )(ya, yb, yc, x2d, mod, lw['w_out'], lw['norm_mlp'])


def _mlp_kernel(h2_ref, x1_ref, mod_ref, w1_ref, w2_ref, o_ref):
    j = pl.program_id(1)

    @pl.when(j == 0)
    def _():
        o_ref[...] = jnp.zeros_like(o_ref)

    f = jnp.maximum(_dot(h2_ref[...], w1_ref[...]), 0.0)
    o_ref[...] += _dot((f * f).astype(BF16), w2_ref[...])

    @pl.when(j == pl.num_programs(1) - 1)
    def _():
        o_ref[...] = x1_ref[...] + mod_ref[5:6, :] * o_ref[...]


def _mlp(h2, x1, mod, lw, latent, tm, tf, layer):
    rows = h2.shape[0]
    return pl.pallas_call(
        _mlp_kernel,
        out_shape=jax.ShapeDtypeStruct((rows, D_MODEL), F32),
        grid=(rows // tm, D_FF // tf),
        in_specs=[
            pl.BlockSpec((tm, D_MODEL), lambda i, j: (i, 0)),
            pl.BlockSpec((tm, D_MODEL), lambda i, j: (i, 0), pipeline_mode=pl.Buffered(1)),
            _mod_spec(latent, tm, layer),
            pl.BlockSpec((None, D_MODEL, tf), lambda i, j: (layer, 0, j)),
            pl.BlockSpec((None, tf, D_MODEL), lambda i, j: (layer, j, 0)),
        ],
        out_specs=pl.BlockSpec((tm, D_MODEL), lambda i, j: (i, 0)),
        compiler_params=_cparams(("parallel", "arbitrary")),
        name="mlp",
    )(h2, x1, mod, lw['w_ff1'], lw['w_ff2'])


def _permute_w_in(w):
    base = OFF_QCN
    ckv0 = base + MLA_HEADS * MLA_QK
    qc = w[:, :, base:ckv0].reshape(DEPTH, D_MODEL, MLA_HEADS, MLA_QK)
    nope = qc[..., :MLA_NOPE].reshape(DEPTH, D_MODEL, MLA_HEADS * MLA_NOPE)
    rope = qc[..., MLA_NOPE:].reshape(DEPTH, D_MODEL, MLA_HEADS * MLA_ROPE)
    kr = w[:, :, ckv0 + MLA_KV_RANK:]
    parts = [w[:, :, :base], nope, rope, w[:, :, ckv0:ckv0 + MLA_KV_RANK], kr, kr]
    return jnp.concatenate(parts, axis=2).astype(BF16)


def _rope_tables(seq):
    t = jnp.arange(seq)
    row = (t // GRID_W).astype(F32)
    col = (t % GRID_W).astype(F32)

    def table(d):
        quarter = d // 4
        inv = ROPE_BASE ** (-(jnp.arange(quarter, dtype=F32) / quarter))
        ar = row[:, None] * inv[None, :]
        ac = col[:, None] * inv[None, :]
        cos = jnp.concatenate([jnp.cos(ar), jnp.cos(ar), jnp.cos(ac), jnp.cos(ac)], axis=-1)
        sin = jnp.concatenate([-jnp.sin(ar), jnp.sin(ar), -jnp.sin(ac), jnp.sin(ac)], axis=-1)
        reps = LANES // d
        return jnp.tile(cos, (1, reps)), jnp.tile(sin, (1, reps))

    cos_b, sin_b = table(GQA_HEAD_DIM)
    cos_c, sin_c = table(MLA_ROPE)
    return {'cos_b': cos_b, 'sin_b': sin_b, 'cos_c': cos_c, 'sin_c': sin_c}


def _ssm_params(lam_re, lam_im, log_dt, b_re, b_im, c_re, c_im):
    a = lam_re.astype(F32)
    w = lam_im.astype(F32)
    dt = jnp.exp(log_dt.astype(F32))[..., None]
    mag = jnp.exp(a * dt)
    lbr = mag * jnp.cos(w * dt)
    lbi = mag * jnp.sin(w * dt)
    den = a * a + w * w
    cr = (((lbr - 1.0) * a + lbi * w) / den)[..., None]
    ci = ((lbi * a - (lbr - 1.0) * w) / den)[..., None]
    bre = b_re.astype(F32)
    bim = b_im.astype(F32)
    bb_re = cr * bre - ci * bim
    bb_im = cr * bim + ci * bre
    lam_ri = jnp.stack([lbr.reshape(DEPTH, 2, SSM_N), lbi.reshape(DEPTH, 2, SSM_N)], axis=2)
    lam_b = jnp.broadcast_to(lam_ri[:, :, :, None, :], (DEPTH, 2, 2, SUBLANES, SSM_N))
    gh = SSM_GROUPS // 2
    eye = jnp.eye(gh, dtype=F32)[:, None, :, None]

    def blockdiag(x, rows_per_g, cols_per_g):
        x = x.reshape(DEPTH, 2, 2, gh, rows_per_g, 1, cols_per_g) * eye
        return x.reshape(DEPTH, 2, 2, gh * rows_per_g, gh * cols_per_g)

    def bmat(x):
        return blockdiag(jnp.swapaxes(x, -1, -2), SSM_GROUP, SSM_STATE)

    def cmat(x):
        return blockdiag(jnp.swapaxes(x, -1, -2), SSM_STATE, SSM_GROUP)

    b_cat = jnp.concatenate([bmat(bb_re), bmat(bb_im)], axis=-1).astype(BF16)
    return {'lam': lam_b, 'b': b_cat, 'c_re': cmat(c_re.astype(F32)).astype(BF16),
            'c_im': cmat(c_im.astype(F32)).astype(BF16)}


def _trunk_layer(x2d, mod, lw, sp, tabs, latent, ctx, layer):
    seq = DEC_SEQ if latent else SEQ
    bsz = x2d.shape[0] // seq
    u, qb, kb, vb, qc, ckv, kr, krg = _in_proj(x2d, mod, latent, lw, tabs, TM_PROJ, layer)

    u3 = u.reshape(bsz, seq, SSM_CH)
    y_f, y_b, h_t = _ssm_scan(u3, ctx['h0'], sp, layer)
    ya = _ssm_glu(u3, y_f, y_b, lw['ssm_d'], lw['ssm_w_glu'], layer).reshape(bsz * seq, SSM_CH)

    if latent:
        yb = _gqa_attn(qb, kb, vb, True, ctx['k'], ctx['v'], layer)
        yc = _mla_attn(qc, ckv, kr, krg, lw, True, ctx['ckv'], ctx['kr'], layer)
    else:
        yb = _gqa_attn(qb, kb, vb, False)
        yc = _mla_attn(qc, ckv, kr, krg, lw, False, layer=layer)

    x1, h2 = _out_proj(ya, yb, yc, x2d, mod, lw, latent, TM_PROJ, layer)
    x2 = _mlp(h2, x1, mod, lw, latent, TM_MLP, TF_MLP, layer)
    return x2, (kb, vb, ckv, kr, h_t)


def kernel(x_prompt, x_sample, cache_attn_k, cache_attn_v, cache_mla_ckv, cache_mla_krope, state_ssm, c, c_ctx, w_mod, b_mod, norm_mix, norm_mlp, w_in, gqa_q_norm, gqa_k_norm, mla_kv_norm, mla_q_norm, mla_k_norm, mla_w_uk, mla_w_uv, ssm_lam_re, ssm_lam_im, ssm_log_dt, ssm_b_re, ssm_b_im, ssm_c_re, ssm_c_im, ssm_d, ssm_w_glu, w_out, w_ff1, w_ff2):
    cvec = jnp.zeros((N_MOD, D_MODEL), F32).at[0].set(c_ctx).at[1:1 + DEC_BATCH].set(c)
    mod = _adaln(cvec, w_mod, b_mod.reshape(DEPTH, 1, 6 * D_MODEL)).reshape(DEPTH, N_MOD, 6, D_MODEL)

    tabs = _rope_tables(DEC_SEQ)
    cache_k = cache_attn_k.reshape(DEC_BATCH, DEPTH, PAST_LEN, GQA_KV_HEADS * GQA_HEAD_DIM)
    cache_v = cache_attn_v.reshape(DEC_BATCH, DEPTH, PAST_LEN, GQA_KV_HEADS * GQA_HEAD_DIM)

    xp = x_prompt.reshape(BATCH * SEQ, D_MODEL)
    xs = x_sample.reshape(DEC_BATCH * DEC_SEQ, D_MODEL)
    new_k, new_v, new_ckv, new_kr, new_ssm = [], [], [], [], []
    h0_zero = jnp.zeros((BATCH // SUBLANES, 2, 2, SUBLANES, SSM_N), F32)

    row3 = lambda a: a.reshape(DEPTH, 1, -1)
    dup = lambda a: jnp.concatenate([a, a], axis=-1)
    lw = {
        'norm_mix': row3(norm_mix), 'norm_mlp': row3(norm_mlp),
        'w_in': _permute_w_in(w_in),
        'w_out': w_out.astype(BF16), 'w_ff1': w_ff1.astype(BF16), 'w_ff2': w_ff2.astype(BF16),
        'gqa_q_norm': row3(gqa_q_norm), 'gqa_k_norm': row3(gqa_k_norm), 'mla_kv_norm': row3(mla_kv_norm),
        'mla_q_nope_g': row3(mla_q_norm[:, :MLA_NOPE]),
        'mla_q_rope_g': row3(dup(mla_q_norm[:, MLA_NOPE:])),
        'mla_k_nope_g': row3(mla_k_norm[:, :MLA_NOPE]),
        'mla_k_rope_g': row3(dup(mla_k_norm[:, MLA_NOPE:])),
        'mla_k_rope_g64': row3(mla_k_norm[:, MLA_NOPE:]),
        'mla_w_uk': mla_w_uk.reshape(DEPTH, MLA_KV_RANK, MLA_HEADS * MLA_NOPE).astype(BF16),
        'mla_w_uv': mla_w_uv.reshape(DEPTH, MLA_KV_RANK, MLA_HEADS * MLA_V).astype(BF16),
        'ssm_d': row3(ssm_d), 'ssm_w_glu': ssm_w_glu.astype(BF16),
    }
    sp = _ssm_params(ssm_lam_re, ssm_lam_im, ssm_log_dt, ssm_b_re, ssm_b_im, ssm_c_re, ssm_c_im)
    h0_lat = state_ssm.reshape(DEC_BATCH, DEPTH, 2, SSM_N, 2).transpose(1, 2, 4, 0, 3)[:, None]

    for l in range(DEPTH):
        xp, (k, v, ckv_n, kr, h_t) = _trunk_layer(xp, mod, lw, sp, tabs, False, {'h0': h0_zero}, l)
        new_k.append(k.reshape(BATCH, SEQ, GQA_KV_HEADS, GQA_HEAD_DIM))
        new_v.append(v.reshape(BATCH, SEQ, GQA_KV_HEADS, GQA_HEAD_DIM))
        new_ckv.append(ckv_n.reshape(BATCH, SEQ, MLA_KV_RANK))
        new_kr.append(kr.reshape(BATCH, SEQ, MLA_ROPE))
        hs = h_t.transpose(0, 3, 1, 4, 2).reshape(BATCH, 2, SSM_GROUPS, SSM_STATE, 2)
        new_ssm.append(hs)

        ctx = {'k': cache_k, 'v': cache_v, 'ckv': cache_mla_ckv, 'kr': cache_mla_krope, 'h0': h0_lat[l]}
        xs, _ = _trunk_layer(xs, mod, lw, sp, tabs, True, ctx, l)

    return (xp.reshape(BATCH, SEQ, D_MODEL), xs.reshape(DEC_BATCH, DEC_SEQ, D_MODEL),
            jnp.stack(new_k, axis=1), jnp.stack(new_v, axis=1), jnp.stack(new_ckv, axis=1),
            jnp.stack(new_kr, axis=1), jnp.stack(new_ssm, axis=1))
```

```python
import functools
import math

import jax
import jax.numpy as jnp
from jax import lax
from jax.experimental import pallas as pl
from jax.experimental.pallas import tpu as pltpu

D_MODEL = 2048
BATCH = 32
SEQ = 256
DEPTH = 2
DEC_BATCH = 8
DEC_SEQ = 1024
PAST_LEN = 512
GRID_W = 64
ROPE_BASE = 10000.0
EPS = 1e-6
SSM_CH = 512
SSM_GROUP = 16
SSM_GROUPS = SSM_CH // SSM_GROUP
SSM_STATE = 64
SSM_N = SSM_GROUPS * SSM_STATE
GQA_HEADS = 6
GQA_KV_HEADS = 2
GQA_GROUP = GQA_HEADS // GQA_KV_HEADS
GQA_HEAD_DIM = 128
MLA_HEADS = 6
MLA_NOPE = 128
MLA_ROPE = 64
MLA_QK = MLA_NOPE + MLA_ROPE
MLA_QK_PAD = 256
MLA_V = 128
MLA_KV_RANK = 512
D_FF = 4 * D_MODEL
N_MOD = 16

OFF_U = 0
OFF_QB = OFF_U + SSM_CH
OFF_KB = OFF_QB + GQA_HEADS * GQA_HEAD_DIM
OFF_VB = OFF_KB + GQA_KV_HEADS * GQA_HEAD_DIM
OFF_QCN = OFF_VB + GQA_KV_HEADS * GQA_HEAD_DIM
OFF_QCR = OFF_QCN + MLA_HEADS * MLA_NOPE
OFF_CKV = OFF_QCR + MLA_HEADS * MLA_ROPE
OFF_KR = OFF_CKV + MLA_KV_RANK
IN_WIDTH_P = OFF_KR + 2 * MLA_ROPE

LANES = 128
SUBLANES = 8
VMEM_LIMIT = 56 * 1024 * 1024

TM_PROJ = 512
TM_MLP = 1024
TF_MLP = 512
ROW_PIECES = 2
IN_PROJ_PIECES = 2

BF16 = jnp.bfloat16
F32 = jnp.float32


def _cparams(sem):
    return pltpu.CompilerParams(dimension_semantics=sem, vmem_limit_bytes=VMEM_LIMIT)


def _dot(a, b):
    return jnp.dot(a, b, preferred_element_type=F32)


def _dot_nt(a, b):
    return lax.dot_general(a, b, (((1,), (1,)), ((), ())), preferred_element_type=F32)


def _adaln_kernel(c_ref, w_ref, b_ref, o_ref):
    c = c_ref[...]
    s = (c * jax.nn.sigmoid(c)).astype(BF16)
    o_ref[...] = _dot(s, w_ref[...].astype(BF16)) + b_ref[...]


def _adaln(cvec, w_mod, b_mod):
    tn = 1024
    return pl.pallas_call(
        _adaln_kernel,
        out_shape=jax.ShapeDtypeStruct((DEPTH, N_MOD, 6 * D_MODEL), F32),
        grid=(DEPTH, 6 * D_MODEL // tn),
        in_specs=[
            pl.BlockSpec((N_MOD, D_MODEL), lambda l, j: (0, 0)),
            pl.BlockSpec((None, D_MODEL, tn), lambda l, j: (l, 0, j)),
            pl.BlockSpec((None, 1, tn), lambda l, j: (l, 0, j)),
        ],
        out_specs=pl.BlockSpec((None, N_MOD, tn), lambda l, j: (l, 0, j)),
        compiler_params=_cparams(("parallel", "parallel")),
        name="adaln",
    )(cvec, w_mod, b_mod)


def _swap_halves(x, block):
    lane = lax.broadcasted_iota(jnp.int32, x.shape, 1)
    first = (lane % (2 * block)) < block
    return jnp.where(first, pltpu.roll(x, LANES - block, 1), pltpu.roll(x, block, 1))


def _rope(x, cos, sin_signed, block):
    return x * cos + _swap_halves(x, block) * sin_signed


def _in_proj_kernel(latent, x_ref, mod_ref, gmix_ref, w_ref, gq_ref, gk_ref, gkv_ref, gqn_ref, gqr_ref, gkr_ref,
                    cosb_ref, sinb_ref, cosc_ref, sinc_ref,
                    u_ref, qb_ref, kb_ref, vb_ref, qc_ref, ckv_ref, kr_ref, krg_ref):
    piece = x_ref.shape[0] // IN_PROJ_PIECES
    for r in range(IN_PROJ_PIECES):
        rs = pl.ds(r * piece, piece)
        _in_proj_rows(latent, x_ref.at[rs], mod_ref, gmix_ref, w_ref, gq_ref, gk_ref, gkv_ref, gqn_ref, gqr_ref,
                      gkr_ref, cosb_ref.at[rs], sinb_ref.at[rs], cosc_ref.at[rs], sinc_ref.at[rs],
                      u_ref.at[rs], qb_ref.at[rs], kb_ref.at[rs], vb_ref.at[rs], qc_ref.at[:, rs], ckv_ref.at[rs],
                      kr_ref.at[rs], krg_ref.at[rs])


def _in_proj_rows(latent, x_ref, mod_ref, gmix_ref, w_ref, gq_ref, gk_ref, gkv_ref, gqn_ref, gqr_ref, gkr_ref,
                  cosb_ref, sinb_ref, cosc_ref, sinc_ref,
                  u_ref, qb_ref, kb_ref, vb_ref, qc_ref, ckv_ref, kr_ref, krg_ref):
    x = x_ref[...]
    ms = jnp.mean(x * x, axis=-1, keepdims=True)
    y = x * lax.rsqrt(ms + EPS) * gmix_ref[...]
    h = (y * (1.0 + mod_ref[1:2, :]) + mod_ref[0:1, :]).astype(BF16)

    def proj(off, width):
        return _dot(h, w_ref[:, off:off + width])

    scale_b = 1.0 / math.sqrt(GQA_HEAD_DIM)
    zq = proj(OFF_QB, GQA_HEADS * GQA_HEAD_DIM)
    for hd in range(GQA_HEADS):
        col = zq[:, hd * LANES:(hd + 1) * LANES]
        q = col * lax.rsqrt(jnp.mean(col * col, axis=-1, keepdims=True) + EPS) * gq_ref[...]
        if latent:
            q = _rope(q, cosb_ref[...], sinb_ref[...], GQA_HEAD_DIM // 4)
        qb_ref[:, hd * LANES:(hd + 1) * LANES] = (q * scale_b).astype(qb_ref.dtype)
    zk = proj(OFF_KB, GQA_KV_HEADS * GQA_HEAD_DIM)
    for hd in range(GQA_KV_HEADS):
        col = zk[:, hd * LANES:(hd + 1) * LANES]
        k = col * lax.rsqrt(jnp.mean(col * col, axis=-1, keepdims=True) + EPS) * gk_ref[...]
        if latent:
            k = _rope(k, cosb_ref[...], sinb_ref[...], GQA_HEAD_DIM // 4)
        kb_ref[:, hd * LANES:(hd + 1) * LANES] = k.astype(kb_ref.dtype)

    scale_c = 1.0 / math.sqrt(MLA_QK)
    zn = proj(OFF_QCN, MLA_HEADS * MLA_NOPE)
    zr = proj(OFF_QCR, MLA_HEADS * MLA_ROPE)
    lane = lax.broadcasted_iota(jnp.int32, (x.shape[0], LANES), 1)
    low = lane < MLA_ROPE
    for pair in range(MLA_HEADS // 2):
        colr = zr[:, pair * LANES:(pair + 1) * LANES]
        sq = colr * colr
        ss_lo = jnp.sum(jnp.where(low, sq, 0.0), axis=-1, keepdims=True)
        ss_hi = jnp.sum(jnp.where(low, 0.0, sq), axis=-1, keepdims=True)
        rs = []
        for half, ss_r in ((0, ss_lo), (1, ss_hi)):
            hd = 2 * pair + half
            coln = zn[:, hd * LANES:(hd + 1) * LANES]
            ss = jnp.sum(coln * coln, axis=-1, keepdims=True) + ss_r
            r = lax.rsqrt(ss * (1.0 / MLA_QK) + EPS)
            rs.append(r)
            qc_ref[hd, :, :MLA_NOPE] = (coln * r * gqn_ref[...] * scale_c).astype(qc_ref.dtype)
        qr = colr * jnp.where(low, rs[0], rs[1]) * gqr_ref[...]
        if latent:
            qr = _rope(qr, cosc_ref[...], sinc_ref[...], MLA_ROPE // 4)
        qr = qr * scale_c
        zeros = jnp.zeros((x.shape[0], MLA_QK_PAD - MLA_QK), qc_ref.dtype)
        for half in range(2):
            hd = 2 * pair + half
            qc_ref[hd, :, MLA_NOPE:MLA_QK] = qr[:, half * MLA_ROPE:(half + 1) * MLA_ROPE].astype(qc_ref.dtype)
            qc_ref[hd, :, MLA_QK:] = zeros

    zc = proj(OFF_CKV, MLA_KV_RANK)
    ckv_ref[...] = (zc * lax.rsqrt(jnp.mean(zc * zc, axis=-1, keepdims=True) + EPS) * gkv_ref[...]).astype(ckv_ref.dtype)
    zkr = proj(OFF_KR, 2 * MLA_ROPE)
    kr_ref[...] = zkr[:, :MLA_ROPE]
    krg = zkr * gkr_ref[...]
    if latent:
        krg = _rope(krg, cosc_ref[...], sinc_ref[...], MLA_ROPE // 4)
    krg_ref[...] = krg[:, :MLA_ROPE]

    vb_ref[...] = proj(OFF_VB, GQA_KV_HEADS * GQA_HEAD_DIM).astype(vb_ref.dtype)
    u_ref[...] = proj(OFF_U, SSM_CH)


def _lspec(block_tail, layer):
    zeros = (0,) * len(block_tail)
    return pl.BlockSpec((None,) + tuple(block_tail), lambda *_: (layer,) + zeros)


def _mod_spec(latent, tm, layer):
    tiles_per_seq = DEC_SEQ // tm
    if latent:
        return pl.BlockSpec((None, None, 6, D_MODEL), lambda i, *_: (layer, 1 + i // tiles_per_seq, 0, 0))
    return pl.BlockSpec((None, None, 6, D_MODEL), lambda i, *_: (layer, 0, 0, 0))


def _in_proj(x2d, mod, latent, lw, tabs, tm, layer):
    rows = x2d.shape[0]
    n_tiles = rows // tm
    tiles_per_seq = DEC_SEQ // tm
    if latent:
        tab_map = lambda i: (i % tiles_per_seq, 0)
    else:
        tab_map = lambda i: (0, 0)
    row = lambda i: (i, 0)
    act_dt = BF16 if latent else F32
    in_specs = [
        pl.BlockSpec((tm, D_MODEL), row),
        _mod_spec(latent, tm, layer),
        _lspec((1, D_MODEL), layer),
        _lspec((D_MODEL, IN_WIDTH_P), layer),
        _lspec((1, LANES), layer),
        _lspec((1, LANES), layer),
        _lspec((1, MLA_KV_RANK), layer),
        _lspec((1, LANES), layer),
        _lspec((1, LANES), layer),
        _lspec((1, LANES), layer),
        pl.BlockSpec((tm, LANES), tab_map),
        pl.BlockSpec((tm, LANES), tab_map),
        pl.BlockSpec((tm, LANES), tab_map),
        pl.BlockSpec((tm, LANES), tab_map),
    ]
    out_shape = (
        jax.ShapeDtypeStruct((rows, SSM_CH), F32),
        jax.ShapeDtypeStruct((rows, GQA_HEADS * GQA_HEAD_DIM), BF16),
        jax.ShapeDtypeStruct((rows, GQA_KV_HEADS * GQA_HEAD_DIM), act_dt),
        jax.ShapeDtypeStruct((rows, GQA_KV_HEADS * GQA_HEAD_DIM), act_dt),
        jax.ShapeDtypeStruct((MLA_HEADS, rows, MLA_QK_PAD), BF16),
        jax.ShapeDtypeStruct((rows, MLA_KV_RANK), act_dt),
        jax.ShapeDtypeStruct((rows, MLA_ROPE), F32),
        jax.ShapeDtypeStruct((rows, MLA_ROPE), F32),
    )
    out_specs = (
        pl.BlockSpec((tm, SSM_CH), row),
        pl.BlockSpec((tm, GQA_HEADS * GQA_HEAD_DIM), row),
        pl.BlockSpec((tm, GQA_KV_HEADS * GQA_HEAD_DIM), row),
        pl.BlockSpec((tm, GQA_KV_HEADS * GQA_HEAD_DIM), row),
        pl.BlockSpec((MLA_HEADS, tm, MLA_QK_PAD), lambda i: (0, i, 0)),
        pl.BlockSpec((tm, MLA_KV_RANK), row),
        pl.BlockSpec((tm, MLA_ROPE), row),
        pl.BlockSpec((tm, MLA_ROPE), row),
    )
    return pl.pallas_call(
        functools.partial(_in_proj_kernel, latent),
        out_shape=out_shape,
        grid=(n_tiles,),
        in_specs=in_specs,
        out_specs=out_specs,
        compiler_params=_cparams(("parallel",)),
        name="in_proj_lat" if latent else "in_proj_ctx",
    )(x2d, mod, lw['norm_mix'], lw['w_in'], lw['gqa_q_norm'], lw['gqa_k_norm'], lw['mla_kv_norm'],
      lw['mla_q_nope_g'], lw['mla_q_rope_g'], lw['mla_k_rope_g'],
      tabs['cos_b'], tabs['sin_b'], tabs['cos_c'], tabs['sin_c'])


SSM_TC = 128
SSM_HALVES = 2


def _ssm_scan_kernel(n_chunks, backward, u_ref, h0_ref, lam_ref, b_ref, cre_ref, cim_ref, y_ref, hT_ref,
                     utm_ref, sre0_ref, sim0_ref, sre1_ref, sim1_ref, h0s_ref, h1s_ref):
    i = pl.program_id(1)
    tc = SSM_TC
    half_n = SSM_N // SSM_HALVES
    halves = ((sre0_ref, sim0_ref, h0s_ref), (sre1_ref, sim1_ref, h1s_ref))

    @pl.when(i == 0)
    def _():
        for k, (_, _, hk_ref) in enumerate(halves):
            hk_ref[...] = h0_ref[:, :, k * half_n:(k + 1) * half_n]

    for b in range(SUBLANES):
        for j in range(SSM_CH // LANES):
            utm_ref[j, pl.ds(b, tc, stride=SUBLANES), :] = u_ref[b, :, j * LANES:(j + 1) * LANES]

    for k, (sre_ref, sim_ref, _) in enumerate(halves):
        uk = jnp.concatenate([utm_ref[2 * k], utm_ref[2 * k + 1]], axis=-1).astype(BF16)
        r = _dot(uk, b_ref[k])
        sre_ref[...] = r[:, :half_n]
        sim_ref[...] = r[:, half_n:]

    for k, (sre_ref, sim_ref, hk_ref) in enumerate(halves):
        sl = slice(k * half_n, (k + 1) * half_n)
        lr = lam_ref[0, :, sl]
        li = lam_ref[1, :, sl]
        hr = hk_ref[0]
        hi = hk_ref[1]
        for s in range(tc):
            t = (tc - 1 - s) if backward else s
            rows = slice(t * SUBLANES, (t + 1) * SUBLANES)
            hr, hi = (lr * hr - li * hi + sre_ref[rows, :], lr * hi + li * hr + sim_ref[rows, :])
            sre_ref[rows, :] = hr
            sim_ref[rows, :] = hi
        hk_ref[0] = hr
        hk_ref[1] = hi

    for k, (sre_ref, sim_ref, _) in enumerate(halves):
        yk = _dot(sre_ref[...].astype(BF16), cre_ref[k]) - _dot(sim_ref[...].astype(BF16), cim_ref[k])
        y_ref[2 * k] = yk[:, :LANES]
        y_ref[2 * k + 1] = yk[:, LANES:]

    @pl.when(i == n_chunks - 1)
    def _():
        for k, (_, _, hk_ref) in enumerate(halves):
            hT_ref[:, :, k * half_n:(k + 1) * half_n] = hk_ref[...]


def _ssm_scan(u3, h0, sp, layer):
    bsz, t, _ = u3.shape
    n_groups = bsz // SUBLANES
    n_chunks = t // SSM_TC
    tc = SSM_TC

    def one_direction(d):
        chunk = (lambda i: n_chunks - 1 - i) if d else (lambda i: i)
        return pl.pallas_call(
            functools.partial(_ssm_scan_kernel, n_chunks, d),
            out_shape=(
                jax.ShapeDtypeStruct((n_groups, SSM_CH // LANES, t * SUBLANES, LANES), F32),
                jax.ShapeDtypeStruct((n_groups, 2, SUBLANES, SSM_N), F32),
            ),
            grid=(n_groups, n_chunks),
            in_specs=[
                pl.BlockSpec((SUBLANES, tc, SSM_CH), lambda g, i: (g, chunk(i), 0)),
                pl.BlockSpec((None, None, 2, SUBLANES, SSM_N), lambda g, i: (g, d, 0, 0, 0)),
                pl.BlockSpec((None, None, 2, SUBLANES, SSM_N), lambda g, i: (layer, d, 0, 0, 0)),
                pl.BlockSpec((None, None, 2, SSM_CH // 2, SSM_N), lambda g, i: (layer, d, 0, 0, 0)),
                pl.BlockSpec((None, None, 2, SSM_N // 2, SSM_CH // 2), lambda g, i: (layer, d, 0, 0, 0)),
                pl.BlockSpec((None, None, 2, SSM_N // 2, SSM_CH // 2), lambda g, i: (layer, d, 0, 0, 0)),
            ],
            out_specs=(
                pl.BlockSpec((None, SSM_CH // LANES, tc * SUBLANES, LANES), lambda g, i: (g, 0, chunk(i), 0)),
                pl.BlockSpec((None, 2, SUBLANES, SSM_N), lambda g, i: (g, 0, 0, 0)),
            ),
            scratch_shapes=_ssm_scratch(tc),
            compiler_params=_cparams(("parallel", "arbitrary")),
            name="ssm_scan_bwd" if d else "ssm_scan_fwd",
        )(u3, h0, sp['lam'], sp['b'], sp['c_re'], sp['c_im'])

    (y_f, h_f), (y_b, h_b) = one_direction(0), one_direction(1)
    return y_f, y_b, jnp.stack([h_f, h_b], axis=1)


def _ssm_scratch(tc):
    return [
            pltpu.VMEM((SSM_CH // LANES, tc * SUBLANES, LANES), F32),
            pltpu.VMEM((tc * SUBLANES, SSM_N // SSM_HALVES), F32),
            pltpu.VMEM((tc * SUBLANES, SSM_N // SSM_HALVES), F32),
            pltpu.VMEM((tc * SUBLANES, SSM_N // SSM_HALVES), F32),
            pltpu.VMEM((tc * SUBLANES, SSM_N // SSM_HALVES), F32),
            pltpu.VMEM((2, SUBLANES, SSM_N // SSM_HALVES), F32),
            pltpu.VMEM((2, SUBLANES, SSM_N // SSM_HALVES), F32),
        ]


def _ssm_glu_kernel(u_ref, yf_ref, yb_ref, d_ref, w_ref, o_ref, ycat_ref):
    tc = SSM_TC
    for b in range(SUBLANES):
        for j in range(SSM_CH // LANES):
            sl = slice(j * LANES, (j + 1) * LANES)
            rows_b = pl.ds(b, tc, stride=SUBLANES)
            ycat_ref[b * tc:(b + 1) * tc, sl] = (yf_ref.at[j][rows_b, :] + yb_ref.at[j][rows_b, :]
                                                + d_ref[:, sl] * u_ref[b, :, sl])
    zg = _dot(ycat_ref[...].astype(BF16), w_ref[...])
    out = zg[:, :SSM_CH] * jax.nn.sigmoid(zg[:, SSM_CH:])
    o_ref[...] = out.reshape(SUBLANES, tc, SSM_CH).astype(o_ref.dtype)


def _ssm_glu(u3, y_f, y_b, d_skip, w_glu, layer):
    bsz, t, _ = u3.shape
    n_groups = bsz // SUBLANES
    n_chunks = t // SSM_TC
    tc = SSM_TC
    yspec = pl.BlockSpec((None, SSM_CH // LANES, tc * SUBLANES, LANES), lambda g, i: (g, 0, i, 0))
    return pl.pallas_call(
        _ssm_glu_kernel,
        out_shape=jax.ShapeDtypeStruct((bsz, t, SSM_CH), BF16),
        grid=(n_groups, n_chunks),
        in_specs=[
            pl.BlockSpec((SUBLANES, tc, SSM_CH), lambda g, i: (g, i, 0)),
            yspec,
            yspec,
            _lspec((1, SSM_CH), layer),
            _lspec((SSM_CH, 2 * SSM_CH), layer),
        ],
        out_specs=pl.BlockSpec((SUBLANES, tc, SSM_CH), lambda g, i: (g, i, 0)),
        scratch_shapes=[pltpu.VMEM((SUBLANES * tc, SSM_CH), F32)],
        compiler_params=_cparams(("parallel", "parallel")),
        name="ssm_glu",
    )(u3, y_f, y_b, d_skip, w_glu)


def _ones_column(n):
    return jnp.ones((n, LANES), BF16)


def _softmax_pv(scores, v_ext):
    m = jnp.max(scores, axis=-1, keepdims=True)
    p = jnp.exp((scores - m).astype(BF16))
    o = _dot(p, v_ext)
    return o[:, :LANES] / o[:, LANES:]


ATTN_TQ = 256


def _for_each_q_block(nb, seq, body):
    nq = seq // ATTN_TQ
    for s in range(nb):
        for qi in range(nq):
            body(s, s * seq + qi * ATTN_TQ)


def _gqa_kernel(latent, nb, seq, *refs):
    if latent:
        q_ref, k_ref, v_ref, kc_ref, vc_ref, o_ref, k_s, v_s = refs
    else:
        q_ref, k_ref, v_ref, o_ref, k_s, v_s = refs
    tq = ATTN_TQ
    for s in range(nb):
        for h in range(GQA_KV_HEADS):
            hl = slice(h * LANES, (h + 1) * LANES)
            k_s[s, h, 0:seq, :] = k_ref[s * seq:(s + 1) * seq, hl].astype(BF16)
            v_s[s, h, 0:seq, :LANES] = v_ref[s * seq:(s + 1) * seq, hl].astype(BF16)
            if latent:
                k_s[s, h, seq:, :] = kc_ref[:, hl].astype(BF16)
                v_s[s, h, seq:, :LANES] = vc_ref[:, hl].astype(BF16)
            v_s[s, h, :, LANES:] = _ones_column(v_s.shape[2])

    def body(s, r0):
        for h in range(GQA_KV_HEADS):
            heads = [h * GQA_GROUP + g for g in range(GQA_GROUP)]
            q3 = jnp.concatenate([q_ref[pl.ds(r0, tq), hd * LANES:(hd + 1) * LANES] for hd in heads], axis=0)
            o = _softmax_pv(_dot_nt(q3, k_s[s, h]), v_s[s, h])
            for g, hd in enumerate(heads):
                o_ref[pl.ds(r0, tq), hd * LANES:(hd + 1) * LANES] = o[g * tq:(g + 1) * tq].astype(o_ref.dtype)

    _for_each_q_block(nb, seq, body)


def _gqa_attn(qb, kb, vb, latent, cache_k=None, cache_v=None, layer=0):
    rows = qb.shape[0]
    seq = DEC_SEQ if latent else SEQ
    nb = 1 if latent else 4
    t_all = seq + (PAST_LEN if latent else 0)
    qw = GQA_HEADS * GQA_HEAD_DIM
    kw = GQA_KV_HEADS * GQA_HEAD_DIM
    row = lambda i: (i, 0)
    in_specs = [
        pl.BlockSpec((nb * seq, qw), row),
        pl.BlockSpec((nb * seq, kw), row),
        pl.BlockSpec((nb * seq, kw), row),
    ]
    args = [qb, kb, vb]
    if latent:
        cspec = pl.BlockSpec((None, None, PAST_LEN, kw), lambda i: (i, layer, 0, 0))
        in_specs += [cspec, cspec]
        args += [cache_k, cache_v]
    return pl.pallas_call(
        functools.partial(_gqa_kernel, latent, nb, seq),
        out_shape=jax.ShapeDtypeStruct((rows, qw), BF16),
        grid=(rows // (nb * seq),),
        in_specs=in_specs,
        out_specs=pl.BlockSpec((nb * seq, qw), row),
        scratch_shapes=[
            pltpu.VMEM((nb, GQA_KV_HEADS, t_all, GQA_HEAD_DIM), BF16),
            pltpu.VMEM((nb, GQA_KV_HEADS, t_all, 2 * LANES), BF16),
        ],
        compiler_params=_cparams(("parallel",)),
        name="gqa_lat" if latent else "gqa_ctx",
    )(*args)


def _mla_kernel(latent, nb, seq, *refs):
    if latent:
        (q_ref, ckv_ref, kr_ref, krg_ref, ckvc_ref, krc_ref, wuk_ref, wuv_ref, gn_ref, gr_ref,
         o_ref, k_s, v_s) = refs
    else:
        (q_ref, ckv_ref, kr_ref, krg_ref, wuk_ref, wuv_ref, gn_ref, gr_ref, o_ref, k_s, v_s) = refs
    tq = ATTN_TQ

    def expand(ckv, kr_raw, krg, place):
        c = ckv.astype(BF16)
        kn_all = _dot(c, wuk_ref[...])
        v_all = _dot(c, wuv_ref[...])
        ss_kr = jnp.sum(kr_raw * kr_raw, axis=-1, keepdims=True)
        for h in range(MLA_HEADS):
            hl = slice(h * LANES, (h + 1) * LANES)
            kn = kn_all[:, hl]
            rs = lax.rsqrt((jnp.sum(kn * kn, axis=-1, keepdims=True) + ss_kr) * (1.0 / MLA_QK) + EPS)
            kn = (kn * rs * gn_ref[...]).astype(BF16)
            kp = (krg * rs).astype(BF16)
            vv = v_all[:, hl].astype(BF16)
            for s, lo, n, rows in place:
                k_s[s, h, lo:lo + n, :MLA_NOPE] = kn[rows]
                k_s[s, h, lo:lo + n, MLA_NOPE:MLA_QK] = kp[rows]
                k_s[s, h, lo:lo + n, MLA_QK:] = jnp.zeros((n, MLA_QK_PAD - MLA_QK), BF16)
                v_s[s, h, lo:lo + n, :MLA_V] = vv[rows]
                v_s[s, h, lo:lo + n, MLA_V:] = _ones_column(n)

    expand(ckv_ref[...], kr_ref[...], krg_ref[...],
           [(s, 0, seq, slice(s * seq, (s + 1) * seq)) for s in range(nb)])
    if latent:
        krc = krc_ref[...]
        expand(ckvc_ref[...], krc, krc * gr_ref[...], [(0, seq, PAST_LEN, slice(0, PAST_LEN))])

    def body(s, r0):
        for h in range(MLA_HEADS):
            sc = _dot_nt(q_ref[h, pl.ds(r0, tq), :], k_s[s, h])
            o_ref[pl.ds(r0, tq), h * LANES:(h + 1) * LANES] = _softmax_pv(sc, v_s[s, h]).astype(o_ref.dtype)

    _for_each_q_block(nb, seq, body)


def _mla_attn(qc, ckv, kr, krg, lw, latent, cache_ckv=None, cache_kr=None, layer=0):
    rows = qc.shape[1]
    seq = DEC_SEQ if latent else SEQ
    nb = 1 if latent else 4
    t_all = seq + (PAST_LEN if latent else 0)
    row = lambda i: (i, 0)
    in_specs = [
        pl.BlockSpec((MLA_HEADS, nb * seq, MLA_QK_PAD), lambda i: (0, i, 0)),
        pl.BlockSpec((nb * seq, MLA_KV_RANK), row),
        pl.BlockSpec((nb * seq, MLA_ROPE), row),
        pl.BlockSpec((nb * seq, MLA_ROPE), row),
    ]
    args = [qc, ckv, kr, krg]
    if latent:
        in_specs += [
            pl.BlockSpec((None, None, PAST_LEN, MLA_KV_RANK), lambda i: (i, layer, 0, 0)),
            pl.BlockSpec((None, None, PAST_LEN, MLA_ROPE), lambda i: (i, layer, 0, 0)),
        ]
        args += [cache_ckv, cache_kr]
    in_specs += [
        _lspec((MLA_KV_RANK, MLA_HEADS * MLA_NOPE), layer),
        _lspec((MLA_KV_RANK, MLA_HEADS * MLA_V), layer),
        _lspec((1, MLA_NOPE), layer),
        _lspec((1, MLA_ROPE), layer),
    ]
    args += [lw['mla_w_uk'], lw['mla_w_uv'], lw['mla_k_nope_g'], lw['mla_k_rope_g64']]
    return pl.pallas_call(
        functools.partial(_mla_kernel, latent, nb, seq),
        out_shape=jax.ShapeDtypeStruct((rows, MLA_HEADS * MLA_V), BF16),
        grid=(rows // (nb * seq),),
        in_specs=in_specs,
        out_specs=pl.BlockSpec((nb * seq, MLA_HEADS * MLA_V), row),
        scratch_shapes=[
            pltpu.VMEM((nb, MLA_HEADS, t_all, MLA_QK_PAD), BF16),
            pltpu.VMEM((nb, MLA_HEADS, t_all, MLA_V + LANES), BF16),
        ],
        compiler_params=_cparams(("parallel",)),
        name="mla_lat" if latent else "mla_ctx",
    )(*args)


def _out_proj_kernel(ya_ref, yb_ref, yc_ref, x_ref, mod_ref, w_ref, g_ref, x1_ref, h2_ref):
    wa = SSM_CH
    wb = wa + GQA_HEADS * GQA_HEAD_DIM
    piece = x_ref.shape[0] // ROW_PIECES
    pieces = [slice(r * piece, (r + 1) * piece) for r in range(ROW_PIECES)]
    outs = [(_dot(ya_ref[rs, :], w_ref[0:wa, :]) + _dot(yb_ref[rs, :], w_ref[wa:wb, :])
             + _dot(yc_ref[rs, :], w_ref[wb:, :])) for rs in pieces]
    for rs, o in zip(pieces, outs):
        x1 = x_ref[rs, :] + mod_ref[2:3, :] * o
        x1_ref[rs, :] = x1
        ms = jnp.mean(x1 * x1, axis=-1, keepdims=True)
        y = x1 * lax.rsqrt(ms + EPS) * g_ref[...]
        h2_ref[rs, :] = (y * (1.0 + mod_ref[4:5, :]) + mod_ref[3:4, :]).astype(h2_ref.dtype)


def _out_proj(ya, yb, yc, x2d, mod, lw, latent, tm, layer):
    rows = x2d.shape[0]
    row = lambda i: (i, 0)
    return pl.pallas_call(
        _out_proj_kernel,
        out_shape=(jax.ShapeDtypeStruct((rows, D_MODEL), F32), jax.ShapeDtypeStruct((rows, D_MODEL), BF16)),
        grid=(rows // tm,),
        in_specs=[
            pl.BlockSpec((tm, SSM_CH), row),
            pl.BlockSpec((tm, GQA_HEADS * GQA_HEAD_DIM), row),
            pl.BlockSpec((tm, MLA_HEADS * MLA_V), row),
            pl.BlockSpec((tm, D_MODEL), row),
            _mod_spec(latent, tm, layer),
            _lspec((D_MODEL, D_MODEL), layer),
            _lspec((1, D_MODEL), layer),
        ],
        out_specs=(pl.BlockSpec((tm, D_MODEL), row), pl.BlockSpec((tm, D_MODEL), row)),
        compiler_params=_cparams(("parallel",)),
        name="out_proj",
    )(ya, yb, yc, x2d, mod, lw['w_out'], lw['norm_mlp'])


def _mlp_kernel(cast_next, *refs):
    if cast_next:
        h2_ref, x1_ref, mod_ref, w1_ref, w2_ref, n1_ref, n2_ref, o_ref, c1_ref, c2_ref = refs
        c1_ref[...] = n1_ref[...].astype(BF16)
        c2_ref[...] = n2_ref[...].astype(BF16)
    else:
        h2_ref, x1_ref, mod_ref, w1_ref, w2_ref, o_ref = refs
    j = pl.program_id(1)

    @pl.when(j == 0)
    def _():
        o_ref[...] = jnp.zeros_like(o_ref)

    f = jnp.maximum(_dot(h2_ref[...], w1_ref[...]), 0.0)
    o_ref[...] += _dot((f * f).astype(BF16), w2_ref[...])

    @pl.when(j == pl.num_programs(1) - 1)
    def _():
        o_ref[...] = x1_ref[...] + mod_ref[5:6, :] * o_ref[...]


def _mlp(h2, x1, mod, w_ff, latent, tm, tf, layer, next_w=None):
    rows = h2.shape[0]
    ni, nj = rows // tm, D_FF // tf
    in_specs = [
        pl.BlockSpec((tm, D_MODEL), lambda i, j: (i, 0)),
        pl.BlockSpec((tm, D_MODEL), lambda i, j: (i, 0), pipeline_mode=pl.Buffered(1)),
        _mod_spec(latent, tm, layer),
        pl.BlockSpec((D_MODEL, tf), lambda i, j: (0, j)),
        pl.BlockSpec((tf, D_MODEL), lambda i, j: (j, 0)),
    ]
    out_shape = [jax.ShapeDtypeStruct((rows, D_MODEL), F32)]
    out_specs = [pl.BlockSpec((tm, D_MODEL), lambda i, j: (i, 0))]
    args = [h2, x1, mod, w_ff[0], w_ff[1]]
    if next_w is not None:
        r1, r2 = D_MODEL // (ni * nj), D_FF // (ni * nj)
        step = lambda i, j: i * nj + j
        in_specs += [pl.BlockSpec((None, r1, D_FF), lambda i, j: (layer + 1, step(i, j), 0)),
                     pl.BlockSpec((None, r2, D_MODEL), lambda i, j: (layer + 1, step(i, j), 0))]
        out_shape += [jax.ShapeDtypeStruct((D_MODEL, D_FF), BF16), jax.ShapeDtypeStruct((D_FF, D_MODEL), BF16)]
        out_specs += [pl.BlockSpec((r1, D_FF), lambda i, j: (step(i, j), 0)),
                      pl.BlockSpec((r2, D_MODEL), lambda i, j: (step(i, j), 0))]
        args += list(next_w)
    outs = pl.pallas_call(
        functools.partial(_mlp_kernel, next_w is not None),
        out_shape=tuple(out_shape),
        grid=(ni, nj),
        in_specs=in_specs,
        out_specs=tuple(out_specs),
        compiler_params=_cparams(("arbitrary", "arbitrary")),
        name="mlp_cast" if next_w is not None else "mlp",
    )(*args)
    return outs[0], (tuple(outs[1:]) if next_w is not None else None)


def _permute_w_in(w):
    base = OFF_QCN
    ckv0 = base + MLA_HEADS * MLA_QK
    qc = w[:, :, base:ckv0].reshape(DEPTH, D_MODEL, MLA_HEADS, MLA_QK)
    nope = qc[..., :MLA_NOPE].reshape(DEPTH, D_MODEL, MLA_HEADS * MLA_NOPE)
    rope = qc[..., MLA_NOPE:].reshape(DEPTH, D_MODEL, MLA_HEADS * MLA_ROPE)
    kr = w[:, :, ckv0 + MLA_KV_RANK:]
    parts = [w[:, :, :base], nope, rope, w[:, :, ckv0:ckv0 + MLA_KV_RANK], kr, kr]
    return jnp.concatenate(parts, axis=2).astype(BF16)


def _rope_tables(seq):
    t = jnp.arange(seq)
    row = (t // GRID_W).astype(F32)
    col = (t % GRID_W).astype(F32)

    def table(d):
        quarter = d // 4
        inv = ROPE_BASE ** (-(jnp.arange(quarter, dtype=F32) / quarter))
        ar = row[:, None] * inv[None, :]
        ac = col[:, None] * inv[None, :]
        cos = jnp.concatenate([jnp.cos(ar), jnp.cos(ar), jnp.cos(ac), jnp.cos(ac)], axis=-1)
        sin = jnp.concatenate([-jnp.sin(ar), jnp.sin(ar), -jnp.sin(ac), jnp.sin(ac)], axis=-1)
        reps = LANES // d
        return jnp.tile(cos, (1, reps)), jnp.tile(sin, (1, reps))

    cos_b, sin_b = table(GQA_HEAD_DIM)
    cos_c, sin_c = table(MLA_ROPE)
    return {'cos_b': cos_b, 'sin_b': sin_b, 'cos_c': cos_c, 'sin_c': sin_c}


def _ssm_params(lam_re, lam_im, log_dt, b_re, b_im, c_re, c_im):
    a = lam_re.astype(F32)
    w = lam_im.astype(F32)
    dt = jnp.exp(log_dt.astype(F32))[..., None]
    mag = jnp.exp(a * dt)
    lbr = mag * jnp.cos(w * dt)
    lbi = mag * jnp.sin(w * dt)
    den = a * a + w * w
    cr = (((lbr - 1.0) * a + lbi * w) / den)[..., None]
    ci = ((lbi * a - (lbr - 1.0) * w) / den)[..., None]
    bre = b_re.astype(F32)
    bim = b_im.astype(F32)
    bb_re = cr * bre - ci * bim
    bb_im = cr * bim + ci * bre
    lam_ri = jnp.stack([lbr.reshape(DEPTH, 2, SSM_N), lbi.reshape(DEPTH, 2, SSM_N)], axis=2)
    lam_b = jnp.broadcast_to(lam_ri[:, :, :, None, :], (DEPTH, 2, 2, SUBLANES, SSM_N))
    gh = SSM_GROUPS // 2
    eye = jnp.eye(gh, dtype=F32)[:, None, :, None]

    def blockdiag(x, rows_per_g, cols_per_g):
        x = x.reshape(DEPTH, 2, 2, gh, rows_per_g, 1, cols_per_g) * eye
        return x.reshape(DEPTH, 2, 2, gh * rows_per_g, gh * cols_per_g)

    def bmat(x):
        return blockdiag(jnp.swapaxes(x, -1, -2), SSM_GROUP, SSM_STATE)

    def cmat(x):
        return blockdiag(jnp.swapaxes(x, -1, -2), SSM_STATE, SSM_GROUP)

    b_cat = jnp.concatenate([bmat(bb_re), bmat(bb_im)], axis=-1).astype(BF16)
    return {'lam': lam_b, 'b': b_cat, 'c_re': cmat(c_re.astype(F32)).astype(BF16),
            'c_im': cmat(c_im.astype(F32)).astype(BF16)}


def _trunk_layer(x2d, mod, lw, sp, tabs, latent, ctx, layer, w_ff, next_w=None):
    seq = DEC_SEQ if latent else SEQ
    bsz = x2d.shape[0] // seq
    u, qb, kb, vb, qc, ckv, kr, krg = _in_proj(x2d, mod, latent, lw, tabs, TM_PROJ, layer)

    u3 = u.reshape(bsz, seq, SSM_CH)
    y_f, y_b, h_t = _ssm_scan(u3, ctx['h0'], sp, layer)
    ya = _ssm_glu(u3, y_f, y_b, lw['ssm_d'], lw['ssm_w_glu'], layer).reshape(bsz * seq, SSM_CH)

    if latent:
        yb = _gqa_attn(qb, kb, vb, True, ctx['k'], ctx['v'], layer)
        yc = _mla_attn(qc, ckv, kr, krg, lw, True, ctx['ckv'], ctx['kr'], layer)
    else:
        yb = _gqa_attn(qb, kb, vb, False)
        yc = _mla_attn(qc, ckv, kr, krg, lw, False, layer=layer)

    x1, h2 = _out_proj(ya, yb, yc, x2d, mod, lw, latent, TM_PROJ, layer)
    x2, w_ff_next = _mlp(h2, x1, mod, w_ff, latent, TM_MLP, TF_MLP, layer, next_w)
    return x2, (kb, vb, ckv, kr, h_t), w_ff_next


def kernel(x_prompt, x_sample, cache_attn_k, cache_attn_v, cache_mla_ckv, cache_mla_krope, state_ssm, c, c_ctx, w_mod, b_mod, norm_mix, norm_mlp, w_in, gqa_q_norm, gqa_k_norm, mla_kv_norm, mla_q_norm, mla_k_norm, mla_w_uk, mla_w_uv, ssm_lam_re, ssm_lam_im, ssm_log_dt, ssm_b_re, ssm_b_im, ssm_c_re, ssm_c_im, ssm_d, ssm_w_glu, w_out, w_ff1, w_ff2):
    cvec = jnp.zeros((N_MOD, D_MODEL), F32).at[0].set(c_ctx).at[1:1 + DEC_BATCH].set(c)
    mod = _adaln(cvec, w_mod, b_mod.reshape(DEPTH, 1, 6 * D_MODEL)).reshape(DEPTH, N_MOD, 6, D_MODEL)

    tabs = _rope_tables(DEC_SEQ)
    cache_k = cache_attn_k.reshape(DEC_BATCH, DEPTH, PAST_LEN, GQA_KV_HEADS * GQA_HEAD_DIM)
    cache_v = cache_attn_v.reshape(DEC_BATCH, DEPTH, PAST_LEN, GQA_KV_HEADS * GQA_HEAD_DIM)

    xp = x_prompt.reshape(BATCH * SEQ, D_MODEL)
    xs = x_sample.reshape(DEC_BATCH * DEC_SEQ, D_MODEL)
    new_k, new_v, new_ckv, new_kr, new_ssm = [], [], [], [], []
    h0_zero = jnp.zeros((BATCH // SUBLANES, 2, 2, SUBLANES, SSM_N), F32)

    row3 = lambda a: a.reshape(DEPTH, 1, -1)
    dup = lambda a: jnp.concatenate([a, a], axis=-1)
    lw = {
        'norm_mix': row3(norm_mix), 'norm_mlp': row3(norm_mlp),
        'w_in': _permute_w_in(w_in),
        'w_out': w_out.astype(BF16),
        'gqa_q_norm': row3(gqa_q_norm), 'gqa_k_norm': row3(gqa_k_norm), 'mla_kv_norm': row3(mla_kv_norm),
        'mla_q_nope_g': row3(mla_q_norm[:, :MLA_NOPE]),
        'mla_q_rope_g': row3(dup(mla_q_norm[:, MLA_NOPE:])),
        'mla_k_nope_g': row3(mla_k_norm[:, :MLA_NOPE]),
        'mla_k_rope_g': row3(dup(mla_k_norm[:, MLA_NOPE:])),
        'mla_k_rope_g64': row3(mla_k_norm[:, MLA_NOPE:]),
        'mla_w_uk': mla_w_uk.reshape(DEPTH, MLA_KV_RANK, MLA_HEADS * MLA_NOPE).astype(BF16),
        'mla_w_uv': mla_w_uv.reshape(DEPTH, MLA_KV_RANK, MLA_HEADS * MLA_V).astype(BF16),
        'ssm_d': row3(ssm_d), 'ssm_w_glu': ssm_w_glu.astype(BF16),
    }
    sp = _ssm_params(ssm_lam_re, ssm_lam_im, ssm_log_dt, ssm_b_re, ssm_b_im, ssm_c_re, ssm_c_im)
    h0_lat = state_ssm.reshape(DEC_BATCH, DEPTH, 2, SSM_N, 2).transpose(1, 2, 4, 0, 3)[:, None]

    w_ff = (w_ff1[0].astype(BF16), w_ff2[0].astype(BF16))
    for l in range(DEPTH):
        next_w = (w_ff1, w_ff2) if l + 1 < DEPTH else None
        xp, (k, v, ckv_n, kr, h_t), w_ff_next = _trunk_layer(xp, mod, lw, sp, tabs, False, {'h0': h0_zero}, l,
                                                            w_ff, next_w)
        new_k.append(k.reshape(BATCH, SEQ, GQA_KV_HEADS, GQA_HEAD_DIM))
        new_v.append(v.reshape(BATCH, SEQ, GQA_KV_HEADS, GQA_HEAD_DIM))
        new_ckv.append(ckv_n.reshape(BATCH, SEQ, MLA_KV_RANK))
        new_kr.append(kr.reshape(BATCH, SEQ, MLA_ROPE))
        hs = h_t.transpose(0, 3, 1, 4, 2).reshape(BATCH, 2, SSM_GROUPS, SSM_STATE, 2)
        new_ssm.append(hs)

        ctx = {'k': cache_k, 'v': cache_v, 'ckv': cache_mla_ckv, 'kr': cache_mla_krope, 'h0': h0_lat[l]}
        xs, _, _ = _trunk_layer(xs, mod, lw, sp, tabs, True, ctx, l, w_ff)
        w_ff = w_ff_next

    return (xp.reshape(BATCH, SEQ, D_MODEL), xs.reshape(DEC_BATCH, DEC_SEQ, D_MODEL),
            jnp.stack(new_k, axis=1), jnp.stack(new_v, axis=1), jnp.stack(new_ckv, axis=1),
            jnp.stack(new_kr, axis=1), jnp.stack(new_ssm, axis=1))
```

```python
import functools
import math

import jax
import jax.numpy as jnp
from jax import lax
from jax.experimental import pallas as pl
from jax.experimental.pallas import tpu as pltpu

D_MODEL = 2048
BATCH = 32
SEQ = 256
DEPTH = 2
DEC_BATCH = 8
DEC_SEQ = 1024
PAST_LEN = 512
GRID_W = 64
ROPE_BASE = 10000.0
EPS = 1e-6
SSM_CH = 512
SSM_GROUP = 16
SSM_GROUPS = SSM_CH // SSM_GROUP
SSM_STATE = 64
SSM_N = SSM_GROUPS * SSM_STATE
GQA_HEADS = 6
GQA_KV_HEADS = 2
GQA_GROUP = GQA_HEADS // GQA_KV_HEADS
GQA_HEAD_DIM = 128
MLA_HEADS = 6
MLA_NOPE = 128
MLA_ROPE = 64
MLA_QK = MLA_NOPE + MLA_ROPE
MLA_QK_PAD = 256
MLA_V = 128
MLA_KV_RANK = 512
D_FF = 4 * D_MODEL
N_MOD = 16

OFF_U = 0
OFF_QB = OFF_U + SSM_CH
OFF_KB = OFF_QB + GQA_HEADS * GQA_HEAD_DIM
OFF_VB = OFF_KB + GQA_KV_HEADS * GQA_HEAD_DIM
OFF_QCN = OFF_VB + GQA_KV_HEADS * GQA_HEAD_DIM
OFF_QCR = OFF_QCN + MLA_HEADS * MLA_NOPE
OFF_CKV = OFF_QCR + MLA_HEADS * MLA_ROPE
OFF_KR = OFF_CKV + MLA_KV_RANK
IN_WIDTH_P = OFF_KR + 2 * MLA_ROPE

LANES = 128
SUBLANES = 8
VMEM_LIMIT = 56 * 1024 * 1024

TM_PROJ = 512
TM_MLP = 1024
TF_MLP = 512
ROW_PIECES = 2
IN_PROJ_PIECES = 2

BF16 = jnp.bfloat16
F32 = jnp.float32


def _cparams(sem):
    return pltpu.CompilerParams(dimension_semantics=sem, vmem_limit_bytes=VMEM_LIMIT)


def _dot(a, b):
    return jnp.dot(a, b, preferred_element_type=F32)


def _dot_nt(a, b):
    return lax.dot_general(a, b, (((1,), (1,)), ((), ())), preferred_element_type=F32)


def _adaln_kernel(c_ref, w_ref, b_ref, o_ref):
    c = c_ref[...]
    s = (c * jax.nn.sigmoid(c)).astype(BF16)
    o_ref[...] = _dot(s, w_ref[...].astype(BF16)) + b_ref[...]


def _adaln(cvec, w_mod, b_mod):
    tn = 1024
    return pl.pallas_call(
        _adaln_kernel,
        out_shape=jax.ShapeDtypeStruct((DEPTH, N_MOD, 6 * D_MODEL), F32),
        grid=(DEPTH, 6 * D_MODEL // tn),
        in_specs=[
            pl.BlockSpec((N_MOD, D_MODEL), lambda l, j: (0, 0)),
            pl.BlockSpec((None, D_MODEL, tn), lambda l, j: (l, 0, j)),
            pl.BlockSpec((None, 1, tn), lambda l, j: (l, 0, j)),
        ],
        out_specs=pl.BlockSpec((None, N_MOD, tn), lambda l, j: (l, 0, j)),
        compiler_params=_cparams(("parallel", "parallel")),
        name="adaln",
    )(cvec, w_mod, b_mod)


def _swap_halves(x, block):
    lane = lax.broadcasted_iota(jnp.int32, x.shape, 1)
    first = (lane % (2 * block)) < block
    return jnp.where(first, pltpu.roll(x, LANES - block, 1), pltpu.roll(x, block, 1))


def _rope(x, cos, sin_signed, block):
    return x * cos + _swap_halves(x, block) * sin_signed


def _in_proj_kernel(latent, x_ref, mod_ref, gmix_ref, w_ref, gq_ref, gk_ref, gkv_ref, gqn_ref, gqr_ref, gkr_ref,
                    cosb_ref, sinb_ref, cosc_ref, sinc_ref,
                    u_ref, qb_ref, kb_ref, vb_ref, qc_ref, ckv_ref, kr_ref, krg_ref):
    piece = x_ref.shape[0] // IN_PROJ_PIECES
    for r in range(IN_PROJ_PIECES):
        rs = pl.ds(r * piece, piece)
        _in_proj_rows(latent, x_ref.at[rs], mod_ref, gmix_ref, w_ref, gq_ref, gk_ref, gkv_ref, gqn_ref, gqr_ref,
                      gkr_ref, cosb_ref.at[rs], sinb_ref.at[rs], cosc_ref.at[rs], sinc_ref.at[rs],
                      u_ref.at[rs], qb_ref.at[rs], kb_ref.at[rs], vb_ref.at[rs], qc_ref.at[:, rs], ckv_ref.at[rs],
                      kr_ref.at[rs], krg_ref.at[rs])


def _in_proj_rows(latent, x_ref, mod_ref, gmix_ref, w_ref, gq_ref, gk_ref, gkv_ref, gqn_ref, gqr_ref, gkr_ref,
                  cosb_ref, sinb_ref, cosc_ref, sinc_ref,
                  u_ref, qb_ref, kb_ref, vb_ref, qc_ref, ckv_ref, kr_ref, krg_ref):
    x = x_ref[...]
    ms = jnp.mean(x * x, axis=-1, keepdims=True)
    y = x * lax.rsqrt(ms + EPS) * gmix_ref[...]
    h = (y * (1.0 + mod_ref[1:2, :]) + mod_ref[0:1, :]).astype(BF16)

    def proj(off, width):
        return _dot(h, w_ref[:, off:off + width])

    scale_b = 1.0 / math.sqrt(GQA_HEAD_DIM)
    zq = proj(OFF_QB, GQA_HEADS * GQA_HEAD_DIM)
    for hd in range(GQA_HEADS):
        col = zq[:, hd * LANES:(hd + 1) * LANES]
        q = col * lax.rsqrt(jnp.mean(col * col, axis=-1, keepdims=True) + EPS) * gq_ref[...]
        if latent:
            q = _rope(q, cosb_ref[...], sinb_ref[...], GQA_HEAD_DIM // 4)
        qb_ref[:, hd * LANES:(hd + 1) * LANES] = (q * scale_b).astype(qb_ref.dtype)
    zk = proj(OFF_KB, GQA_KV_HEADS * GQA_HEAD_DIM)
    for hd in range(GQA_KV_HEADS):
        col = zk[:, hd * LANES:(hd + 1) * LANES]
        k = col * lax.rsqrt(jnp.mean(col * col, axis=-1, keepdims=True) + EPS) * gk_ref[...]
        if latent:
            k = _rope(k, cosb_ref[...], sinb_ref[...], GQA_HEAD_DIM // 4)
        kb_ref[:, hd * LANES:(hd + 1) * LANES] = k.astype(kb_ref.dtype)

    scale_c = 1.0 / math.sqrt(MLA_QK)
    zn = proj(OFF_QCN, MLA_HEADS * MLA_NOPE)
    zr = proj(OFF_QCR, MLA_HEADS * MLA_ROPE)
    lane = lax.broadcasted_iota(jnp.int32, (x.shape[0], LANES), 1)
    low = lane < MLA_ROPE
    for pair in range(MLA_HEADS // 2):
        colr = zr[:, pair * LANES:(pair + 1) * LANES]
        sq = colr * colr
        ss_lo = jnp.sum(jnp.where(low, sq, 0.0), axis=-1, keepdims=True)
        ss_hi = jnp.sum(jnp.where(low, 0.0, sq), axis=-1, keepdims=True)
        rs = []
        for half, ss_r in ((0, ss_lo), (1, ss_hi)):
            hd = 2 * pair + half
            coln = zn[:, hd * LANES:(hd + 1) * LANES]
            ss = jnp.sum(coln * coln, axis=-1, keepdims=True) + ss_r
            r = lax.rsqrt(ss * (1.0 / MLA_QK) + EPS)
            rs.append(r)
            qc_ref[hd, :, :MLA_NOPE] = (coln * r * gqn_ref[...] * scale_c).astype(qc_ref.dtype)
        qr = colr * jnp.where(low, rs[0], rs[1]) * gqr_ref[...]
        if latent:
            qr = _rope(qr, cosc_ref[...], sinc_ref[...], MLA_ROPE // 4)
        qr = qr * scale_c
        zeros = jnp.zeros((x.shape[0], MLA_QK_PAD - MLA_QK), qc_ref.dtype)
        for half in range(2):
            hd = 2 * pair + half
            qc_ref[hd, :, MLA_NOPE:MLA_QK] = qr[:, half * MLA_ROPE:(half + 1) * MLA_ROPE].astype(qc_ref.dtype)
            qc_ref[hd, :, MLA_QK:] = zeros

    zc = proj(OFF_CKV, MLA_KV_RANK)
    ckv_ref[...] = (zc * lax.rsqrt(jnp.mean(zc * zc, axis=-1, keepdims=True) + EPS) * gkv_ref[...]).astype(ckv_ref.dtype)
    zkr = proj(OFF_KR, 2 * MLA_ROPE)
    kr_ref[...] = zkr[:, :MLA_ROPE]
    krg = zkr * gkr_ref[...]
    if latent:
        krg = _rope(krg, cosc_ref[...], sinc_ref[...], MLA_ROPE // 4)
    krg_ref[...] = krg[:, :MLA_ROPE]

    vb_ref[...] = proj(OFF_VB, GQA_KV_HEADS * GQA_HEAD_DIM).astype(vb_ref.dtype)
    u_ref[...] = proj(OFF_U, SSM_CH)


def _lspec(block_tail, layer):
    zeros = (0,) * len(block_tail)
    return pl.BlockSpec((None,) + tuple(block_tail), lambda *_: (layer,) + zeros)


def _mod_spec(latent, tm, layer):
    tiles_per_seq = DEC_SEQ // tm
    if latent:
        return pl.BlockSpec((None, None, 6, D_MODEL), lambda i, *_: (layer, 1 + i // tiles_per_seq, 0, 0))
    return pl.BlockSpec((None, None, 6, D_MODEL), lambda i, *_: (layer, 0, 0, 0))


def _in_proj(x2d, mod, latent, lw, tabs, tm, layer):
    rows = x2d.shape[0]
    n_tiles = rows // tm
    tiles_per_seq = DEC_SEQ // tm
    if latent:
        tab_map = lambda i: (i % tiles_per_seq, 0)
    else:
        tab_map = lambda i: (0, 0)
    row = lambda i: (i, 0)
    act_dt = BF16 if latent else F32
    in_specs = [
        pl.BlockSpec((tm, D_MODEL), row),
        _mod_spec(latent, tm, layer),
        _lspec((1, D_MODEL), layer),
        _lspec((D_MODEL, IN_WIDTH_P), layer),
        _lspec((1, LANES), layer),
        _lspec((1, LANES), layer),
        _lspec((1, MLA_KV_RANK), layer),
        _lspec((1, LANES), layer),
        _lspec((1, LANES), layer),
        _lspec((1, LANES), layer),
        pl.BlockSpec((tm, LANES), tab_map),
        pl.BlockSpec((tm, LANES), tab_map),
        pl.BlockSpec((tm, LANES), tab_map),
        pl.BlockSpec((tm, LANES), tab_map),
    ]
    out_shape = (
        jax.ShapeDtypeStruct((rows, SSM_CH), F32),
        jax.ShapeDtypeStruct((rows, GQA_HEADS * GQA_HEAD_DIM), BF16),
        jax.ShapeDtypeStruct((rows, GQA_KV_HEADS * GQA_HEAD_DIM), act_dt),
        jax.ShapeDtypeStruct((rows, GQA_KV_HEADS * GQA_HEAD_DIM), act_dt),
        jax.ShapeDtypeStruct((MLA_HEADS, rows, MLA_QK_PAD), BF16),
        jax.ShapeDtypeStruct((rows, MLA_KV_RANK), act_dt),
        jax.ShapeDtypeStruct((rows, MLA_ROPE), F32),
        jax.ShapeDtypeStruct((rows, MLA_ROPE), F32),
    )
    out_specs = (
        pl.BlockSpec((tm, SSM_CH), row),
        pl.BlockSpec((tm, GQA_HEADS * GQA_HEAD_DIM), row),
        pl.BlockSpec((tm, GQA_KV_HEADS * GQA_HEAD_DIM), row),
        pl.BlockSpec((tm, GQA_KV_HEADS * GQA_HEAD_DIM), row),
        pl.BlockSpec((MLA_HEADS, tm, MLA_QK_PAD), lambda i: (0, i, 0)),
        pl.BlockSpec((tm, MLA_KV_RANK), row),
        pl.BlockSpec((tm, MLA_ROPE), row),
        pl.BlockSpec((tm, MLA_ROPE), row),
    )
    return pl.pallas_call(
        functools.partial(_in_proj_kernel, latent),
        out_shape=out_shape,
        grid=(n_tiles,),
        in_specs=in_specs,
        out_specs=out_specs,
        compiler_params=_cparams(("parallel",)),
        name="in_proj_lat" if latent else "in_proj_ctx",
    )(x2d, mod, lw['norm_mix'], lw['w_in'], lw['gqa_q_norm'], lw['gqa_k_norm'], lw['mla_kv_norm'],
      lw['mla_q_nope_g'], lw['mla_q_rope_g'], lw['mla_k_rope_g'],
      tabs['cos_b'], tabs['sin_b'], tabs['cos_c'], tabs['sin_c'])


SSM_TC = 128
SSM_HALVES = 2


def _ssm_scan_kernel(n_chunks, backward, *refs):
    if backward:
        (u_ref, h0_ref, lam_ref, b_ref, cre_ref, cim_ref, yf_ref, dsk_ref, wglu_ref, o_ref, hT_ref,
         utm_ref, sre0_ref, sim0_ref, sre1_ref, sim1_ref, h0s_ref, h1s_ref, y_ref, ycat_ref) = refs
    else:
        (u_ref, h0_ref, lam_ref, b_ref, cre_ref, cim_ref, y_ref, hT_ref,
         utm_ref, sre0_ref, sim0_ref, sre1_ref, sim1_ref, h0s_ref, h1s_ref) = refs
    i = pl.program_id(1)
    tc = SSM_TC
    half_n = SSM_N // SSM_HALVES
    halves = ((sre0_ref, sim0_ref, h0s_ref), (sre1_ref, sim1_ref, h1s_ref))

    @pl.when(i == 0)
    def _():
        for k, (_, _, hk_ref) in enumerate(halves):
            hk_ref[...] = h0_ref[:, :, k * half_n:(k + 1) * half_n]

    for b in range(SUBLANES):
        for j in range(SSM_CH // LANES):
            utm_ref[j, pl.ds(b, tc, stride=SUBLANES), :] = u_ref[b, :, j * LANES:(j + 1) * LANES]

    for k, (sre_ref, sim_ref, _) in enumerate(halves):
        uk = jnp.concatenate([utm_ref[2 * k], utm_ref[2 * k + 1]], axis=-1).astype(BF16)
        r = _dot(uk, b_ref[k])
        sre_ref[...] = r[:, :half_n]
        sim_ref[...] = r[:, half_n:]

    for k, (sre_ref, sim_ref, hk_ref) in enumerate(halves):
        sl = slice(k * half_n, (k + 1) * half_n)
        lr = lam_ref[0, :, sl]
        li = lam_ref[1, :, sl]
        hr = hk_ref[0]
        hi = hk_ref[1]
        for s in range(tc):
            t = (tc - 1 - s) if backward else s
            rows = slice(t * SUBLANES, (t + 1) * SUBLANES)
            hr, hi = (lr * hr - li * hi + sre_ref[rows, :], lr * hi + li * hr + sim_ref[rows, :])
            sre_ref[rows, :] = hr
            sim_ref[rows, :] = hi
        hk_ref[0] = hr
        hk_ref[1] = hi

    for k, (sre_ref, sim_ref, _) in enumerate(halves):
        yk = _dot(sre_ref[...].astype(BF16), cre_ref[k]) - _dot(sim_ref[...].astype(BF16), cim_ref[k])
        y_ref[2 * k] = yk[:, :LANES]
        y_ref[2 * k + 1] = yk[:, LANES:]

    if backward:
        for b in range(SUBLANES):
            for j in range(SSM_CH // LANES):
                sl = slice(j * LANES, (j + 1) * LANES)
                rows_b = pl.ds(b, tc, stride=SUBLANES)
                ycat_ref[b * tc:(b + 1) * tc, sl] = (yf_ref.at[j][rows_b, :] + y_ref.at[j][rows_b, :]
                                                    + dsk_ref[:, sl] * u_ref[b, :, sl])
        zg = _dot(ycat_ref[...].astype(BF16), wglu_ref[...])
        out = zg[:, :SSM_CH] * jax.nn.sigmoid(zg[:, SSM_CH:])
        o_ref[...] = out.reshape(SUBLANES, tc, SSM_CH).astype(o_ref.dtype)

    @pl.when(i == n_chunks - 1)
    def _():
        for k, (_, _, hk_ref) in enumerate(halves):
            hT_ref[:, :, k * half_n:(k + 1) * half_n] = hk_ref[...]


def _ssm_scan(u3, h0, sp, d_skip, w_glu, layer):
    bsz, t, _ = u3.shape
    n_groups = bsz // SUBLANES
    n_chunks = t // SSM_TC
    tc = SSM_TC
    y_shape = jax.ShapeDtypeStruct((n_groups, SSM_CH // LANES, t * SUBLANES, LANES), F32)
    h_shape = jax.ShapeDtypeStruct((n_groups, 2, SUBLANES, SSM_N), F32)
    h_spec = pl.BlockSpec((None, 2, SUBLANES, SSM_N), lambda g, i: (g, 0, 0, 0))

    def one_direction(d, extra_args, extra_specs, out_shape, out_spec, extra_scratch):
        chunk = (lambda i: n_chunks - 1 - i) if d else (lambda i: i)
        y_like = lambda g, i: (g, 0, chunk(i), 0)
        return pl.pallas_call(
            functools.partial(_ssm_scan_kernel, n_chunks, d),
            out_shape=(out_shape, h_shape),
            grid=(n_groups, n_chunks),
            in_specs=[
                pl.BlockSpec((SUBLANES, tc, SSM_CH), lambda g, i: (g, chunk(i), 0)),
                pl.BlockSpec((None, None, 2, SUBLANES, SSM_N), lambda g, i: (g, d, 0, 0, 0)),
                pl.BlockSpec((None, None, 2, SUBLANES, SSM_N), lambda g, i: (layer, d, 0, 0, 0)),
                pl.BlockSpec((None, None, 2, SSM_CH // 2, SSM_N), lambda g, i: (layer, d, 0, 0, 0)),
                pl.BlockSpec((None, None, 2, SSM_N // 2, SSM_CH // 2), lambda g, i: (layer, d, 0, 0, 0)),
                pl.BlockSpec((None, None, 2, SSM_N // 2, SSM_CH // 2), lambda g, i: (layer, d, 0, 0, 0)),
            ] + extra_specs(y_like),
            out_specs=(out_spec(chunk), h_spec),
            scratch_shapes=_ssm_scratch(tc) + extra_scratch,
            compiler_params=_cparams(("parallel", "arbitrary")),
            name="ssm_scan_bwd" if d else "ssm_scan_fwd",
        )(u3, h0, sp['lam'], sp['b'], sp['c_re'], sp['c_im'], *extra_args)

    y_block = (None, SSM_CH // LANES, tc * SUBLANES, LANES)
    y_f, h_f = one_direction(0, [], lambda y_like: [], y_shape,
                             lambda chunk: pl.BlockSpec(y_block, lambda g, i: (g, 0, chunk(i), 0)), [])
    ya, h_b = one_direction(
        1, [y_f, d_skip, w_glu],
        lambda y_like: [pl.BlockSpec(y_block, y_like), _lspec((1, SSM_CH), layer),
                        _lspec((SSM_CH, 2 * SSM_CH), layer)],
        jax.ShapeDtypeStruct((bsz, t, SSM_CH), BF16),
        lambda chunk: pl.BlockSpec((SUBLANES, tc, SSM_CH), lambda g, i: (g, chunk(i), 0)),
        [pltpu.VMEM((SSM_CH // LANES, tc * SUBLANES, LANES), F32), pltpu.VMEM((SUBLANES * tc, SSM_CH), F32)])
    return ya, jnp.stack([h_f, h_b], axis=1)


def _ssm_scratch(tc):
    return [
            pltpu.VMEM((SSM_CH // LANES, tc * SUBLANES, LANES), F32),
            pltpu.VMEM((tc * SUBLANES, SSM_N // SSM_HALVES), F32),
            pltpu.VMEM((tc * SUBLANES, SSM_N // SSM_HALVES), F32),
            pltpu.VMEM((tc * SUBLANES, SSM_N // SSM_HALVES), F32),
            pltpu.VMEM((tc * SUBLANES, SSM_N // SSM_HALVES), F32),
            pltpu.VMEM((2, SUBLANES, SSM_N // SSM_HALVES), F32),
            pltpu.VMEM((2, SUBLANES, SSM_N // SSM_HALVES), F32),
        ]


def _ones_column(n):
    return jnp.ones((n, LANES), BF16)


def _softmax_pv(scores, v_ext):
    m = jnp.max(scores, axis=-1, keepdims=True)
    p = jnp.exp((scores - m).astype(BF16))
    o = _dot(p, v_ext)
    return o[:, :LANES] / o[:, LANES:]


ATTN_TQ = 256


def _for_each_q_block(nb, seq, body):
    nq = seq // ATTN_TQ
    for s in range(nb):
        for qi in range(nq):
            body(s, s * seq + qi * ATTN_TQ)


def _gqa_kernel(latent, nb, seq, *refs):
    if latent:
        q_ref, k_ref, v_ref, kc_ref, vc_ref, o_ref, k_s, v_s = refs
    else:
        q_ref, k_ref, v_ref, o_ref, k_s, v_s = refs
    tq = ATTN_TQ
    for s in range(nb):
        for h in range(GQA_KV_HEADS):
            hl = slice(h * LANES, (h + 1) * LANES)
            k_s[s, h, 0:seq, :] = k_ref[s * seq:(s + 1) * seq, hl].astype(BF16)
            v_s[s, h, 0:seq, :LANES] = v_ref[s * seq:(s + 1) * seq, hl].astype(BF16)
            if latent:
                k_s[s, h, seq:, :] = kc_ref[:, hl].astype(BF16)
                v_s[s, h, seq:, :LANES] = vc_ref[:, hl].astype(BF16)
            v_s[s, h, :, LANES:] = _ones_column(v_s.shape[2])

    def body(s, r0):
        for h in range(GQA_KV_HEADS):
            heads = [h * GQA_GROUP + g for g in range(GQA_GROUP)]
            q3 = jnp.concatenate([q_ref[pl.ds(r0, tq), hd * LANES:(hd + 1) * LANES] for hd in heads], axis=0)
            o = _softmax_pv(_dot_nt(q3, k_s[s, h]), v_s[s, h])
            for g, hd in enumerate(heads):
                o_ref[pl.ds(r0, tq), hd * LANES:(hd + 1) * LANES] = o[g * tq:(g + 1) * tq].astype(o_ref.dtype)

    _for_each_q_block(nb, seq, body)


def _gqa_attn(qb, kb, vb, latent, cache_k=None, cache_v=None, layer=0):
    rows = qb.shape[0]
    seq = DEC_SEQ if latent else SEQ
    nb = 1 if latent else 4
    t_all = seq + (PAST_LEN if latent else 0)
    qw = GQA_HEADS * GQA_HEAD_DIM
    kw = GQA_KV_HEADS * GQA_HEAD_DIM
    row = lambda i: (i, 0)
    in_specs = [
        pl.BlockSpec((nb * seq, qw), row),
        pl.BlockSpec((nb * seq, kw), row),
        pl.BlockSpec((nb * seq, kw), row),
    ]
    args = [qb, kb, vb]
    if latent:
        cspec = pl.BlockSpec((None, None, PAST_LEN, kw), lambda i: (i, layer, 0, 0))
        in_specs += [cspec, cspec]
        args += [cache_k, cache_v]
    return pl.pallas_call(
        functools.partial(_gqa_kernel, latent, nb, seq),
        out_shape=jax.ShapeDtypeStruct((rows, qw), BF16),
        grid=(rows // (nb * seq),),
        in_specs=in_specs,
        out_specs=pl.BlockSpec((nb * seq, qw), row),
        scratch_shapes=[
            pltpu.VMEM((nb, GQA_KV_HEADS, t_all, GQA_HEAD_DIM), BF16),
            pltpu.VMEM((nb, GQA_KV_HEADS, t_all, 2 * LANES), BF16),
        ],
        compiler_params=_cparams(("parallel",)),
        name="gqa_lat" if latent else "gqa_ctx",
    )(*args)


def _mla_kernel(latent, nb, seq, *refs):
    if latent:
        (q_ref, ckv_ref, kr_ref, krg_ref, ckvc_ref, krc_ref, wuk_ref, wuv_ref, gn_ref, gr_ref,
         o_ref, k_s, v_s) = refs
    else:
        (q_ref, ckv_ref, kr_ref, krg_ref, wuk_ref, wuv_ref, gn_ref, gr_ref, o_ref, k_s, v_s) = refs
    tq = ATTN_TQ

    def expand(ckv, kr_raw, krg, place):
        c = ckv.astype(BF16)
        kn_all = _dot(c, wuk_ref[...])
        v_all = _dot(c, wuv_ref[...])
        ss_kr = jnp.sum(kr_raw * kr_raw, axis=-1, keepdims=True)
        for h in range(MLA_HEADS):
            hl = slice(h * LANES, (h + 1) * LANES)
            kn = kn_all[:, hl]
            rs = lax.rsqrt((jnp.sum(kn * kn, axis=-1, keepdims=True) + ss_kr) * (1.0 / MLA_QK) + EPS)
            kn = (kn * rs * gn_ref[...]).astype(BF16)
            kp = (krg * rs).astype(BF16)
            vv = v_all[:, hl].astype(BF16)
            for s, lo, n, rows in place:
                k_s[s, h, lo:lo + n, :MLA_NOPE] = kn[rows]
                k_s[s, h, lo:lo + n, MLA_NOPE:MLA_QK] = kp[rows]
                k_s[s, h, lo:lo + n, MLA_QK:] = jnp.zeros((n, MLA_QK_PAD - MLA_QK), BF16)
                v_s[s, h, lo:lo + n, :MLA_V] = vv[rows]
                v_s[s, h, lo:lo + n, MLA_V:] = _ones_column(n)

    expand(ckv_ref[...], kr_ref[...], krg_ref[...],
           [(s, 0, seq, slice(s * seq, (s + 1) * seq)) for s in range(nb)])
    if latent:
        krc = krc_ref[...]
        expand(ckvc_ref[...], krc, krc * gr_ref[...], [(0, seq, PAST_LEN, slice(0, PAST_LEN))])

    def body(s, r0):
        for h in range(MLA_HEADS):
            sc = _dot_nt(q_ref[h, pl.ds(r0, tq), :], k_s[s, h])
            o_ref[pl.ds(r0, tq), h * LANES:(h + 1) * LANES] = _softmax_pv(sc, v_s[s, h]).astype(o_ref.dtype)

    _for_each_q_block(nb, seq, body)


def _mla_attn(qc, ckv, kr, krg, lw, latent, cache_ckv=None, cache_kr=None, layer=0):
    rows = qc.shape[1]
    seq = DEC_SEQ if latent else SEQ
    nb = 1 if latent else 4
    t_all = seq + (PAST_LEN if latent else 0)
    row = lambda i: (i, 0)
    in_specs = [
        pl.BlockSpec((MLA_HEADS, nb * seq, MLA_QK_PAD), lambda i: (0, i, 0)),
        pl.BlockSpec((nb * seq, MLA_KV_RANK), row),
        pl.BlockSpec((nb * seq, MLA_ROPE), row),
        pl.BlockSpec((nb * seq, MLA_ROPE), row),
    ]
    args = [qc, ckv, kr, krg]
    if latent:
        in_specs += [
            pl.BlockSpec((None, None, PAST_LEN, MLA_KV_RANK), lambda i: (i, layer, 0, 0)),
            pl.BlockSpec((None, None, PAST_LEN, MLA_ROPE), lambda i: (i, layer, 0, 0)),
        ]
        args += [cache_ckv, cache_kr]
    in_specs += [
        _lspec((MLA_KV_RANK, MLA_HEADS * MLA_NOPE), layer),
        _lspec((MLA_KV_RANK, MLA_HEADS * MLA_V), layer),
        _lspec((1, MLA_NOPE), layer),
        _lspec((1, MLA_ROPE), layer),
    ]
    args += [lw['mla_w_uk'], lw['mla_w_uv'], lw['mla_k_nope_g'], lw['mla_k_rope_g64']]
    return pl.pallas_call(
        functools.partial(_mla_kernel, latent, nb, seq),
        out_shape=jax.ShapeDtypeStruct((rows, MLA_HEADS * MLA_V), BF16),
        grid=(rows // (nb * seq),),
        in_specs=in_specs,
        out_specs=pl.BlockSpec((nb * seq, MLA_HEADS * MLA_V), row),
        scratch_shapes=[
            pltpu.VMEM((nb, MLA_HEADS, t_all, MLA_QK_PAD), BF16),
            pltpu.VMEM((nb, MLA_HEADS, t_all, MLA_V + LANES), BF16),
        ],
        compiler_params=_cparams(("parallel",)),
        name="mla_lat" if latent else "mla_ctx",
    )(*args)


def _out_proj_kernel(ya_ref, yb_ref, yc_ref, x_ref, mod_ref, w_ref, g_ref, x1_ref, h2_ref):
    wa = SSM_CH
    wb = wa + GQA_HEADS * GQA_HEAD_DIM
    piece = x_ref.shape[0] // ROW_PIECES
    pieces = [slice(r * piece, (r + 1) * piece) for r in range(ROW_PIECES)]
    outs = [(_dot(ya_ref[rs, :], w_ref[0:wa, :]) + _dot(yb_ref[rs, :], w_ref[wa:wb, :])
             + _dot(yc_ref[rs, :], w_ref[wb:, :])) for rs in pieces]
    for rs, o in zip(pieces, outs):
        x1 = x_ref[rs, :] + mod_ref[2:3, :] * o
        x1_ref[rs, :] = x1
        ms = jnp.mean(x1 * x1, axis=-1, keepdims=True)
        y = x1 * lax.rsqrt(ms + EPS) * g_ref[...]
        h2_ref[rs, :] = (y * (1.0 + mod_ref[4:5, :]) + mod_ref[3:4, :]).astype(h2_ref.dtype)


def _out_proj(ya, yb, yc, x2d, mod, lw, latent, tm, layer):
    rows = x2d.shape[0]
    row = lambda i: (i, 0)
    return pl.pallas_call(
        _out_proj_kernel,
        out_shape=(jax.ShapeDtypeStruct((rows, D_MODEL), F32), jax.ShapeDtypeStruct((rows, D_MODEL), BF16)),
        grid=(rows // tm,),
        in_specs=[
            pl.BlockSpec((tm, SSM_CH), row),
            pl.BlockSpec((tm, GQA_HEADS * GQA_HEAD_DIM), row),
            pl.BlockSpec((tm, MLA_HEADS * MLA_V), row),
            pl.BlockSpec((tm, D_MODEL), row),
            _mod_spec(latent, tm, layer),
            _lspec((D_MODEL, D_MODEL), layer),
            _lspec((1, D_MODEL), layer),
        ],
        out_specs=(pl.BlockSpec((tm, D_MODEL), row), pl.BlockSpec((tm, D_MODEL), row)),
        compiler_params=_cparams(("parallel",)),
        name="out_proj",
    )(ya, yb, yc, x2d, mod, lw['w_out'], lw['norm_mlp'])


def _mlp_kernel(cast_next, *refs):
    if cast_next:
        h2_ref, x1_ref, mod_ref, w1_ref, w2_ref, n1_ref, n2_ref, o_ref, c1_ref, c2_ref = refs
        c1_ref[...] = n1_ref[...].astype(BF16)
        c2_ref[...] = n2_ref[...].astype(BF16)
    else:
        h2_ref, x1_ref, mod_ref, w1_ref, w2_ref, o_ref = refs
    j = pl.program_id(1)

    @pl.when(j == 0)
    def _():
        o_ref[...] = jnp.zeros_like(o_ref)

    f = jnp.maximum(_dot(h2_ref[...], w1_ref[...]), 0.0)
    o_ref[...] += _dot((f * f).astype(BF16), w2_ref[...])

    @pl.when(j == pl.num_programs(1) - 1)
    def _():
        o_ref[...] = x1_ref[...] + mod_ref[5:6, :] * o_ref[...]


def _mlp(h2, x1, mod, w_ff, latent, tm, tf, layer, next_w=None):
    rows = h2.shape[0]
    ni, nj = rows // tm, D_FF // tf
    in_specs = [
        pl.BlockSpec((tm, D_MODEL), lambda i, j: (i, 0)),
        pl.BlockSpec((tm, D_MODEL), lambda i, j: (i, 0), pipeline_mode=pl.Buffered(1)),
        _mod_spec(latent, tm, layer),
        pl.BlockSpec((D_MODEL, tf), lambda i, j: (0, j)),
        pl.BlockSpec((tf, D_MODEL), lambda i, j: (j, 0)),
    ]
    out_shape = [jax.ShapeDtypeStruct((rows, D_MODEL), F32)]
    out_specs = [pl.BlockSpec((tm, D_MODEL), lambda i, j: (i, 0))]
    args = [h2, x1, mod, w_ff[0], w_ff[1]]
    if next_w is not None:
        r1, r2 = D_MODEL // (ni * nj), D_FF // (ni * nj)
        step = lambda i, j: i * nj + j
        in_specs += [pl.BlockSpec((None, r1, D_FF), lambda i, j: (layer + 1, step(i, j), 0)),
                     pl.BlockSpec((None, r2, D_MODEL), lambda i, j: (layer + 1, step(i, j), 0))]
        out_shape += [jax.ShapeDtypeStruct((D_MODEL, D_FF), BF16), jax.ShapeDtypeStruct((D_FF, D_MODEL), BF16)]
        out_specs += [pl.BlockSpec((r1, D_FF), lambda i, j: (step(i, j), 0)),
                      pl.BlockSpec((r2, D_MODEL), lambda i, j: (step(i, j), 0))]
        args += list(next_w)
    outs = pl.pallas_call(
        functools.partial(_mlp_kernel, next_w is not None),
        out_shape=tuple(out_shape),
        grid=(ni, nj),
        in_specs=in_specs,
        out_specs=tuple(out_specs),
        compiler_params=_cparams(("arbitrary", "arbitrary")),
        name="mlp_cast" if next_w is not None else "mlp",
    )(*args)
    return outs[0], (tuple(outs[1:]) if next_w is not None else None)


def _permute_w_in(w):
    base = OFF_QCN
    ckv0 = base + MLA_HEADS * MLA_QK
    qc = w[:, :, base:ckv0].reshape(DEPTH, D_MODEL, MLA_HEADS, MLA_QK)
    nope = qc[..., :MLA_NOPE].reshape(DEPTH, D_MODEL, MLA_HEADS * MLA_NOPE)
    rope = qc[..., MLA_NOPE:].reshape(DEPTH, D_MODEL, MLA_HEADS * MLA_ROPE)
    kr = w[:, :, ckv0 + MLA_KV_RANK:]
    parts = [w[:, :, :base], nope, rope, w[:, :, ckv0:ckv0 + MLA_KV_RANK], kr, kr]
    return jnp.concatenate(parts, axis=2).astype(BF16)


def _rope_tables(seq):
    t = jnp.arange(seq)
    row = (t // GRID_W).astype(F32)
    col = (t % GRID_W).astype(F32)

    def table(d):
        quarter = d // 4
        inv = ROPE_BASE ** (-(jnp.arange(quarter, dtype=F32) / quarter))
        ar = row[:, None] * inv[None, :]
        ac = col[:, None] * inv[None, :]
        cos = jnp.concatenate([jnp.cos(ar), jnp.cos(ar), jnp.cos(ac), jnp.cos(ac)], axis=-1)
        sin = jnp.concatenate([-jnp.sin(ar), jnp.sin(ar), -jnp.sin(ac), jnp.sin(ac)], axis=-1)
        reps = LANES // d
        return jnp.tile(cos, (1, reps)), jnp.tile(sin, (1, reps))

    cos_b, sin_b = table(GQA_HEAD_DIM)
    cos_c, sin_c = table(MLA_ROPE)
    return {'cos_b': cos_b, 'sin_b': sin_b, 'cos_c': cos_c, 'sin_c': sin_c}


def _ssm_params(lam_re, lam_im, log_dt, b_re, b_im, c_re, c_im):
    a = lam_re.astype(F32)
    w = lam_im.astype(F32)
    dt = jnp.exp(log_dt.astype(F32))[..., None]
    mag = jnp.exp(a * dt)
    lbr = mag * jnp.cos(w * dt)
    lbi = mag * jnp.sin(w * dt)
    den = a * a + w * w
    cr = (((lbr - 1.0) * a + lbi * w) / den)[..., None]
    ci = ((lbi * a - (lbr - 1.0) * w) / den)[..., None]
    bre = b_re.astype(F32)
    bim = b_im.astype(F32)
    bb_re = cr * bre - ci * bim
    bb_im = cr * bim + ci * bre
    lam_ri = jnp.stack([lbr.reshape(DEPTH, 2, SSM_N), lbi.reshape(DEPTH, 2, SSM_N)], axis=2)
    lam_b = jnp.broadcast_to(lam_ri[:, :, :, None, :], (DEPTH, 2, 2, SUBLANES, SSM_N))
    gh = SSM_GROUPS // 2
    eye = jnp.eye(gh, dtype=F32)[:, None, :, None]

    def blockdiag(x, rows_per_g, cols_per_g):
        x = x.reshape(DEPTH, 2, 2, gh, rows_per_g, 1, cols_per_g) * eye
        return x.reshape(DEPTH, 2, 2, gh * rows_per_g, gh * cols_per_g)

    def bmat(x):
        return blockdiag(jnp.swapaxes(x, -1, -2), SSM_GROUP, SSM_STATE)

    def cmat(x):
        return blockdiag(jnp.swapaxes(x, -1, -2), SSM_STATE, SSM_GROUP)

    b_cat = jnp.concatenate([bmat(bb_re), bmat(bb_im)], axis=-1).astype(BF16)
    return {'lam': lam_b, 'b': b_cat, 'c_re': cmat(c_re.astype(F32)).astype(BF16),
            'c_im': cmat(c_im.astype(F32)).astype(BF16)}


def _trunk_layer(x2d, mod, lw, sp, tabs, latent, ctx, layer, w_ff, next_w=None):
    seq = DEC_SEQ if latent else SEQ
    bsz = x2d.shape[0] // seq
    u, qb, kb, vb, qc, ckv, kr, krg = _in_proj(x2d, mod, latent, lw, tabs, TM_PROJ, layer)

    u3 = u.reshape(bsz, seq, SSM_CH)
    ya, h_t = _ssm_scan(u3, ctx['h0'], sp, lw['ssm_d'], lw['ssm_w_glu'], layer)
    ya = ya.reshape(bsz * seq, SSM_CH)

    if latent:
        yb = _gqa_attn(qb, kb, vb, True, ctx['k'], ctx['v'], layer)
        yc = _mla_attn(qc, ckv, kr, krg, lw, True, ctx['ckv'], ctx['kr'], layer)
    else:
        yb = _gqa_attn(qb, kb, vb, False)
        yc = _mla_attn(qc, ckv, kr, krg, lw, False, layer=layer)

    x1, h2 = _out_proj(ya, yb, yc, x2d, mod, lw, latent, TM_PROJ, layer)
    x2, w_ff_next = _mlp(h2, x1, mod, w_ff, latent, TM_MLP, TF_MLP, layer, next_w)
    return x2, (kb, vb, ckv, kr, h_t), w_ff_next


def kernel(x_prompt, x_sample, cache_attn_k, cache_attn_v, cache_mla_ckv, cache_mla_krope, state_ssm, c, c_ctx, w_mod, b_mod, norm_mix, norm_mlp, w_in, gqa_q_norm, gqa_k_norm, mla_kv_norm, mla_q_norm, mla_k_norm, mla_w_uk, mla_w_uv, ssm_lam_re, ssm_lam_im, ssm_log_dt, ssm_b_re, ssm_b_im, ssm_c_re, ssm_c_im, ssm_d, ssm_w_glu, w_out, w_ff1, w_ff2):
    cvec = jnp.zeros((N_MOD, D_MODEL), F32).at[0].set(c_ctx).at[1:1 + DEC_BATCH].set(c)
    mod = _adaln(cvec, w_mod, b_mod.reshape(DEPTH, 1, 6 * D_MODEL)).reshape(DEPTH, N_MOD, 6, D_MODEL)

    tabs = _rope_tables(DEC_SEQ)
    cache_k = cache_attn_k.reshape(DEC_BATCH, DEPTH, PAST_LEN, GQA_KV_HEADS * GQA_HEAD_DIM)
    cache_v = cache_attn_v.reshape(DEC_BATCH, DEPTH, PAST_LEN, GQA_KV_HEADS * GQA_HEAD_DIM)

    xp = x_prompt.reshape(BATCH * SEQ, D_MODEL)
    xs = x_sample.reshape(DEC_BATCH * DEC_SEQ, D_MODEL)
    new_k, new_v, new_ckv, new_kr, new_ssm = [], [], [], [], []
    h0_zero = jnp.zeros((BATCH // SUBLANES, 2, 2, SUBLANES, SSM_N), F32)

    row3 = lambda a: a.reshape(DEPTH, 1, -1)
    dup = lambda a: jnp.concatenate([a, a], axis=-1)
    lw = {
        'norm_mix': row3(norm_mix), 'norm_mlp': row3(norm_mlp),
        'w_in': _permute_w_in(w_in),
        'w_out': w_out.astype(BF16),
        'gqa_q_norm': row3(gqa_q_norm), 'gqa_k_norm': row3(gqa_k_norm), 'mla_kv_norm': row3(mla_kv_norm),
        'mla_q_nope_g': row3(mla_q_norm[:, :MLA_NOPE]),
        'mla_q_rope_g': row3(dup(mla_q_norm[:, MLA_NOPE:])),
        'mla_k_nope_g': row3(mla_k_norm[:, :MLA_NOPE]),
        'mla_k_rope_g': row3(dup(mla_k_norm[:, MLA_NOPE:])),
        'mla_k_rope_g64': row3(mla_k_norm[:, MLA_NOPE:]),
        'mla_w_uk': mla_w_uk.reshape(DEPTH, MLA_KV_RANK, MLA_HEADS * MLA_NOPE).astype(BF16),
        'mla_w_uv': mla_w_uv.reshape(DEPTH, MLA_KV_RANK, MLA_HEADS * MLA_V).astype(BF16),
        'ssm_d': row3(ssm_d), 'ssm_w_glu': ssm_w_glu.astype(BF16),
    }
    sp = _ssm_params(ssm_lam_re, ssm_lam_im, ssm_log_dt, ssm_b_re, ssm_b_im, ssm_c_re, ssm_c_im)
    h0_lat = state_ssm.reshape(DEC_BATCH, DEPTH, 2, SSM_N, 2).transpose(1, 2, 4, 0, 3)[:, None]

    w_ff = (w_ff1[0].astype(BF16), w_ff2[0].astype(BF16))
    for l in range(DEPTH):
        next_w = (w_ff1, w_ff2) if l + 1 < DEPTH else None
        xp, (k, v, ckv_n, kr, h_t), w_ff_next = _trunk_layer(xp, mod, lw, sp, tabs, False, {'h0': h0_zero}, l,
                                                            w_ff, next_w)
        new_k.append(k.reshape(BATCH, SEQ, GQA_KV_HEADS, GQA_HEAD_DIM))
        new_v.append(v.reshape(BATCH, SEQ, GQA_KV_HEADS, GQA_HEAD_DIM))
        new_ckv.append(ckv_n.reshape(BATCH, SEQ, MLA_KV_RANK))
        new_kr.append(kr.reshape(BATCH, SEQ, MLA_ROPE))
        hs = h_t.transpose(0, 3, 1, 4, 2).reshape(BATCH, 2, SSM_GROUPS, SSM_STATE, 2)
        new_ssm.append(hs)

        ctx = {'k': cache_k, 'v': cache_v, 'ckv': cache_mla_ckv, 'kr': cache_mla_krope, 'h0': h0_lat[l]}
        xs, _, _ = _trunk_layer(xs, mod, lw, sp, tabs, True, ctx, l, w_ff)
        w_ff = w_ff_next

    return (xp.reshape(BATCH, SEQ, D_MODEL), xs.reshape(DEC_BATCH, DEC_SEQ, D_MODEL),
            jnp.stack(new_k, axis=1), jnp.stack(new_v, axis=1), jnp.stack(new_ckv, axis=1),
            jnp.stack(new_kr, axis=1), jnp.stack(new_ssm, axis=1))
```

```python
import functools
import math

import jax
import jax.numpy as jnp
from jax import lax
from jax.experimental import pallas as pl
from jax.experimental.pallas import tpu as pltpu

D_MODEL = 2048
BATCH = 32
SEQ = 256
DEPTH = 2
DEC_BATCH = 8
DEC_SEQ = 1024
PAST_LEN = 512
GRID_W = 64
ROPE_BASE = 10000.0
EPS = 1e-6
SSM_CH = 512
SSM_GROUP = 16
SSM_GROUPS = SSM_CH // SSM_GROUP
SSM_STATE = 64
SSM_N = SSM_GROUPS * SSM_STATE
GQA_HEADS = 6
GQA_KV_HEADS = 2
GQA_GROUP = GQA_HEADS // GQA_KV_HEADS
GQA_HEAD_DIM = 128
MLA_HEADS = 6
MLA_NOPE = 128
MLA_ROPE = 64
MLA_QK = MLA_NOPE + MLA_ROPE
MLA_QK_PAD = 256
MLA_V = 128
MLA_KV_RANK = 512
D_FF = 4 * D_MODEL
N_MOD = 16

OFF_U = 0
OFF_QB = OFF_U + SSM_CH
OFF_KB = OFF_QB + GQA_HEADS * GQA_HEAD_DIM
OFF_VB = OFF_KB + GQA_KV_HEADS * GQA_HEAD_DIM
OFF_QCN = OFF_VB + GQA_KV_HEADS * GQA_HEAD_DIM
OFF_QCR = OFF_QCN + MLA_HEADS * MLA_NOPE
OFF_CKV = OFF_QCR + MLA_HEADS * MLA_ROPE
OFF_KR = OFF_CKV + MLA_KV_RANK
IN_WIDTH_P = OFF_KR + 2 * MLA_ROPE

LANES = 128
SUBLANES = 8
VMEM_LIMIT = 56 * 1024 * 1024

TM_PROJ = 512
TM_MLP = 1024
TF_MLP = 512
ROW_PIECES = 2
IN_PROJ_PIECES = 2

BF16 = jnp.bfloat16
F32 = jnp.float32


def _cparams(sem):
    return pltpu.CompilerParams(dimension_semantics=sem, vmem_limit_bytes=VMEM_LIMIT)


def _dot(a, b):
    return jnp.dot(a, b, preferred_element_type=F32)


def _dot_nt(a, b):
    return lax.dot_general(a, b, (((1,), (1,)), ((), ())), preferred_element_type=F32)


def _adaln_kernel(c_ref, w_ref, b_ref, o_ref):
    c = c_ref[...]
    s = (c * jax.nn.sigmoid(c)).astype(BF16)
    o_ref[...] = _dot(s, w_ref[...].astype(BF16)) + b_ref[...]


def _adaln(cvec, w_mod, b_mod):
    tn = 1024
    return pl.pallas_call(
        _adaln_kernel,
        out_shape=jax.ShapeDtypeStruct((DEPTH, N_MOD, 6 * D_MODEL), F32),
        grid=(DEPTH, 6 * D_MODEL // tn),
        in_specs=[
            pl.BlockSpec((N_MOD, D_MODEL), lambda l, j: (0, 0)),
            pl.BlockSpec((None, D_MODEL, tn), lambda l, j: (l, 0, j)),
            pl.BlockSpec((None, 1, tn), lambda l, j: (l, 0, j)),
        ],
        out_specs=pl.BlockSpec((None, N_MOD, tn), lambda l, j: (l, 0, j)),
        compiler_params=_cparams(("parallel", "parallel")),
        name="adaln",
    )(cvec, w_mod, b_mod)


def _swap_halves(x, block):
    lane = lax.broadcasted_iota(jnp.int32, x.shape, 1)
    first = (lane % (2 * block)) < block
    return jnp.where(first, pltpu.roll(x, LANES - block, 1), pltpu.roll(x, block, 1))


def _rope(x, cos, sin_signed, block):
    return x * cos + _swap_halves(x, block) * sin_signed


def _in_proj_kernel(latent, x_ref, mod_ref, gmix_ref, w_ref, gq_ref, gk_ref, gkv_ref, gqn_ref, gqr_ref, gkr_ref,
                    cosb_ref, sinb_ref, cosc_ref, sinc_ref,
                    u_ref, qb_ref, kb_ref, vb_ref, qc_ref, ckv_ref, kr_ref, krg_ref):
    piece = x_ref.shape[0] // IN_PROJ_PIECES
    for r in range(IN_PROJ_PIECES):
        rs = pl.ds(r * piece, piece)
        _in_proj_rows(latent, x_ref.at[rs], mod_ref, gmix_ref, w_ref, gq_ref, gk_ref, gkv_ref, gqn_ref, gqr_ref,
                      gkr_ref, cosb_ref.at[rs], sinb_ref.at[rs], cosc_ref.at[rs], sinc_ref.at[rs],
                      u_ref.at[rs], qb_ref.at[rs], kb_ref.at[rs], vb_ref.at[rs], qc_ref.at[:, rs], ckv_ref.at[rs],
                      kr_ref.at[rs], krg_ref.at[rs])


def _in_proj_rows(latent, x_ref, mod_ref, gmix_ref, w_ref, gq_ref, gk_ref, gkv_ref, gqn_ref, gqr_ref, gkr_ref,
                  cosb_ref, sinb_ref, cosc_ref, sinc_ref,
                  u_ref, qb_ref, kb_ref, vb_ref, qc_ref, ckv_ref, kr_ref, krg_ref):
    x = x_ref[...]
    ms = jnp.mean(x * x, axis=-1, keepdims=True)
    y = x * lax.rsqrt(ms + EPS) * gmix_ref[...]
    h = (y * (1.0 + mod_ref[1:2, :]) + mod_ref[0:1, :]).astype(BF16)

    def proj(off, width):
        return _dot(h, w_ref[:, off:off + width])

    scale_b = 1.0 / math.sqrt(GQA_HEAD_DIM)
    zq = proj(OFF_QB, GQA_HEADS * GQA_HEAD_DIM)
    for hd in range(GQA_HEADS):
        col = zq[:, hd * LANES:(hd + 1) * LANES]
        q = col * lax.rsqrt(jnp.mean(col * col, axis=-1, keepdims=True) + EPS) * gq_ref[...]
        if latent:
            q = _rope(q, cosb_ref[...], sinb_ref[...], GQA_HEAD_DIM // 4)
        qb_ref[:, hd * LANES:(hd + 1) * LANES] = (q * scale_b).astype(qb_ref.dtype)
    zk = proj(OFF_KB, GQA_KV_HEADS * GQA_HEAD_DIM)
    for hd in range(GQA_KV_HEADS):
        col = zk[:, hd * LANES:(hd + 1) * LANES]
        k = col * lax.rsqrt(jnp.mean(col * col, axis=-1, keepdims=True) + EPS) * gk_ref[...]
        if latent:
            k = _rope(k, cosb_ref[...], sinb_ref[...], GQA_HEAD_DIM // 4)
        kb_ref[:, hd * LANES:(hd + 1) * LANES] = k.astype(kb_ref.dtype)

    scale_c = 1.0 / math.sqrt(MLA_QK)
    zn = proj(OFF_QCN, MLA_HEADS * MLA_NOPE)
    zr = proj(OFF_QCR, MLA_HEADS * MLA_ROPE)
    lane = lax.broadcasted_iota(jnp.int32, (x.shape[0], LANES), 1)
    low = lane < MLA_ROPE
    for pair in range(MLA_HEADS // 2):
        colr = zr[:, pair * LANES:(pair + 1) * LANES]
        sq = colr * colr
        ss_lo = jnp.sum(jnp.where(low, sq, 0.0), axis=-1, keepdims=True)
        ss_hi = jnp.sum(jnp.where(low, 0.0, sq), axis=-1, keepdims=True)
        rs = []
        for half, ss_r in ((0, ss_lo), (1, ss_hi)):
            hd = 2 * pair + half
            coln = zn[:, hd * LANES:(hd + 1) * LANES]
            ss = jnp.sum(coln * coln, axis=-1, keepdims=True) + ss_r
            r = lax.rsqrt(ss * (1.0 / MLA_QK) + EPS)
            rs.append(r)
            qc_ref[hd, :, :MLA_NOPE] = (coln * r * gqn_ref[...] * scale_c).astype(qc_ref.dtype)
        qr = colr * jnp.where(low, rs[0], rs[1]) * gqr_ref[...]
        if latent:
            qr = _rope(qr, cosc_ref[...], sinc_ref[...], MLA_ROPE // 4)
        qr = qr * scale_c
        zeros = jnp.zeros((x.shape[0], MLA_QK_PAD - MLA_QK), qc_ref.dtype)
        for half in range(2):
            hd = 2 * pair + half
            qc_ref[hd, :, MLA_NOPE:MLA_QK] = qr[:, half * MLA_ROPE:(half + 1) * MLA_ROPE].astype(qc_ref.dtype)
            qc_ref[hd, :, MLA_QK:] = zeros

    zc = proj(OFF_CKV, MLA_KV_RANK)
    ckv_ref[...] = (zc * lax.rsqrt(jnp.mean(zc * zc, axis=-1, keepdims=True) + EPS) * gkv_ref[...]).astype(ckv_ref.dtype)
    zkr = proj(OFF_KR, 2 * MLA_ROPE)
    kr_ref[...] = zkr[:, :MLA_ROPE]
    krg = zkr * gkr_ref[...]
    if latent:
        krg = _rope(krg, cosc_ref[...], sinc_ref[...], MLA_ROPE // 4)
    krg_ref[...] = krg[:, :MLA_ROPE]

    vb_ref[...] = proj(OFF_VB, GQA_KV_HEADS * GQA_HEAD_DIM).astype(vb_ref.dtype)
    u_ref[...] = proj(OFF_U, SSM_CH)


def _lspec(block_tail, layer):
    zeros = (0,) * len(block_tail)
    return pl.BlockSpec((None,) + tuple(block_tail), lambda *_: (layer,) + zeros)


def _mod_spec(latent, tm, layer):
    tiles_per_seq = DEC_SEQ // tm
    if latent:
        return pl.BlockSpec((None, None, 6, D_MODEL), lambda i, *_: (layer, 1 + i // tiles_per_seq, 0, 0))
    return pl.BlockSpec((None, None, 6, D_MODEL), lambda i, *_: (layer, 0, 0, 0))


def _in_proj(x2d, mod, latent, lw, w_in_l, tabs, tm, layer):
    rows = x2d.shape[0]
    n_tiles = rows // tm
    tiles_per_seq = DEC_SEQ // tm
    if latent:
        tab_map = lambda i: (i % tiles_per_seq, 0)
    else:
        tab_map = lambda i: (0, 0)
    row = lambda i: (i, 0)
    act_dt = BF16 if latent else F32
    in_specs = [
        pl.BlockSpec((tm, D_MODEL), row),
        _mod_spec(latent, tm, layer),
        _lspec((1, D_MODEL), layer),
        pl.BlockSpec((D_MODEL, IN_WIDTH_P), lambda i: (0, 0)),
        _lspec((1, LANES), layer),
        _lspec((1, LANES), layer),
        _lspec((1, MLA_KV_RANK), layer),
        _lspec((1, LANES), layer),
        _lspec((1, LANES), layer),
        _lspec((1, LANES), layer),
        pl.BlockSpec((tm, LANES), tab_map),
        pl.BlockSpec((tm, LANES), tab_map),
        pl.BlockSpec((tm, LANES), tab_map),
        pl.BlockSpec((tm, LANES), tab_map),
    ]
    out_shape = (
        jax.ShapeDtypeStruct((rows, SSM_CH), F32),
        jax.ShapeDtypeStruct((rows, GQA_HEADS * GQA_HEAD_DIM), BF16),
        jax.ShapeDtypeStruct((rows, GQA_KV_HEADS * GQA_HEAD_DIM), act_dt),
        jax.ShapeDtypeStruct((rows, GQA_KV_HEADS * GQA_HEAD_DIM), act_dt),
        jax.ShapeDtypeStruct((MLA_HEADS, rows, MLA_QK_PAD), BF16),
        jax.ShapeDtypeStruct((rows, MLA_KV_RANK), act_dt),
        jax.ShapeDtypeStruct((rows, MLA_ROPE), F32),
        jax.ShapeDtypeStruct((rows, MLA_ROPE), F32),
    )
    out_specs = (
        pl.BlockSpec((tm, SSM_CH), row),
        pl.BlockSpec((tm, GQA_HEADS * GQA_HEAD_DIM), row),
        pl.BlockSpec((tm, GQA_KV_HEADS * GQA_HEAD_DIM), row),
        pl.BlockSpec((tm, GQA_KV_HEADS * GQA_HEAD_DIM), row),
        pl.BlockSpec((MLA_HEADS, tm, MLA_QK_PAD), lambda i: (0, i, 0)),
        pl.BlockSpec((tm, MLA_KV_RANK), row),
        pl.BlockSpec((tm, MLA_ROPE), row),
        pl.BlockSpec((tm, MLA_ROPE), row),
    )
    return pl.pallas_call(
        functools.partial(_in_proj_kernel, latent),
        out_shape=out_shape,
        grid=(n_tiles,),
        in_specs=in_specs,
        out_specs=out_specs,
        compiler_params=_cparams(("parallel",)),
        name="in_proj_lat" if latent else "in_proj_ctx",
    )(x2d, mod, lw['norm_mix'], w_in_l, lw['gqa_q_norm'], lw['gqa_k_norm'], lw['mla_kv_norm'],
      lw['mla_q_nope_g'], lw['mla_q_rope_g'], lw['mla_k_rope_g'],
      tabs['cos_b'], tabs['sin_b'], tabs['cos_c'], tabs['sin_c'])


SSM_TC = 128
SSM_HALVES = 2


def _ssm_scan_kernel(n_chunks, backward, *refs):
    if backward:
        (u_ref, h0_ref, lam_ref, b_ref, cre_ref, cim_ref, yf_ref, dsk_ref, wglu_ref, o_ref, hT_ref,
         utm_ref, sre0_ref, sim0_ref, sre1_ref, sim1_ref, h0s_ref, h1s_ref, y_ref, ycat_ref) = refs
    else:
        (u_ref, h0_ref, lam_ref, b_ref, cre_ref, cim_ref, y_ref, hT_ref,
         utm_ref, sre0_ref, sim0_ref, sre1_ref, sim1_ref, h0s_ref, h1s_ref) = refs
    i = pl.program_id(1)
    tc = SSM_TC
    half_n = SSM_N // SSM_HALVES
    halves = ((sre0_ref, sim0_ref, h0s_ref), (sre1_ref, sim1_ref, h1s_ref))

    @pl.when(i == 0)
    def _():
        for k, (_, _, hk_ref) in enumerate(halves):
            hk_ref[...] = h0_ref[:, :, k * half_n:(k + 1) * half_n]

    for b in range(SUBLANES):
        for j in range(SSM_CH // LANES):
            utm_ref[j, pl.ds(b, tc, stride=SUBLANES), :] = u_ref[b, :, j * LANES:(j + 1) * LANES]

    for k, (sre_ref, sim_ref, _) in enumerate(halves):
        uk = jnp.concatenate([utm_ref[2 * k], utm_ref[2 * k + 1]], axis=-1).astype(BF16)
        r = _dot(uk, b_ref[k])
        sre_ref[...] = r[:, :half_n]
        sim_ref[...] = r[:, half_n:]

    for k, (sre_ref, sim_ref, hk_ref) in enumerate(halves):
        sl = slice(k * half_n, (k + 1) * half_n)
        lr = lam_ref[0, :, sl]
        li = lam_ref[1, :, sl]
        hr = hk_ref[0]
        hi = hk_ref[1]
        for s in range(tc):
            t = (tc - 1 - s) if backward else s
            rows = slice(t * SUBLANES, (t + 1) * SUBLANES)
            hr, hi = (lr * hr - li * hi + sre_ref[rows, :], lr * hi + li * hr + sim_ref[rows, :])
            sre_ref[rows, :] = hr
            sim_ref[rows, :] = hi
        hk_ref[0] = hr
        hk_ref[1] = hi

    for k, (sre_ref, sim_ref, _) in enumerate(halves):
        yk = _dot(sre_ref[...].astype(BF16), cre_ref[k]) - _dot(sim_ref[...].astype(BF16), cim_ref[k])
        y_ref[2 * k] = yk[:, :LANES]
        y_ref[2 * k + 1] = yk[:, LANES:]

    if backward:
        for b in range(SUBLANES):
            for j in range(SSM_CH // LANES):
                sl = slice(j * LANES, (j + 1) * LANES)
                rows_b = pl.ds(b, tc, stride=SUBLANES)
                ycat_ref[b * tc:(b + 1) * tc, sl] = (yf_ref.at[j][rows_b, :] + y_ref.at[j][rows_b, :]
                                                    + dsk_ref[:, sl] * u_ref[b, :, sl])
        zg = _dot(ycat_ref[...].astype(BF16), wglu_ref[...])
        out = zg[:, :SSM_CH] * jax.nn.sigmoid(zg[:, SSM_CH:])
        o_ref[...] = out.reshape(SUBLANES, tc, SSM_CH).astype(o_ref.dtype)

    @pl.when(i == n_chunks - 1)
    def _():
        for k, (_, _, hk_ref) in enumerate(halves):
            hT_ref[:, :, k * half_n:(k + 1) * half_n] = hk_ref[...]


def _ssm_scan(u3, h0, sp, d_skip, w_glu, layer):
    bsz, t, _ = u3.shape
    n_groups = bsz // SUBLANES
    n_chunks = t // SSM_TC
    tc = SSM_TC
    y_shape = jax.ShapeDtypeStruct((n_groups, SSM_CH // LANES, t * SUBLANES, LANES), F32)
    h_shape = jax.ShapeDtypeStruct((n_groups, 2, SUBLANES, SSM_N), F32)
    h_spec = pl.BlockSpec((None, 2, SUBLANES, SSM_N), lambda g, i: (g, 0, 0, 0))

    def one_direction(d, extra_args, extra_specs, out_shape, out_spec, extra_scratch):
        chunk = (lambda i: n_chunks - 1 - i) if d else (lambda i: i)
        y_like = lambda g, i: (g, 0, chunk(i), 0)
        return pl.pallas_call(
            functools.partial(_ssm_scan_kernel, n_chunks, d),
            out_shape=(out_shape, h_shape),
            grid=(n_groups, n_chunks),
            in_specs=[
                pl.BlockSpec((SUBLANES, tc, SSM_CH), lambda g, i: (g, chunk(i), 0)),
                pl.BlockSpec((None, None, 2, SUBLANES, SSM_N), lambda g, i: (g, d, 0, 0, 0)),
                pl.BlockSpec((None, None, 2, SUBLANES, SSM_N), lambda g, i: (layer, d, 0, 0, 0)),
                pl.BlockSpec((None, None, 2, SSM_CH // 2, SSM_N), lambda g, i: (layer, d, 0, 0, 0)),
                pl.BlockSpec((None, None, 2, SSM_N // 2, SSM_CH // 2), lambda g, i: (layer, d, 0, 0, 0)),
                pl.BlockSpec((None, None, 2, SSM_N // 2, SSM_CH // 2), lambda g, i: (layer, d, 0, 0, 0)),
            ] + extra_specs(y_like),
            out_specs=(out_spec(chunk), h_spec),
            scratch_shapes=_ssm_scratch(tc) + extra_scratch,
            compiler_params=_cparams(("parallel", "arbitrary")),
            name="ssm_scan_bwd" if d else "ssm_scan_fwd",
        )(u3, h0, sp['lam'], sp['b'], sp['c_re'], sp['c_im'], *extra_args)

    y_block = (None, SSM_CH // LANES, tc * SUBLANES, LANES)
    y_f, h_f = one_direction(0, [], lambda y_like: [], y_shape,
                             lambda chunk: pl.BlockSpec(y_block, lambda g, i: (g, 0, chunk(i), 0)), [])
    ya, h_b = one_direction(
        1, [y_f, d_skip, w_glu],
        lambda y_like: [pl.BlockSpec(y_block, y_like), _lspec((1, SSM_CH), layer),
                        _lspec((SSM_CH, 2 * SSM_CH), layer)],
        jax.ShapeDtypeStruct((bsz, t, SSM_CH), BF16),
        lambda chunk: pl.BlockSpec((SUBLANES, tc, SSM_CH), lambda g, i: (g, chunk(i), 0)),
        [pltpu.VMEM((SSM_CH // LANES, tc * SUBLANES, LANES), F32), pltpu.VMEM((SUBLANES * tc, SSM_CH), F32)])
    return ya, jnp.stack([h_f, h_b], axis=1)


def _ssm_scratch(tc):
    return [
            pltpu.VMEM((SSM_CH // LANES, tc * SUBLANES, LANES), F32),
            pltpu.VMEM((tc * SUBLANES, SSM_N // SSM_HALVES), F32),
            pltpu.VMEM((tc * SUBLANES, SSM_N // SSM_HALVES), F32),
            pltpu.VMEM((tc * SUBLANES, SSM_N // SSM_HALVES), F32),
            pltpu.VMEM((tc * SUBLANES, SSM_N // SSM_HALVES), F32),
            pltpu.VMEM((2, SUBLANES, SSM_N // SSM_HALVES), F32),
            pltpu.VMEM((2, SUBLANES, SSM_N // SSM_HALVES), F32),
        ]


def _ones_column(n):
    return jnp.ones((n, LANES), BF16)


def _softmax_pv(scores, v_ext):
    m = jnp.max(scores, axis=-1, keepdims=True)
    p = jnp.exp((scores - m).astype(BF16))
    o = _dot(p, v_ext)
    return o[:, :LANES] / o[:, LANES:]


ATTN_TQ = 256


def _for_each_q_block(nb, seq, body):
    nq = seq // ATTN_TQ
    for s in range(nb):
        for qi in range(nq):
            body(s, s * seq + qi * ATTN_TQ)


def _gqa_kernel(latent, nb, seq, *refs):
    if latent:
        q_ref, k_ref, v_ref, kc_ref, vc_ref, o_ref, k_s, v_s = refs
    else:
        q_ref, k_ref, v_ref, o_ref, k_s, v_s = refs
    tq = ATTN_TQ
    for s in range(nb):
        for h in range(GQA_KV_HEADS):
            hl = slice(h * LANES, (h + 1) * LANES)
            k_s[s, h, 0:seq, :] = k_ref[s * seq:(s + 1) * seq, hl].astype(BF16)
            v_s[s, h, 0:seq, :LANES] = v_ref[s * seq:(s + 1) * seq, hl].astype(BF16)
            if latent:
                k_s[s, h, seq:, :] = kc_ref[:, hl].astype(BF16)
                v_s[s, h, seq:, :LANES] = vc_ref[:, hl].astype(BF16)
            v_s[s, h, :, LANES:] = _ones_column(v_s.shape[2])

    def body(s, r0):
        for h in range(GQA_KV_HEADS):
            heads = [h * GQA_GROUP + g for g in range(GQA_GROUP)]
            q3 = jnp.concatenate([q_ref[pl.ds(r0, tq), hd * LANES:(hd + 1) * LANES] for hd in heads], axis=0)
            o = _softmax_pv(_dot_nt(q3, k_s[s, h]), v_s[s, h])
            for g, hd in enumerate(heads):
                o_ref[pl.ds(r0, tq), hd * LANES:(hd + 1) * LANES] = o[g * tq:(g + 1) * tq].astype(o_ref.dtype)

    _for_each_q_block(nb, seq, body)


def _gqa_attn(qb, kb, vb, latent, cache_k=None, cache_v=None, layer=0):
    rows = qb.shape[0]
    seq = DEC_SEQ if latent else SEQ
    nb = 1 if latent else 4
    t_all = seq + (PAST_LEN if latent else 0)
    qw = GQA_HEADS * GQA_HEAD_DIM
    kw = GQA_KV_HEADS * GQA_HEAD_DIM
    row = lambda i: (i, 0)
    in_specs = [
        pl.BlockSpec((nb * seq, qw), row),
        pl.BlockSpec((nb * seq, kw), row),
        pl.BlockSpec((nb * seq, kw), row),
    ]
    args = [qb, kb, vb]
    if latent:
        cspec = pl.BlockSpec((None, None, PAST_LEN, kw), lambda i: (i, layer, 0, 0))
        in_specs += [cspec, cspec]
        args += [cache_k, cache_v]
    return pl.pallas_call(
        functools.partial(_gqa_kernel, latent, nb, seq),
        out_shape=jax.ShapeDtypeStruct((rows, qw), BF16),
        grid=(rows // (nb * seq),),
        in_specs=in_specs,
        out_specs=pl.BlockSpec((nb * seq, qw), row),
        scratch_shapes=[
            pltpu.VMEM((nb, GQA_KV_HEADS, t_all, GQA_HEAD_DIM), BF16),
            pltpu.VMEM((nb, GQA_KV_HEADS, t_all, 2 * LANES), BF16),
        ],
        compiler_params=_cparams(("parallel",)),
        name="gqa_lat" if latent else "gqa_ctx",
    )(*args)


def _mla_kernel(latent, nb, seq, *refs):
    if latent:
        (q_ref, ckv_ref, kr_ref, krg_ref, ckvc_ref, krc_ref, wuk_ref, wuv_ref, gn_ref, gr_ref,
         o_ref, k_s, v_s) = refs
    else:
        (q_ref, ckv_ref, kr_ref, krg_ref, wuk_ref, wuv_ref, gn_ref, gr_ref, o_ref, k_s, v_s) = refs
    tq = ATTN_TQ

    def expand(ckv, kr_raw, krg, place):
        c = ckv.astype(BF16)
        kn_all = _dot(c, wuk_ref[...])
        v_all = _dot(c, wuv_ref[...])
        ss_kr = jnp.sum(kr_raw * kr_raw, axis=-1, keepdims=True)
        for h in range(MLA_HEADS):
            hl = slice(h * LANES, (h + 1) * LANES)
            kn = kn_all[:, hl]
            rs = lax.rsqrt((jnp.sum(kn * kn, axis=-1, keepdims=True) + ss_kr) * (1.0 / MLA_QK) + EPS)
            kn = (kn * rs * gn_ref[...]).astype(BF16)
            kp = (krg * rs).astype(BF16)
            vv = v_all[:, hl].astype(BF16)
            for s, lo, n, rows in place:
                k_s[s, h, lo:lo + n, :MLA_NOPE] = kn[rows]
                k_s[s, h, lo:lo + n, MLA_NOPE:MLA_QK] = kp[rows]
                k_s[s, h, lo:lo + n, MLA_QK:] = jnp.zeros((n, MLA_QK_PAD - MLA_QK), BF16)
                v_s[s, h, lo:lo + n, :MLA_V] = vv[rows]
                v_s[s, h, lo:lo + n, MLA_V:] = _ones_column(n)

    expand(ckv_ref[...], kr_ref[...], krg_ref[...],
           [(s, 0, seq, slice(s * seq, (s + 1) * seq)) for s in range(nb)])
    if latent:
        krc = krc_ref[...]
        expand(ckvc_ref[...], krc, krc * gr_ref[...], [(0, seq, PAST_LEN, slice(0, PAST_LEN))])

    def body(s, r0):
        for h in range(MLA_HEADS):
            sc = _dot_nt(q_ref[h, pl.ds(r0, tq), :], k_s[s, h])
            o_ref[pl.ds(r0, tq), h * LANES:(h + 1) * LANES] = _softmax_pv(sc, v_s[s, h]).astype(o_ref.dtype)

    _for_each_q_block(nb, seq, body)


def _mla_attn(qc, ckv, kr, krg, lw, latent, cache_ckv=None, cache_kr=None, layer=0):
    rows = qc.shape[1]
    seq = DEC_SEQ if latent else SEQ
    nb = 1 if latent else 4
    t_all = seq + (PAST_LEN if latent else 0)
    row = lambda i: (i, 0)
    in_specs = [
        pl.BlockSpec((MLA_HEADS, nb * seq, MLA_QK_PAD), lambda i: (0, i, 0)),
        pl.BlockSpec((nb * seq, MLA_KV_RANK), row),
        pl.BlockSpec((nb * seq, MLA_ROPE), row),
        pl.BlockSpec((nb * seq, MLA_ROPE), row),
    ]
    args = [qc, ckv, kr, krg]
    if latent:
        in_specs += [
            pl.BlockSpec((None, None, PAST_LEN, MLA_KV_RANK), lambda i: (i, layer, 0, 0)),
            pl.BlockSpec((None, None, PAST_LEN, MLA_ROPE), lambda i: (i, layer, 0, 0)),
        ]
        args += [cache_ckv, cache_kr]
    in_specs += [
        _lspec((MLA_KV_RANK, MLA_HEADS * MLA_NOPE), layer),
        _lspec((MLA_KV_RANK, MLA_HEADS * MLA_V), layer),
        _lspec((1, MLA_NOPE), layer),
        _lspec((1, MLA_ROPE), layer),
    ]
    args += [lw['mla_w_uk'], lw['mla_w_uv'], lw['mla_k_nope_g'], lw['mla_k_rope_g64']]
    return pl.pallas_call(
        functools.partial(_mla_kernel, latent, nb, seq),
        out_shape=jax.ShapeDtypeStruct((rows, MLA_HEADS * MLA_V), BF16),
        grid=(rows // (nb * seq),),
        in_specs=in_specs,
        out_specs=pl.BlockSpec((nb * seq, MLA_HEADS * MLA_V), row),
        scratch_shapes=[
            pltpu.VMEM((nb, MLA_HEADS, t_all, MLA_QK_PAD), BF16),
            pltpu.VMEM((nb, MLA_HEADS, t_all, MLA_V + LANES), BF16),
        ],
        compiler_params=_cparams(("parallel",)),
        name="mla_lat" if latent else "mla_ctx",
    )(*args)


def _out_proj_kernel(ya_ref, yb_ref, yc_ref, x_ref, mod_ref, w_ref, g_ref, x1_ref, h2_ref):
    wa = SSM_CH
    wb = wa + GQA_HEADS * GQA_HEAD_DIM
    piece = x_ref.shape[0] // ROW_PIECES
    pieces = [slice(r * piece, (r + 1) * piece) for r in range(ROW_PIECES)]
    outs = [(_dot(ya_ref[rs, :], w_ref[0:wa, :]) + _dot(yb_ref[rs, :], w_ref[wa:wb, :])
             + _dot(yc_ref[rs, :], w_ref[wb:, :])) for rs in pieces]
    for rs, o in zip(pieces, outs):
        x1 = x_ref[rs, :] + mod_ref[2:3, :] * o
        x1_ref[rs, :] = x1
        ms = jnp.mean(x1 * x1, axis=-1, keepdims=True)
        y = x1 * lax.rsqrt(ms + EPS) * g_ref[...]
        h2_ref[rs, :] = (y * (1.0 + mod_ref[4:5, :]) + mod_ref[3:4, :]).astype(h2_ref.dtype)


def _out_proj(ya, yb, yc, x2d, mod, lw, w_out_l, latent, tm, layer):
    rows = x2d.shape[0]
    row = lambda i: (i, 0)
    return pl.pallas_call(
        _out_proj_kernel,
        out_shape=(jax.ShapeDtypeStruct((rows, D_MODEL), F32), jax.ShapeDtypeStruct((rows, D_MODEL), BF16)),
        grid=(rows // tm,),
        in_specs=[
            pl.BlockSpec((tm, SSM_CH), row),
            pl.BlockSpec((tm, GQA_HEADS * GQA_HEAD_DIM), row),
            pl.BlockSpec((tm, MLA_HEADS * MLA_V), row),
            pl.BlockSpec((tm, D_MODEL), row),
            _mod_spec(latent, tm, layer),
            pl.BlockSpec((D_MODEL, D_MODEL), lambda i: (0, 0)),
            _lspec((1, D_MODEL), layer),
        ],
        out_specs=(pl.BlockSpec((tm, D_MODEL), row), pl.BlockSpec((tm, D_MODEL), row)),
        compiler_params=_cparams(("parallel",)),
        name="out_proj",
    )(ya, yb, yc, x2d, mod, w_out_l, lw['norm_mlp'])


def _permute_w_in_rows(w, out_ref):
    base = OFF_QCN
    ckv0 = base + MLA_HEADS * MLA_QK
    out_ref[:, :base] = w[:, :base].astype(BF16)
    for h in range(MLA_HEADS):
        out_ref[:, OFF_QCN + h * MLA_NOPE:OFF_QCN + (h + 1) * MLA_NOPE] = (
            w[:, base + h * MLA_QK:base + h * MLA_QK + MLA_NOPE].astype(BF16))
    for pair in range(MLA_HEADS // 2):
        ropes = [w[:, base + h * MLA_QK + MLA_NOPE:base + (h + 1) * MLA_QK] for h in (2 * pair, 2 * pair + 1)]
        out_ref[:, OFF_QCR + pair * LANES:OFF_QCR + (pair + 1) * LANES] = jnp.concatenate(ropes, axis=-1).astype(BF16)
    out_ref[:, OFF_CKV:OFF_KR] = w[:, ckv0:ckv0 + MLA_KV_RANK].astype(BF16)
    kr = w[:, ckv0 + MLA_KV_RANK:]
    out_ref[:, OFF_KR:] = jnp.concatenate([kr, kr], axis=-1).astype(BF16)


def _mlp_kernel(cast_next, *refs):
    if cast_next:
        (h2_ref, x1_ref, mod_ref, w1_ref, w2_ref, n1_ref, n2_ref, n3_ref, n4_ref,
         o_ref, c1_ref, c2_ref, c3_ref, c4_ref) = refs
        c1_ref[...] = n1_ref[...].astype(BF16)
        c2_ref[...] = n2_ref[...].astype(BF16)
        c3_ref[...] = n3_ref[...].astype(BF16)
        _permute_w_in_rows(n4_ref[...], c4_ref)
    else:
        h2_ref, x1_ref, mod_ref, w1_ref, w2_ref, o_ref = refs
    j = pl.program_id(1)

    @pl.when(j == 0)
    def _():
        o_ref[...] = jnp.zeros_like(o_ref)

    f = jnp.maximum(_dot(h2_ref[...], w1_ref[...]), 0.0)
    o_ref[...] += _dot((f * f).astype(BF16), w2_ref[...])

    @pl.when(j == pl.num_programs(1) - 1)
    def _():
        o_ref[...] = x1_ref[...] + mod_ref[5:6, :] * o_ref[...]


def _mlp(h2, x1, mod, w_ff, latent, tm, tf, layer, next_w=None):
    rows = h2.shape[0]
    ni, nj = rows // tm, D_FF // tf
    in_specs = [
        pl.BlockSpec((tm, D_MODEL), lambda i, j: (i, 0)),
        pl.BlockSpec((tm, D_MODEL), lambda i, j: (i, 0), pipeline_mode=pl.Buffered(1)),
        _mod_spec(latent, tm, layer),
        pl.BlockSpec((D_MODEL, tf), lambda i, j: (0, j)),
        pl.BlockSpec((tf, D_MODEL), lambda i, j: (j, 0)),
    ]
    out_shape = [jax.ShapeDtypeStruct((rows, D_MODEL), F32)]
    out_specs = [pl.BlockSpec((tm, D_MODEL), lambda i, j: (i, 0))]
    args = [h2, x1, mod, w_ff[0], w_ff[1]]
    if next_w is not None:
        step = lambda i, j: i * nj + j
        for w, out_cols in zip(next_w, (D_FF, D_MODEL, D_MODEL, IN_WIDTH_P)):
            _, r, c = w.shape
            rb = r // (ni * nj)
            in_specs.append(pl.BlockSpec((None, rb, c), lambda i, j: (layer + 1, step(i, j), 0)))
            out_shape.append(jax.ShapeDtypeStruct((r, out_cols), BF16))
            out_specs.append(pl.BlockSpec((rb, out_cols), lambda i, j: (step(i, j), 0)))
        args += list(next_w)
    outs = pl.pallas_call(
        functools.partial(_mlp_kernel, next_w is not None),
        out_shape=tuple(out_shape),
        grid=(ni, nj),
        in_specs=in_specs,
        out_specs=tuple(out_specs),
        compiler_params=_cparams(("arbitrary", "arbitrary")),
        name="mlp_cast" if next_w is not None else "mlp",
    )(*args)
    return outs[0], (tuple(outs[1:]) if next_w is not None else None)


def _permute_w_in(w):
    base = OFF_QCN
    ckv0 = base + MLA_HEADS * MLA_QK
    n = w.shape[0]
    qc = w[:, :, base:ckv0].reshape(n, D_MODEL, MLA_HEADS, MLA_QK)
    nope = qc[..., :MLA_NOPE].reshape(n, D_MODEL, MLA_HEADS * MLA_NOPE)
    rope = qc[..., MLA_NOPE:].reshape(n, D_MODEL, MLA_HEADS * MLA_ROPE)
    kr = w[:, :, ckv0 + MLA_KV_RANK:]
    parts = [w[:, :, :base], nope, rope, w[:, :, ckv0:ckv0 + MLA_KV_RANK], kr, kr]
    return jnp.concatenate(parts, axis=2).astype(BF16)


def _rope_tables(seq):
    t = jnp.arange(seq)
    row = (t // GRID_W).astype(F32)
    col = (t % GRID_W).astype(F32)

    def table(d):
        quarter = d // 4
        inv = ROPE_BASE ** (-(jnp.arange(quarter, dtype=F32) / quarter))
        ar = row[:, None] * inv[None, :]
        ac = col[:, None] * inv[None, :]
        cos = jnp.concatenate([jnp.cos(ar), jnp.cos(ar), jnp.cos(ac), jnp.cos(ac)], axis=-1)
        sin = jnp.concatenate([-jnp.sin(ar), jnp.sin(ar), -jnp.sin(ac), jnp.sin(ac)], axis=-1)
        reps = LANES // d
        return jnp.tile(cos, (1, reps)), jnp.tile(sin, (1, reps))

    cos_b, sin_b = table(GQA_HEAD_DIM)
    cos_c, sin_c = table(MLA_ROPE)
    return {'cos_b': cos_b, 'sin_b': sin_b, 'cos_c': cos_c, 'sin_c': sin_c}


def _ssm_params(lam_re, lam_im, log_dt, b_re, b_im, c_re, c_im):
    a = lam_re.astype(F32)
    w = lam_im.astype(F32)
    dt = jnp.exp(log_dt.astype(F32))[..., None]
    mag = jnp.exp(a * dt)
    lbr = mag * jnp.cos(w * dt)
    lbi = mag * jnp.sin(w * dt)
    den = a * a + w * w
    cr = (((lbr - 1.0) * a + lbi * w) / den)[..., None]
    ci = ((lbi * a - (lbr - 1.0) * w) / den)[..., None]
    bre = b_re.astype(F32)
    bim = b_im.astype(F32)
    bb_re = cr * bre - ci * bim
    bb_im = cr * bim + ci * bre
    lam_ri = jnp.stack([lbr.reshape(DEPTH, 2, SSM_N), lbi.reshape(DEPTH, 2, SSM_N)], axis=2)
    lam_b = jnp.broadcast_to(lam_ri[:, :, :, None, :], (DEPTH, 2, 2, SUBLANES, SSM_N))
    gh = SSM_GROUPS // 2
    eye = jnp.eye(gh, dtype=F32)[:, None, :, None]

    def blockdiag(x, rows_per_g, cols_per_g):
        x = x.reshape(DEPTH, 2, 2, gh, rows_per_g, 1, cols_per_g) * eye
        return x.reshape(DEPTH, 2, 2, gh * rows_per_g, gh * cols_per_g)

    def bmat(x):
        return blockdiag(jnp.swapaxes(x, -1, -2), SSM_GROUP, SSM_STATE)

    def cmat(x):
        return blockdiag(jnp.swapaxes(x, -1, -2), SSM_STATE, SSM_GROUP)

    b_cat = jnp.concatenate([bmat(bb_re), bmat(bb_im)], axis=-1).astype(BF16)
    return {'lam': lam_b, 'b': b_cat, 'c_re': cmat(c_re.astype(F32)).astype(BF16),
            'c_im': cmat(c_im.astype(F32)).astype(BF16)}


def _trunk_layer(x2d, mod, lw, sp, tabs, latent, ctx, layer, w_big, next_w=None):
    seq = DEC_SEQ if latent else SEQ
    bsz = x2d.shape[0] // seq
    u, qb, kb, vb, qc, ckv, kr, krg = _in_proj(x2d, mod, latent, lw, w_big[3], tabs, TM_PROJ, layer)

    u3 = u.reshape(bsz, seq, SSM_CH)
    ya, h_t = _ssm_scan(u3, ctx['h0'], sp, lw['ssm_d'], lw['ssm_w_glu'], layer)
    ya = ya.reshape(bsz * seq, SSM_CH)

    if latent:
        yb = _gqa_attn(qb, kb, vb, True, ctx['k'], ctx['v'], layer)
        yc = _mla_attn(qc, ckv, kr, krg, lw, True, ctx['ckv'], ctx['kr'], layer)
    else:
        yb = _gqa_attn(qb, kb, vb, False)
        yc = _mla_attn(qc, ckv, kr, krg, lw, False, layer=layer)

    x1, h2 = _out_proj(ya, yb, yc, x2d, mod, lw, w_big[2], latent, TM_PROJ, layer)
    x2, w_big_next = _mlp(h2, x1, mod, w_big[:2], latent, TM_MLP, TF_MLP, layer, next_w)
    return x2, (kb, vb, ckv, kr, h_t), w_big_next


def kernel(x_prompt, x_sample, cache_attn_k, cache_attn_v, cache_mla_ckv, cache_mla_krope, state_ssm, c, c_ctx, w_mod, b_mod, norm_mix, norm_mlp, w_in, gqa_q_norm, gqa_k_norm, mla_kv_norm, mla_q_norm, mla_k_norm, mla_w_uk, mla_w_uv, ssm_lam_re, ssm_lam_im, ssm_log_dt, ssm_b_re, ssm_b_im, ssm_c_re, ssm_c_im, ssm_d, ssm_w_glu, w_out, w_ff1, w_ff2):
    cvec = jnp.zeros((N_MOD, D_MODEL), F32).at[0].set(c_ctx).at[1:1 + DEC_BATCH].set(c)
    mod = _adaln(cvec, w_mod, b_mod.reshape(DEPTH, 1, 6 * D_MODEL)).reshape(DEPTH, N_MOD, 6, D_MODEL)

    tabs = _rope_tables(DEC_SEQ)
    cache_k = cache_attn_k.reshape(DEC_BATCH, DEPTH, PAST_LEN, GQA_KV_HEADS * GQA_HEAD_DIM)
    cache_v = cache_attn_v.reshape(DEC_BATCH, DEPTH, PAST_LEN, GQA_KV_HEADS * GQA_HEAD_DIM)

    xp = x_prompt.reshape(BATCH * SEQ, D_MODEL)
    xs = x_sample.reshape(DEC_BATCH * DEC_SEQ, D_MODEL)
    new_k, new_v, new_ckv, new_kr, new_ssm = [], [], [], [], []
    h0_zero = jnp.zeros((BATCH // SUBLANES, 2, 2, SUBLANES, SSM_N), F32)

    row3 = lambda a: a.reshape(DEPTH, 1, -1)
    dup = lambda a: jnp.concatenate([a, a], axis=-1)
    lw = {
        'norm_mix': row3(norm_mix), 'norm_mlp': row3(norm_mlp),
        'gqa_q_norm': row3(gqa_q_norm), 'gqa_k_norm': row3(gqa_k_norm), 'mla_kv_norm': row3(mla_kv_norm),
        'mla_q_nope_g': row3(mla_q_norm[:, :MLA_NOPE]),
        'mla_q_rope_g': row3(dup(mla_q_norm[:, MLA_NOPE:])),
        'mla_k_nope_g': row3(mla_k_norm[:, :MLA_NOPE]),
        'mla_k_rope_g': row3(dup(mla_k_norm[:, MLA_NOPE:])),
        'mla_k_rope_g64': row3(mla_k_norm[:, MLA_NOPE:]),
        'mla_w_uk': mla_w_uk.reshape(DEPTH, MLA_KV_RANK, MLA_HEADS * MLA_NOPE).astype(BF16),
        'mla_w_uv': mla_w_uv.reshape(DEPTH, MLA_KV_RANK, MLA_HEADS * MLA_V).astype(BF16),
        'ssm_d': row3(ssm_d), 'ssm_w_glu': ssm_w_glu.astype(BF16),
    }
    sp = _ssm_params(ssm_lam_re, ssm_lam_im, ssm_log_dt, ssm_b_re, ssm_b_im, ssm_c_re, ssm_c_im)
    h0_lat = state_ssm.reshape(DEC_BATCH, DEPTH, 2, SSM_N, 2).transpose(1, 2, 4, 0, 3)[:, None]

    w_big = (w_ff1[0].astype(BF16), w_ff2[0].astype(BF16), w_out[0].astype(BF16), _permute_w_in(w_in[:1])[0])
    for l in range(DEPTH):
        next_w = (w_ff1, w_ff2, w_out, w_in) if l + 1 < DEPTH else None
        xp, (k, v, ckv_n, kr, h_t), w_big_next = _trunk_layer(xp, mod, lw, sp, tabs, False, {'h0': h0_zero}, l,
                                                             w_big, next_w)
        new_k.append(k.reshape(BATCH, SEQ, GQA_KV_HEADS, GQA_HEAD_DIM))
        new_v.append(v.reshape(BATCH, SEQ, GQA_KV_HEADS, GQA_HEAD_DIM))
        new_ckv.append(ckv_n.reshape(BATCH, SEQ, MLA_KV_RANK))
        new_kr.append(kr.reshape(BATCH, SEQ, MLA_ROPE))
        hs = h_t.transpose(0, 3, 1, 4, 2).reshape(BATCH, 2, SSM_GROUPS, SSM_STATE, 2)
        new_ssm.append(hs)

        ctx = {'k': cache_k, 'v': cache_v, 'ckv': cache_mla_ckv, 'kr': cache_mla_krope, 'h0': h0_lat[l]}
        xs, _, _ = _trunk_layer(xs, mod, lw, sp, tabs, True, ctx, l, w_big)
        w_big = w_big_next

    return (xp.reshape(BATCH, SEQ, D_MODEL), xs.reshape(DEC_BATCH, DEC_SEQ, D_MODEL),
            jnp.stack(new_k, axis=1), jnp.stack(new_v, axis=1), jnp.stack(new_ckv, axis=1),
            jnp.stack(new_kr, axis=1), jnp.stack(new_ssm, axis=1))
```

```python
import functools
import math

import jax
import jax.numpy as jnp
from jax import lax
from jax.experimental import pallas as pl
from jax.experimental.pallas import tpu as pltpu

D_MODEL = 2048
BATCH = 32
SEQ = 256
DEPTH = 2
DEC_BATCH = 8
DEC_SEQ = 1024
PAST_LEN = 512
GRID_W = 64
ROPE_BASE = 10000.0
EPS = 1e-6
SSM_CH = 512
SSM_GROUP = 16
SSM_GROUPS = SSM_CH // SSM_GROUP
SSM_STATE = 64
SSM_N = SSM_GROUPS * SSM_STATE
GQA_HEADS = 6
GQA_KV_HEADS = 2
GQA_GROUP = GQA_HEADS // GQA_KV_HEADS
GQA_HEAD_DIM = 128
MLA_HEADS = 6
MLA_NOPE = 128
MLA_ROPE = 64
MLA_QK = MLA_NOPE + MLA_ROPE
MLA_QK_PAD = 256
MLA_V = 128
MLA_KV_RANK = 512
D_FF = 4 * D_MODEL
N_MOD = 16

OFF_U = 0
OFF_QB = OFF_U + SSM_CH
OFF_KB = OFF_QB + GQA_HEADS * GQA_HEAD_DIM
OFF_VB = OFF_KB + GQA_KV_HEADS * GQA_HEAD_DIM
OFF_QCN = OFF_VB + GQA_KV_HEADS * GQA_HEAD_DIM
OFF_QCR = OFF_QCN + MLA_HEADS * MLA_NOPE
OFF_CKV = OFF_QCR + MLA_HEADS * MLA_ROPE
OFF_KR = OFF_CKV + MLA_KV_RANK
IN_WIDTH_P = OFF_KR + 2 * MLA_ROPE

LANES = 128
SUBLANES = 8
VMEM_LIMIT = 56 * 1024 * 1024

TM_PROJ = 512
TM_MLP = 1024
TF_MLP = 512
ROW_PIECES = 2
IN_PROJ_PIECES = 2

BF16 = jnp.bfloat16
F32 = jnp.float32


def _cparams(sem):
    return pltpu.CompilerParams(dimension_semantics=sem, vmem_limit_bytes=VMEM_LIMIT)


def _dot(a, b):
    return jnp.dot(a, b, preferred_element_type=F32)


def _dot_nt(a, b):
    return lax.dot_general(a, b, (((1,), (1,)), ((), ())), preferred_element_type=F32)


def _adaln_kernel(c_ref, w_ref, b_ref, o_ref):
    c = c_ref[...]
    s = (c * jax.nn.sigmoid(c)).astype(BF16)
    o_ref[...] = _dot(s, w_ref[...].astype(BF16)) + b_ref[...]


def _adaln(cvec, w_mod, b_mod):
    tn = 1024
    return pl.pallas_call(
        _adaln_kernel,
        out_shape=jax.ShapeDtypeStruct((DEPTH, N_MOD, 6 * D_MODEL), F32),
        grid=(DEPTH, 6 * D_MODEL // tn),
        in_specs=[
            pl.BlockSpec((N_MOD, D_MODEL), lambda l, j: (0, 0)),
            pl.BlockSpec((None, D_MODEL, tn), lambda l, j: (l, 0, j)),
            pl.BlockSpec((None, 1, tn), lambda l, j: (l, 0, j)),
        ],
        out_specs=pl.BlockSpec((None, N_MOD, tn), lambda l, j: (l, 0, j)),
        compiler_params=_cparams(("parallel", "parallel")),
        name="adaln",
    )(cvec, w_mod, b_mod)


def _swap_halves(x, block):
    lane = lax.broadcasted_iota(jnp.int32, x.shape, 1)
    first = (lane % (2 * block)) < block
    return jnp.where(first, pltpu.roll(x, LANES - block, 1), pltpu.roll(x, block, 1))


def _rope(x, cos, sin_signed, block):
    return x * cos + _swap_halves(x, block) * sin_signed


def _in_proj_kernel(latent, x_ref, mod_ref, gmix_ref, w_ref, gq_ref, gk_ref, gkv_ref, gqn_ref, gqr_ref, gkr_ref,
                    cosb_ref, sinb_ref, cosc_ref, sinc_ref,
                    u_ref, qb_ref, kb_ref, vb_ref, qc_ref, ckv_ref, kr_ref, krg_ref):
    piece = x_ref.shape[0] // IN_PROJ_PIECES
    for r in range(IN_PROJ_PIECES):
        rs = pl.ds(r * piece, piece)
        _in_proj_rows(latent, x_ref.at[rs], mod_ref, gmix_ref, w_ref, gq_ref, gk_ref, gkv_ref, gqn_ref, gqr_ref,
                      gkr_ref, cosb_ref.at[rs], sinb_ref.at[rs], cosc_ref.at[rs], sinc_ref.at[rs],
                      u_ref.at[rs], qb_ref.at[rs], kb_ref.at[rs], vb_ref.at[rs], qc_ref.at[:, rs], ckv_ref.at[rs],
                      kr_ref.at[rs], krg_ref.at[rs])


def _in_proj_rows(latent, x_ref, mod_ref, gmix_ref, w_ref, gq_ref, gk_ref, gkv_ref, gqn_ref, gqr_ref, gkr_ref,
                  cosb_ref, sinb_ref, cosc_ref, sinc_ref,
                  u_ref, qb_ref, kb_ref, vb_ref, qc_ref, ckv_ref, kr_ref, krg_ref):
    x = x_ref[...]
    ms = jnp.mean(x * x, axis=-1, keepdims=True)
    y = x * lax.rsqrt(ms + EPS) * gmix_ref[...]
    h = (y * (1.0 + mod_ref[1:2, :]) + mod_ref[0:1, :]).astype(BF16)

    def proj(off, width):
        return _dot(h, w_ref[:, off:off + width])

    scale_b = 1.0 / math.sqrt(GQA_HEAD_DIM)
    zq = proj(OFF_QB, GQA_HEADS * GQA_HEAD_DIM)
    for hd in range(GQA_HEADS):
        col = zq[:, hd * LANES:(hd + 1) * LANES]
        q = col * lax.rsqrt(jnp.mean(col * col, axis=-1, keepdims=True) + EPS) * gq_ref[...]
        if latent:
            q = _rope(q, cosb_ref[...], sinb_ref[...], GQA_HEAD_DIM // 4)
        qb_ref[:, hd * LANES:(hd + 1) * LANES] = (q * scale_b).astype(qb_ref.dtype)
    zk = proj(OFF_KB, GQA_KV_HEADS * GQA_HEAD_DIM)
    for hd in range(GQA_KV_HEADS):
        col = zk[:, hd * LANES:(hd + 1) * LANES]
        k = col * lax.rsqrt(jnp.mean(col * col, axis=-1, keepdims=True) + EPS) * gk_ref[...]
        if latent:
            k = _rope(k, cosb_ref[...], sinb_ref[...], GQA_HEAD_DIM // 4)
        kb_ref[:, hd * LANES:(hd + 1) * LANES] = k.astype(kb_ref.dtype)

    scale_c = 1.0 / math.sqrt(MLA_QK)
    zn = proj(OFF_QCN, MLA_HEADS * MLA_NOPE)
    zr = proj(OFF_QCR, MLA_HEADS * MLA_ROPE)
    lane = lax.broadcasted_iota(jnp.int32, (x.shape[0], LANES), 1)
    low = lane < MLA_ROPE
    for pair in range(MLA_HEADS // 2):
        colr = zr[:, pair * LANES:(pair + 1) * LANES]
        sq = colr * colr
        ss_lo = jnp.sum(jnp.where(low, sq, 0.0), axis=-1, keepdims=True)
        ss_hi = jnp.sum(jnp.where(low, 0.0, sq), axis=-1, keepdims=True)
        rs = []
        for half, ss_r in ((0, ss_lo), (1, ss_hi)):
            hd = 2 * pair + half
            coln = zn[:, hd * LANES:(hd + 1) * LANES]
            ss = jnp.sum(coln * coln, axis=-1, keepdims=True) + ss_r
            r = lax.rsqrt(ss * (1.0 / MLA_QK) + EPS)
            rs.append(r)
            qc_ref[hd, :, :MLA_NOPE] = (coln * r * gqn_ref[...] * scale_c).astype(qc_ref.dtype)
        qr = colr * jnp.where(low, rs[0], rs[1]) * gqr_ref[...]
        if latent:
            qr = _rope(qr, cosc_ref[...], sinc_ref[...], MLA_ROPE // 4)
        qr = qr * scale_c
        zeros = jnp.zeros((x.shape[0], MLA_QK_PAD - MLA_QK), qc_ref.dtype)
        for half in range(2):
            hd = 2 * pair + half
            qc_ref[hd, :, MLA_NOPE:MLA_QK] = qr[:, half * MLA_ROPE:(half + 1) * MLA_ROPE].astype(qc_ref.dtype)
            qc_ref[hd, :, MLA_QK:] = zeros

    zc = proj(OFF_CKV, MLA_KV_RANK)
    ckv_ref[...] = (zc * lax.rsqrt(jnp.mean(zc * zc, axis=-1, keepdims=True) + EPS) * gkv_ref[...]).astype(ckv_ref.dtype)
    zkr = proj(OFF_KR, 2 * MLA_ROPE)
    kr_ref[...] = zkr[:, :MLA_ROPE]
    krg = zkr * gkr_ref[...]
    if latent:
        krg = _rope(krg, cosc_ref[...], sinc_ref[...], MLA_ROPE // 4)
    krg_ref[...] = krg[:, :MLA_ROPE]

    vb_ref[...] = proj(OFF_VB, GQA_KV_HEADS * GQA_HEAD_DIM).astype(vb_ref.dtype)
    u_ref[...] = proj(OFF_U, SSM_CH)


def _lspec(block_tail, layer):
    zeros = (0,) * len(block_tail)
    return pl.BlockSpec((None,) + tuple(block_tail), lambda *_: (layer,) + zeros)


def _mod_spec(latent, tm, layer):
    tiles_per_seq = DEC_SEQ // tm
    if latent:
        return pl.BlockSpec((None, None, 6, D_MODEL), lambda i, *_: (layer, 1 + i // tiles_per_seq, 0, 0))
    return pl.BlockSpec((None, None, 6, D_MODEL), lambda i, *_: (layer, 0, 0, 0))


def _in_proj(x2d, mod, latent, lw, tabs, tm, layer):
    rows = x2d.shape[0]
    n_tiles = rows // tm
    tiles_per_seq = DEC_SEQ // tm
    if latent:
        tab_map = lambda i: (i % tiles_per_seq, 0)
    else:
        tab_map = lambda i: (0, 0)
    row = lambda i: (i, 0)
    act_dt = BF16 if latent else F32
    in_specs = [
        pl.BlockSpec((tm, D_MODEL), row),
        _mod_spec(latent, tm, layer),
        _lspec((1, D_MODEL), layer),
        _lspec((D_MODEL, IN_WIDTH_P), layer),
        _lspec((1, LANES), layer),
        _lspec((1, LANES), layer),
        _lspec((1, MLA_KV_RANK), layer),
        _lspec((1, LANES), layer),
        _lspec((1, LANES), layer),
        _lspec((1, LANES), layer),
        pl.BlockSpec((tm, LANES), tab_map),
        pl.BlockSpec((tm, LANES), tab_map),
        pl.BlockSpec((tm, LANES), tab_map),
        pl.BlockSpec((tm, LANES), tab_map),
    ]
    out_shape = (
        jax.ShapeDtypeStruct((rows, SSM_CH), F32),
        jax.ShapeDtypeStruct((rows, GQA_HEADS * GQA_HEAD_DIM), BF16),
        jax.ShapeDtypeStruct((rows, GQA_KV_HEADS * GQA_HEAD_DIM), act_dt),
        jax.ShapeDtypeStruct((rows, GQA_KV_HEADS * GQA_HEAD_DIM), act_dt),
        jax.ShapeDtypeStruct((MLA_HEADS, rows, MLA_QK_PAD), BF16),
        jax.ShapeDtypeStruct((rows, MLA_KV_RANK), act_dt),
        jax.ShapeDtypeStruct((rows, MLA_ROPE), F32),
        jax.ShapeDtypeStruct((rows, MLA_ROPE), F32),
    )
    out_specs = (
        pl.BlockSpec((tm, SSM_CH), row),
        pl.BlockSpec((tm, GQA_HEADS * GQA_HEAD_DIM), row),
        pl.BlockSpec((tm, GQA_KV_HEADS * GQA_HEAD_DIM), row),
        pl.BlockSpec((tm, GQA_KV_HEADS * GQA_HEAD_DIM), row),
        pl.BlockSpec((MLA_HEADS, tm, MLA_QK_PAD), lambda i: (0, i, 0)),
        pl.BlockSpec((tm, MLA_KV_RANK), row),
        pl.BlockSpec((tm, MLA_ROPE), row),
        pl.BlockSpec((tm, MLA_ROPE), row),
    )
    return pl.pallas_call(
        functools.partial(_in_proj_kernel, latent),
        out_shape=out_shape,
        grid=(n_tiles,),
        in_specs=in_specs,
        out_specs=out_specs,
        compiler_params=_cparams(("parallel",)),
        name="in_proj_lat" if latent else "in_proj_ctx",
    )(x2d, mod, lw['norm_mix'], lw['w_in'], lw['gqa_q_norm'], lw['gqa_k_norm'], lw['mla_kv_norm'],
      lw['mla_q_nope_g'], lw['mla_q_rope_g'], lw['mla_k_rope_g'],
      tabs['cos_b'], tabs['sin_b'], tabs['cos_c'], tabs['sin_c'])


SSM_TC = 128
SSM_HALVES = 2


def _ssm_scan_kernel(n_chunks, backward, *refs):
    if backward:
        (u_ref, h0_ref, lam_ref, b_ref, cre_ref, cim_ref, yf_ref, dsk_ref, wglu_ref, o_ref, hT_ref,
         utm_ref, sre0_ref, sim0_ref, sre1_ref, sim1_ref, h0s_ref, h1s_ref, y_ref, ycat_ref) = refs
    else:
        (u_ref, h0_ref, lam_ref, b_ref, cre_ref, cim_ref, y_ref, hT_ref,
         utm_ref, sre0_ref, sim0_ref, sre1_ref, sim1_ref, h0s_ref, h1s_ref) = refs
    i = pl.program_id(1)
    tc = SSM_TC
    half_n = SSM_N // SSM_HALVES
    halves = ((sre0_ref, sim0_ref, h0s_ref), (sre1_ref, sim1_ref, h1s_ref))

    @pl.when(i == 0)
    def _():
        for k, (_, _, hk_ref) in enumerate(halves):
            hk_ref[...] = h0_ref[:, :, k * half_n:(k + 1) * half_n]

    for b in range(SUBLANES):
        for j in range(SSM_CH // LANES):
            utm_ref[j, pl.ds(b, tc, stride=SUBLANES), :] = u_ref[b, :, j * LANES:(j + 1) * LANES]

    for k, (sre_ref, sim_ref, _) in enumerate(halves):
        uk = jnp.concatenate([utm_ref[2 * k], utm_ref[2 * k + 1]], axis=-1).astype(BF16)
        r = _dot(uk, b_ref[k])
        sre_ref[...] = r[:, :half_n]
        sim_ref[...] = r[:, half_n:]

    for k, (sre_ref, sim_ref, hk_ref) in enumerate(halves):
        sl = slice(k * half_n, (k + 1) * half_n)
        lr = lam_ref[0, :, sl]
        li = lam_ref[1, :, sl]
        hr = hk_ref[0]
        hi = hk_ref[1]
        for s in range(tc):
            t = (tc - 1 - s) if backward else s
            rows = slice(t * SUBLANES, (t + 1) * SUBLANES)
            hr, hi = (lr * hr - li * hi + sre_ref[rows, :], lr * hi + li * hr + sim_ref[rows, :])
            sre_ref[rows, :] = hr
            sim_ref[rows, :] = hi
        hk_ref[0] = hr
        hk_ref[1] = hi

    for k, (sre_ref, sim_ref, _) in enumerate(halves):
        yk = _dot(sre_ref[...].astype(BF16), cre_ref[k]) - _dot(sim_ref[...].astype(BF16), cim_ref[k])
        y_ref[2 * k] = yk[:, :LANES]
        y_ref[2 * k + 1] = yk[:, LANES:]

    if backward:
        for b in range(SUBLANES):
            for j in range(SSM_CH // LANES):
                sl = slice(j * LANES, (j + 1) * LANES)
                rows_b = pl.ds(b, tc, stride=SUBLANES)
                ycat_ref[b * tc:(b + 1) * tc, sl] = (yf_ref.at[j][rows_b, :] + y_ref.at[j][rows_b, :]
                                                    + dsk_ref[:, sl] * u_ref[b, :, sl])
        zg = _dot(ycat_ref[...].astype(BF16), wglu_ref[...])
        out = zg[:, :SSM_CH] * jax.nn.sigmoid(zg[:, SSM_CH:])
        o_ref[...] = out.reshape(SUBLANES, tc, SSM_CH).astype(o_ref.dtype)

    @pl.when(i == n_chunks - 1)
    def _():
        for k, (_, _, hk_ref) in enumerate(halves):
            hT_ref[:, :, k * half_n:(k + 1) * half_n] = hk_ref[...]


def _ssm_scan(u3, h0, sp, d_skip, w_glu, layer):
    bsz, t, _ = u3.shape
    n_groups = bsz // SUBLANES
    n_chunks = t // SSM_TC
    tc = SSM_TC
    y_shape = jax.ShapeDtypeStruct((n_groups, SSM_CH // LANES, t * SUBLANES, LANES), F32)
    h_shape = jax.ShapeDtypeStruct((n_groups, 2, SUBLANES, SSM_N), F32)
    h_spec = pl.BlockSpec((None, 2, SUBLANES, SSM_N), lambda g, i: (g, 0, 0, 0))

    def one_direction(d, extra_args, extra_specs, out_shape, out_spec, extra_scratch):
        chunk = (lambda i: n_chunks - 1 - i) if d else (lambda i: i)
        y_like = lambda g, i: (g, 0, chunk(i), 0)
        return pl.pallas_call(
            functools.partial(_ssm_scan_kernel, n_chunks, d),
            out_shape=(out_shape, h_shape),
            grid=(n_groups, n_chunks),
            in_specs=[
                pl.BlockSpec((SUBLANES, tc, SSM_CH), lambda g, i: (g, chunk(i), 0)),
                pl.BlockSpec((None, None, 2, SUBLANES, SSM_N), lambda g, i: (g, d, 0, 0, 0)),
                pl.BlockSpec((None, None, 2, SUBLANES, SSM_N), lambda g, i: (layer, d, 0, 0, 0)),
                pl.BlockSpec((None, None, 2, SSM_CH // 2, SSM_N), lambda g, i: (layer, d, 0, 0, 0)),
                pl.BlockSpec((None, None, 2, SSM_N // 2, SSM_CH // 2), lambda g, i: (layer, d, 0, 0, 0)),
                pl.BlockSpec((None, None, 2, SSM_N // 2, SSM_CH // 2), lambda g, i: (layer, d, 0, 0, 0)),
            ] + extra_specs(y_like),
            out_specs=(out_spec(chunk), h_spec),
            scratch_shapes=_ssm_scratch(tc) + extra_scratch,
            compiler_params=_cparams(("parallel", "arbitrary")),
            name="ssm_scan_bwd" if d else "ssm_scan_fwd",
        )(u3, h0, sp['lam'], sp['b'], sp['c_re'], sp['c_im'], *extra_args)

    y_block = (None, SSM_CH // LANES, tc * SUBLANES, LANES)
    y_f, h_f = one_direction(0, [], lambda y_like: [], y_shape,
                             lambda chunk: pl.BlockSpec(y_block, lambda g, i: (g, 0, chunk(i), 0)), [])
    ya, h_b = one_direction(
        1, [y_f, d_skip, w_glu],
        lambda y_like: [pl.BlockSpec(y_block, y_like), _lspec((1, SSM_CH), layer),
                        _lspec((SSM_CH, 2 * SSM_CH), layer)],
        jax.ShapeDtypeStruct((bsz, t, SSM_CH), BF16),
        lambda chunk: pl.BlockSpec((SUBLANES, tc, SSM_CH), lambda g, i: (g, chunk(i), 0)),
        [pltpu.VMEM((SSM_CH // LANES, tc * SUBLANES, LANES), F32), pltpu.VMEM((SUBLANES * tc, SSM_CH), F32)])
    return ya, jnp.stack([h_f, h_b], axis=1)


def _ssm_scratch(tc):
    return [
            pltpu.VMEM((SSM_CH // LANES, tc * SUBLANES, LANES), F32),
            pltpu.VMEM((tc * SUBLANES, SSM_N // SSM_HALVES), F32),
            pltpu.VMEM((tc * SUBLANES, SSM_N // SSM_HALVES), F32),
            pltpu.VMEM((tc * SUBLANES, SSM_N // SSM_HALVES), F32),
            pltpu.VMEM((tc * SUBLANES, SSM_N // SSM_HALVES), F32),
            pltpu.VMEM((2, SUBLANES, SSM_N // SSM_HALVES), F32),
            pltpu.VMEM((2, SUBLANES, SSM_N // SSM_HALVES), F32),
        ]


def _ones_column(n):
    return jnp.ones((n, LANES), BF16)


def _softmax_pv(scores, v_ext):
    m = jnp.max(scores, axis=-1, keepdims=True)
    p = jnp.exp((scores - m).astype(BF16))
    o = _dot(p, v_ext)
    return o[:, :LANES] / o[:, LANES:]


ATTN_TQ = 256


def _for_each_q_block(nb, seq, body):
    nq = seq // ATTN_TQ
    for s in range(nb):
        for qi in range(nq):
            body(s, s * seq + qi * ATTN_TQ)


def _gqa_kernel(latent, nb, seq, *refs):
    if latent:
        q_ref, k_ref, v_ref, kc_ref, vc_ref, o_ref, k_s, v_s = refs
    else:
        q_ref, k_ref, v_ref, o_ref, k_s, v_s = refs
    tq = ATTN_TQ
    for s in range(nb):
        for h in range(GQA_KV_HEADS):
            hl = slice(h * LANES, (h + 1) * LANES)
            k_s[s, h, 0:seq, :] = k_ref[s * seq:(s + 1) * seq, hl].astype(BF16)
            v_s[s, h, 0:seq, :LANES] = v_ref[s * seq:(s + 1) * seq, hl].astype(BF16)
            if latent:
                k_s[s, h, seq:, :] = kc_ref[:, hl].astype(BF16)
                v_s[s, h, seq:, :LANES] = vc_ref[:, hl].astype(BF16)
            v_s[s, h, :, LANES:] = _ones_column(v_s.shape[2])

    def body(s, r0):
        for h in range(GQA_KV_HEADS):
            heads = [h * GQA_GROUP + g for g in range(GQA_GROUP)]
            q3 = jnp.concatenate([q_ref[pl.ds(r0, tq), hd * LANES:(hd + 1) * LANES] for hd in heads], axis=0)
            o = _softmax_pv(_dot_nt(q3, k_s[s, h]), v_s[s, h])
            for g, hd in enumerate(heads):
                o_ref[pl.ds(r0, tq), hd * LANES:(hd + 1) * LANES] = o[g * tq:(g + 1) * tq].astype(o_ref.dtype)

    _for_each_q_block(nb, seq, body)


def _gqa_attn(qb, kb, vb, latent, cache_k=None, cache_v=None, layer=0):
    rows = qb.shape[0]
    seq = DEC_SEQ if latent else SEQ
    nb = 1 if latent else 4
    t_all = seq + (PAST_LEN if latent else 0)
    qw = GQA_HEADS * GQA_HEAD_DIM
    kw = GQA_KV_HEADS * GQA_HEAD_DIM
    row = lambda i: (i, 0)
    in_specs = [
        pl.BlockSpec((nb * seq, qw), row),
        pl.BlockSpec((nb * seq, kw), row),
        pl.BlockSpec((nb * seq, kw), row),
    ]
    args = [qb, kb, vb]
    if latent:
        cspec = pl.BlockSpec((None, None, PAST_LEN, kw), lambda i: (i, layer, 0, 0))
        in_specs += [cspec, cspec]
        args += [cache_k, cache_v]
    return pl.pallas_call(
        functools.partial(_gqa_kernel, latent, nb, seq),
        out_shape=jax.ShapeDtypeStruct((rows, qw), BF16),
        grid=(rows // (nb * seq),),
        in_specs=in_specs,
        out_specs=pl.BlockSpec((nb * seq, qw), row),
        scratch_shapes=[
            pltpu.VMEM((nb, GQA_KV_HEADS, t_all, GQA_HEAD_DIM), BF16),
            pltpu.VMEM((nb, GQA_KV_HEADS, t_all, 2 * LANES), BF16),
        ],
        compiler_params=_cparams(("parallel",)),
        name="gqa_lat" if latent else "gqa_ctx",
    )(*args)


def _mla_kernel(latent, nb, seq, *refs):
    if latent:
        (q_ref, ckv_ref, kr_ref, krg_ref, ckvc_ref, krc_ref, wuk_ref, wuv_ref, gn_ref, gr_ref,
         o_ref, k_s, v_s) = refs
    else:
        (q_ref, ckv_ref, kr_ref, krg_ref, wuk_ref, wuv_ref, gn_ref, gr_ref, o_ref, k_s, v_s) = refs
    tq = ATTN_TQ

    def expand(ckv, kr_raw, krg, place):
        c = ckv.astype(BF16)
        kn_all = _dot(c, wuk_ref[...])
        v_all = _dot(c, wuv_ref[...])
        ss_kr = jnp.sum(kr_raw * kr_raw, axis=-1, keepdims=True)
        for h in range(MLA_HEADS):
            hl = slice(h * LANES, (h + 1) * LANES)
            kn = kn_all[:, hl]
            rs = lax.rsqrt((jnp.sum(kn * kn, axis=-1, keepdims=True) + ss_kr) * (1.0 / MLA_QK) + EPS)
            kn = (kn * rs * gn_ref[...]).astype(BF16)
            kp = (krg * rs).astype(BF16)
            vv = v_all[:, hl].astype(BF16)
            for s, lo, n, rows in place:
                k_s[s, h, lo:lo + n, :MLA_NOPE] = kn[rows]
                k_s[s, h, lo:lo + n, MLA_NOPE:MLA_QK] = kp[rows]
                k_s[s, h, lo:lo + n, MLA_QK:] = jnp.zeros((n, MLA_QK_PAD - MLA_QK), BF16)
                v_s[s, h, lo:lo + n, :MLA_V] = vv[rows]
                v_s[s, h, lo:lo + n, MLA_V:] = _ones_column(n)

    expand(ckv_ref[...], kr_ref[...], krg_ref[...],
           [(s, 0, seq, slice(s * seq, (s + 1) * seq)) for s in range(nb)])
    if latent:
        krc = krc_ref[...]
        expand(ckvc_ref[...], krc, krc * gr_ref[...], [(0, seq, PAST_LEN, slice(0, PAST_LEN))])

    def body(s, r0):
        for h in range(MLA_HEADS):
            sc = _dot_nt(q_ref[h, pl.ds(r0, tq), :], k_s[s, h])
            o_ref[pl.ds(r0, tq), h * LANES:(h + 1) * LANES] = _softmax_pv(sc, v_s[s, h]).astype(o_ref.dtype)

    _for_each_q_block(nb, seq, body)


def _mla_attn(qc, ckv, kr, krg, lw, latent, cache_ckv=None, cache_kr=None, layer=0):
    rows = qc.shape[1]
    seq = DEC_SEQ if latent else SEQ
    nb = 1 if latent else 4
    t_all = seq + (PAST_LEN if latent else 0)
    row = lambda i: (i, 0)
    in_specs = [
        pl.BlockSpec((MLA_HEADS, nb * seq, MLA_QK_PAD), lambda i: (0, i, 0)),
        pl.BlockSpec((nb * seq, MLA_KV_RANK), row),
        pl.BlockSpec((nb * seq, MLA_ROPE), row),
        pl.BlockSpec((nb * seq, MLA_ROPE), row),
    ]
    args = [qc, ckv, kr, krg]
    if latent:
        in_specs += [
            pl.BlockSpec((None, None, PAST_LEN, MLA_KV_RANK), lambda i: (i, layer, 0, 0)),
            pl.BlockSpec((None, None, PAST_LEN, MLA_ROPE), lambda i: (i, layer, 0, 0)),
        ]
        args += [cache_ckv, cache_kr]
    in_specs += [
        _lspec((MLA_KV_RANK, MLA_HEADS * MLA_NOPE), layer),
        _lspec((MLA_KV_RANK, MLA_HEADS * MLA_V), layer),
        _lspec((1, MLA_NOPE), layer),
        _lspec((1, MLA_ROPE), layer),
    ]
    args += [lw['mla_w_uk'], lw['mla_w_uv'], lw['mla_k_nope_g'], lw['mla_k_rope_g64']]
    return pl.pallas_call(
        functools.partial(_mla_kernel, latent, nb, seq),
        out_shape=jax.ShapeDtypeStruct((rows, MLA_HEADS * MLA_V), BF16),
        grid=(rows // (nb * seq),),
        in_specs=in_specs,
        out_specs=pl.BlockSpec((nb * seq, MLA_HEADS * MLA_V), row),
        scratch_shapes=[
            pltpu.VMEM((nb, MLA_HEADS, t_all, MLA_QK_PAD), BF16),
            pltpu.VMEM((nb, MLA_HEADS, t_all, MLA_V + LANES), BF16),
        ],
        compiler_params=_cparams(("parallel",)),
        name="mla_lat" if latent else "mla_ctx",
    )(*args)


def _out_proj_kernel(ya_ref, yb_ref, yc_ref, x_ref, mod_ref, w_ref, g_ref, x1_ref, h2_ref):
    wa = SSM_CH
    wb = wa + GQA_HEADS * GQA_HEAD_DIM
    piece = x_ref.shape[0] // ROW_PIECES
    pieces = [slice(r * piece, (r + 1) * piece) for r in range(ROW_PIECES)]
    outs = [(_dot(ya_ref[rs, :], w_ref[0:wa, :]) + _dot(yb_ref[rs, :], w_ref[wa:wb, :])
             + _dot(yc_ref[rs, :], w_ref[wb:, :])) for rs in pieces]
    for rs, o in zip(pieces, outs):
        x1 = x_ref[rs, :] + mod_ref[2:3, :] * o
        x1_ref[rs, :] = x1
        ms = jnp.mean(x1 * x1, axis=-1, keepdims=True)
        y = x1 * lax.rsqrt(ms + EPS) * g_ref[...]
        h2_ref[rs, :] = (y * (1.0 + mod_ref[4:5, :]) + mod_ref[3:4, :]).astype(h2_ref.dtype)


def _out_proj(ya, yb, yc, x2d, mod, lw, w_out_l, latent, tm, layer):
    rows = x2d.shape[0]
    row = lambda i: (i, 0)
    return pl.pallas_call(
        _out_proj_kernel,
        out_shape=(jax.ShapeDtypeStruct((rows, D_MODEL), F32), jax.ShapeDtypeStruct((rows, D_MODEL), BF16)),
        grid=(rows // tm,),
        in_specs=[
            pl.BlockSpec((tm, SSM_CH), row),
            pl.BlockSpec((tm, GQA_HEADS * GQA_HEAD_DIM), row),
            pl.BlockSpec((tm, MLA_HEADS * MLA_V), row),
            pl.BlockSpec((tm, D_MODEL), row),
            _mod_spec(latent, tm, layer),
            pl.BlockSpec((D_MODEL, D_MODEL), lambda i: (0, 0)),
            _lspec((1, D_MODEL), layer),
        ],
        out_specs=(pl.BlockSpec((tm, D_MODEL), row), pl.BlockSpec((tm, D_MODEL), row)),
        compiler_params=_cparams(("parallel",)),
        name="out_proj",
    )(ya, yb, yc, x2d, mod, w_out_l, lw['norm_mlp'])


def _mlp_kernel(cast_next, *refs):
    if cast_next:
        (h2_ref, x1_ref, mod_ref, w1_ref, w2_ref, n1_ref, n2_ref, n3_ref,
         o_ref, c1_ref, c2_ref, c3_ref) = refs
        c1_ref[...] = n1_ref[...].astype(BF16)
        c2_ref[...] = n2_ref[...].astype(BF16)
        c3_ref[...] = n3_ref[...].astype(BF16)
    else:
        h2_ref, x1_ref, mod_ref, w1_ref, w2_ref, o_ref = refs
    j = pl.program_id(1)

    @pl.when(j == 0)
    def _():
        o_ref[...] = jnp.zeros_like(o_ref)

    f = jnp.maximum(_dot(h2_ref[...], w1_ref[...]), 0.0)
    o_ref[...] += _dot((f * f).astype(BF16), w2_ref[...])

    @pl.when(j == pl.num_programs(1) - 1)
    def _():
        o_ref[...] = x1_ref[...] + mod_ref[5:6, :] * o_ref[...]


def _mlp(h2, x1, mod, w_ff, latent, tm, tf, layer, next_w=None):
    rows = h2.shape[0]
    ni, nj = rows // tm, D_FF // tf
    in_specs = [
        pl.BlockSpec((tm, D_MODEL), lambda i, j: (i, 0)),
        pl.BlockSpec((tm, D_MODEL), lambda i, j: (i, 0), pipeline_mode=pl.Buffered(1)),
        _mod_spec(latent, tm, layer),
        pl.BlockSpec((D_MODEL, tf), lambda i, j: (0, j)),
        pl.BlockSpec((tf, D_MODEL), lambda i, j: (j, 0)),
    ]
    out_shape = [jax.ShapeDtypeStruct((rows, D_MODEL), F32)]
    out_specs = [pl.BlockSpec((tm, D_MODEL), lambda i, j: (i, 0))]
    args = [h2, x1, mod, w_ff[0], w_ff[1]]
    if next_w is not None:
        step = lambda i, j: i * nj + j
        for w in next_w:
            _, r, c = w.shape
            rb = r // (ni * nj)
            in_specs.append(pl.BlockSpec((None, rb, c), lambda i, j: (layer + 1, step(i, j), 0)))
            out_shape.append(jax.ShapeDtypeStruct((r, c), BF16))
            out_specs.append(pl.BlockSpec((rb, c), lambda i, j: (step(i, j), 0)))
        args += list(next_w)
    outs = pl.pallas_call(
        functools.partial(_mlp_kernel, next_w is not None),
        out_shape=tuple(out_shape),
        grid=(ni, nj),
        in_specs=in_specs,
        out_specs=tuple(out_specs),
        compiler_params=_cparams(("arbitrary", "arbitrary")),
        name="mlp_cast" if next_w is not None else "mlp",
    )(*args)
    return outs[0], (tuple(outs[1:]) if next_w is not None else None)


def _permute_w_in(w):
    base = OFF_QCN
    ckv0 = base + MLA_HEADS * MLA_QK
    n = w.shape[0]
    qc = w[:, :, base:ckv0].reshape(n, D_MODEL, MLA_HEADS, MLA_QK)
    nope = qc[..., :MLA_NOPE].reshape(n, D_MODEL, MLA_HEADS * MLA_NOPE)
    rope = qc[..., MLA_NOPE:].reshape(n, D_MODEL, MLA_HEADS * MLA_ROPE)
    kr = w[:, :, ckv0 + MLA_KV_RANK:]
    parts = [w[:, :, :base], nope, rope, w[:, :, ckv0:ckv0 + MLA_KV_RANK], kr, kr]
    return jnp.concatenate(parts, axis=2).astype(BF16)


def _rope_tables(seq):
    t = jnp.arange(seq)
    row = (t // GRID_W).astype(F32)
    col = (t % GRID_W).astype(F32)

    def table(d):
        quarter = d // 4
        inv = ROPE_BASE ** (-(jnp.arange(quarter, dtype=F32) / quarter))
        ar = row[:, None] * inv[None, :]
        ac = col[:, None] * inv[None, :]
        cos = jnp.concatenate([jnp.cos(ar), jnp.cos(ar), jnp.cos(ac), jnp.cos(ac)], axis=-1)
        sin = jnp.concatenate([-jnp.sin(ar), jnp.sin(ar), -jnp.sin(ac), jnp.sin(ac)], axis=-1)
        reps = LANES // d
        return jnp.tile(cos, (1, reps)), jnp.tile(sin, (1, reps))

    cos_b, sin_b = table(GQA_HEAD_DIM)
    cos_c, sin_c = table(MLA_ROPE)
    return {'cos_b': cos_b, 'sin_b': sin_b, 'cos_c': cos_c, 'sin_c': sin_c}


def _ssm_params(lam_re, lam_im, log_dt, b_re, b_im, c_re, c_im):
    a = lam_re.astype(F32)
    w = lam_im.astype(F32)
    dt = jnp.exp(log_dt.astype(F32))[..., None]
    mag = jnp.exp(a * dt)
    lbr = mag * jnp.cos(w * dt)
    lbi = mag * jnp.sin(w * dt)
    den = a * a + w * w
    cr = (((lbr - 1.0) * a + lbi * w) / den)[..., None]
    ci = ((lbi * a - (lbr - 1.0) * w) / den)[..., None]
    bre = b_re.astype(F32)
    bim = b_im.astype(F32)
    bb_re = cr * bre - ci * bim
    bb_im = cr * bim + ci * bre
    lam_ri = jnp.stack([lbr.reshape(DEPTH, 2, SSM_N), lbi.reshape(DEPTH, 2, SSM_N)], axis=2)
    lam_b = jnp.broadcast_to(lam_ri[:, :, :, None, :], (DEPTH, 2, 2, SUBLANES, SSM_N))
    gh = SSM_GROUPS // 2
    eye = jnp.eye(gh, dtype=F32)[:, None, :, None]

    def blockdiag(x, rows_per_g, cols_per_g):
        x = x.reshape(DEPTH, 2, 2, gh, rows_per_g, 1, cols_per_g) * eye
        return x.reshape(DEPTH, 2, 2, gh * rows_per_g, gh * cols_per_g)

    def bmat(x):
        return blockdiag(jnp.swapaxes(x, -1, -2), SSM_GROUP, SSM_STATE)

    def cmat(x):
        return blockdiag(jnp.swapaxes(x, -1, -2), SSM_STATE, SSM_GROUP)

    b_cat = jnp.concatenate([bmat(bb_re), bmat(bb_im)], axis=-1).astype(BF16)
    return {'lam': lam_b, 'b': b_cat, 'c_re': cmat(c_re.astype(F32)).astype(BF16),
            'c_im': cmat(c_im.astype(F32)).astype(BF16)}


def _trunk_layer(x2d, mod, lw, sp, tabs, latent, ctx, layer, w_big, next_w=None):
    seq = DEC_SEQ if latent else SEQ
    bsz = x2d.shape[0] // seq
    u, qb, kb, vb, qc, ckv, kr, krg = _in_proj(x2d, mod, latent, lw, tabs, TM_PROJ, layer)

    u3 = u.reshape(bsz, seq, SSM_CH)
    ya, h_t = _ssm_scan(u3, ctx['h0'], sp, lw['ssm_d'], lw['ssm_w_glu'], layer)
    ya = ya.reshape(bsz * seq, SSM_CH)

    if latent:
        yb = _gqa_attn(qb, kb, vb, True, ctx['k'], ctx['v'], layer)
        yc = _mla_attn(qc, ckv, kr, krg, lw, True, ctx['ckv'], ctx['kr'], layer)
    else:
        yb = _gqa_attn(qb, kb, vb, False)
        yc = _mla_attn(qc, ckv, kr, krg, lw, False, layer=layer)

    x1, h2 = _out_proj(ya, yb, yc, x2d, mod, lw, w_big[2], latent, TM_PROJ, layer)
    x2, w_big_next = _mlp(h2, x1, mod, w_big[:2], latent, TM_MLP, TF_MLP, layer, next_w)
    return x2, (kb, vb, ckv, kr, h_t), w_big_next


def kernel(x_prompt, x_sample, cache_attn_k, cache_attn_v, cache_mla_ckv, cache_mla_krope, state_ssm, c, c_ctx, w_mod, b_mod, norm_mix, norm_mlp, w_in, gqa_q_norm, gqa_k_norm, mla_kv_norm, mla_q_norm, mla_k_norm, mla_w_uk, mla_w_uv, ssm_lam_re, ssm_lam_im, ssm_log_dt, ssm_b_re, ssm_b_im, ssm_c_re, ssm_c_im, ssm_d, ssm_w_glu, w_out, w_ff1, w_ff2):
    cvec = jnp.zeros((N_MOD, D_MODEL), F32).at[0].set(c_ctx).at[1:1 + DEC_BATCH].set(c)
    mod = _adaln(cvec, w_mod, b_mod.reshape(DEPTH, 1, 6 * D_MODEL)).reshape(DEPTH, N_MOD, 6, D_MODEL)

    tabs = _rope_tables(DEC_SEQ)
    cache_k = cache_attn_k.reshape(DEC_BATCH, DEPTH, PAST_LEN, GQA_KV_HEADS * GQA_HEAD_DIM)
    cache_v = cache_attn_v.reshape(DEC_BATCH, DEPTH, PAST_LEN, GQA_KV_HEADS * GQA_HEAD_DIM)

    xp = x_prompt.reshape(BATCH * SEQ, D_MODEL)
    xs = x_sample.reshape(DEC_BATCH * DEC_SEQ, D_MODEL)
    new_k, new_v, new_ckv, new_kr, new_ssm = [], [], [], [], []
    h0_zero = jnp.zeros((BATCH // SUBLANES, 2, 2, SUBLANES, SSM_N), F32)

    row3 = lambda a: a.reshape(DEPTH, 1, -1)
    dup = lambda a: jnp.concatenate([a, a], axis=-1)
    lw = {
        'norm_mix': row3(norm_mix), 'norm_mlp': row3(norm_mlp),
        'w_in': _permute_w_in(w_in),
        'gqa_q_norm': row3(gqa_q_norm), 'gqa_k_norm': row3(gqa_k_norm), 'mla_kv_norm': row3(mla_kv_norm),
        'mla_q_nope_g': row3(mla_q_norm[:, :MLA_NOPE]),
        'mla_q_rope_g': row3(dup(mla_q_norm[:, MLA_NOPE:])),
        'mla_k_nope_g': row3(mla_k_norm[:, :MLA_NOPE]),
        'mla_k_rope_g': row3(dup(mla_k_norm[:, MLA_NOPE:])),
        'mla_k_rope_g64': row3(mla_k_norm[:, MLA_NOPE:]),
        'mla_w_uk': mla_w_uk.reshape(DEPTH, MLA_KV_RANK, MLA_HEADS * MLA_NOPE).astype(BF16),
        'mla_w_uv': mla_w_uv.reshape(DEPTH, MLA_KV_RANK, MLA_HEADS * MLA_V).astype(BF16),
        'ssm_d': row3(ssm_d), 'ssm_w_glu': ssm_w_glu.astype(BF16),
    }
    sp = _ssm_params(ssm_lam_re, ssm_lam_im, ssm_log_dt, ssm_b_re, ssm_b_im, ssm_c_re, ssm_c_im)
    h0_lat = state_ssm.reshape(DEC_BATCH, DEPTH, 2, SSM_N, 2).transpose(1, 2, 4, 0, 3)[:, None]

    w_big = (w_ff1[0].astype(BF16), w_ff2[0].astype(BF16), w_out[0].astype(BF16))
    for l in range(DEPTH):
        next_w = (w_ff1, w_ff2, w_out) if l + 1 < DEPTH else None
        xp, (k, v, ckv_n, kr, h_t), w_big_next = _trunk_layer(xp, mod, lw, sp, tabs, False, {'h0': h0_zero}, l,
                                                             w_big, next_w)
        new_k.append(k.reshape(BATCH, SEQ, GQA_KV_HEADS, GQA_HEAD_DIM))
        new_v.append(v.reshape(BATCH, SEQ, GQA_KV_HEADS, GQA_HEAD_DIM))
        new_ckv.append(ckv_n.reshape(BATCH, SEQ, MLA_KV_RANK))
        new_kr.append(kr.reshape(BATCH, SEQ, MLA_ROPE))
        hs = h_t.transpose(0, 3, 1, 4, 2).reshape(BATCH, 2, SSM_GROUPS, SSM_STATE, 2)
        new_ssm.append(hs)

        ctx = {'k': cache_k, 'v': cache_v, 'ckv': cache_mla_ckv, 'kr': cache_mla_krope, 'h0': h0_lat[l]}
        xs, _, _ = _trunk_layer(xs, mod, lw, sp, tabs, True, ctx, l, w_big)
        w_big = w_big_next

    return (xp.reshape(BATCH, SEQ, D_MODEL), xs.reshape(DEC_BATCH, DEC_SEQ, D_MODEL),
            jnp.stack(new_k, axis=1), jnp.stack(new_v, axis=1), jnp.stack(new_ckv, axis=1),
            jnp.stack(new_kr, axis=1), jnp.stack(new_ssm, axis=1))
```
